```python
import math
import jax, jax.numpy as jnp
from jax import lax
import numpy as np

D_MODEL = 2048
BATCH = 1
SEQ = 8192
DEPTH = 1
DEC_BATCH = 128
DEC_SEQ = 8
PAST_LEN = 2048
PAGE_SIZE = 128

H_M = 8
DK_M = 128
DV_M = 128
QK_M = H_M * DK_M
W_M = H_M * DV_M
CONV_W = 4
CHUNK_M = 64
H_D = 8
DK_D = 64
DV_D = 128
QK_D = H_D * 2 * DK_D
W_D = H_D * DV_D
Q_BLOCK = 128
N_BUCKETS = 32
MAX_DIST = 128
D_FF = 4 * D_MODEL
EPS = 1e-6
IN_SPLITS = (2 * QK_M, W_M, W_M, H_M, H_M, QK_D, QK_D, W_D, D_MODEL, D_MODEL)
N_IN = 2 * QK_M + 2 * W_M + 2 * H_M + 2 * QK_D + W_D + 2 * D_MODEL

kernel_name = 'diff_mlstm_gated_hybrid_step'


def rmsnorm(x, w):
    xf = x.astype(jnp.float32)
    y = xf * lax.rsqrt(jnp.mean(xf * xf, axis=-1, keepdims=True) + EPS)
    return (y * w.astype(jnp.float32)).astype(x.dtype)


def split_in(z):
    idx = [int(i) for i in np.cumsum(IN_SPLITS)[:-1]]
    return jnp.split(z, idx, axis=-1)


def causal_conv(xcat, w, b):
    S = xcat.shape[1] - (CONV_W - 1)
    y = b
    for j in range(CONV_W):
        y = y + w[j] * xcat[:, j:j + S]
    return y


def mlstm_chunkwise(q, k, v, ig, lf, C0, n0, m0):
    B, S, H, DK = q.shape
    L = math.gcd(S, CHUNK_M)
    NC = S // L
    to_chunks = lambda a: jnp.moveaxis(a.reshape((B, NC, L) + a.shape[2:]), 1, 0)
    causal = jnp.tril(jnp.ones((L, L), dtype=bool))

    def step(carry, inp):
        C, n, m = carry
        qc, kc, vc, ic, fc = inp
        bT = jnp.cumsum(fc, axis=1).transpose(0, 2, 1)
        iT = ic.transpose(0, 2, 1)
        dlog = bT[:, :, :, None] - bT[:, :, None, :] + iT[:, :, None, :]
        dlog = jnp.where(causal, dlog, -jnp.inf)
        inter = bT + m[:, :, None]
        m_t = jnp.maximum(inter, jnp.max(dlog, axis=-1))
        dw = jnp.exp(dlog - m_t[..., None])
        iw = jnp.exp(inter - m_t)
        s = jnp.einsum('blhd,bshd->bhls', qc, kc) * dw
        num = (jnp.einsum('bhls,bshv->blhv', s, vc)
               + iw.transpose(0, 2, 1)[..., None] * jnp.einsum('bhvd,blhd->blhv', C, qc))
        den = jnp.sum(s, axis=-1) + iw * jnp.einsum('bhd,blhd->bhl', n, qc)
        den = jnp.maximum(jnp.abs(den), jnp.exp(-m_t))
        h = num / den.transpose(0, 2, 1)[..., None]
        bL = bT[:, :, -1]
        wlog = bL[:, :, None] - bT + iT
        m_new = jnp.maximum(bL + m, jnp.max(wlog, axis=-1))
        ws = jnp.exp(wlog - m_new[..., None])
        decay = jnp.exp(bL + m - m_new)
        C_new = decay[..., None, None] * C + jnp.einsum('bhs,bshv,bshd->bhvd', ws, vc, kc)
        n_new = decay[..., None] * n + jnp.einsum('bhs,bshd->bhd', ws, kc)
        return (C_new, n_new, m_new), h

    (C, n, m), hs = lax.scan(step, (C0, n0, m0),
                             (to_chunks(q), to_chunks(k), to_chunks(v), to_chunks(ig), to_chunks(lf)))
    h = jnp.moveaxis(hs, 0, 1).reshape(B, S, H, v.shape[-1])
    return h, C, n, m


def rel_bucket(dist):
    n = jnp.maximum(dist, 0)
    max_exact = N_BUCKETS // 2
    nf = jnp.maximum(n, 1).astype(jnp.float32)
    large = max_exact + (jnp.log(nf / max_exact) / math.log(MAX_DIST / max_exact)
                         * (N_BUCKETS - max_exact)).astype(jnp.int32)
    large = jnp.minimum(large, N_BUCKETS - 1)
    return jnp.where(n < max_exact, n, large)


def diff_attend_block(q, k, v, qpos, kpos, lam, rel_bias):
    f32 = jnp.float32
    logits = jnp.einsum('bqhcd,bkhcd->bhcqk', q.astype(f32), k.astype(f32)) * (DK_D ** -0.5)
    bias = rel_bias.astype(f32)[rel_bucket(qpos[:, None] - kpos[None, :])]
    logits = logits + jnp.transpose(bias, (2, 0, 1))[None, :, None]
    logits = jnp.where(kpos[None, :] <= qpos[:, None], logits, -jnp.inf)
    p = jax.nn.softmax(logits, axis=-1)
    w = p[:, :, 0] - lam * p[:, :, 1]
    return jnp.einsum('bhqk,bkhv->bqhv', w.astype(v.dtype), v)


def diff_attention(q, k, v, q_start, lam, rel_bias):
    B, Q = q.shape[:2]
    blk = math.gcd(Q, Q_BLOCK)
    nb = Q // blk
    kpos = jnp.arange(k.shape[1], dtype=jnp.int32)
    qb = jnp.moveaxis(q.reshape(B, nb, blk, H_D, 2, DK_D), 1, 0)

    def one(args):
        q_blk, i = args
        qpos = q_start + i * blk + jnp.arange(blk, dtype=jnp.int32)
        return diff_attend_block(q_blk, k, v, qpos, kpos, lam, rel_bias)

    out = lax.map(one, (qb, jnp.arange(nb, dtype=jnp.int32)))
    return jnp.moveaxis(out, 0, 1).reshape(B, Q, H_D, DV_D)


def trunk_layer(x, conv_prev, C0, n0, m0, attend, lw, layer):
    f32 = jnp.float32
    B, S, _ = x.shape
    xn = rmsnorm(x, lw['norm1'])
    z = xn @ lw['w_in']
    qk_pre, v_m, o_m, i_pre, f_pre, q_d, k_d, v_d, g_a, g_b = split_in(z)
    qk_cat = jnp.concatenate([conv_prev.astype(qk_pre.dtype), qk_pre], axis=1)
    qk = jax.nn.silu(causal_conv(qk_cat, lw['conv_w'], lw['conv_b']))
    new_conv = qk_cat[:, S:]
    q_m = qk[..., :QK_M].reshape(B, S, H_M, DK_M).astype(f32)
    k_m = qk[..., QK_M:].reshape(B, S, H_M, DK_M).astype(f32) * (DK_M ** -0.5)
    v_m = v_m.reshape(B, S, H_M, DV_M).astype(f32)
    ig = (i_pre + lw['b_i']).astype(f32)
    lf = jax.nn.log_sigmoid((f_pre + lw['b_f']).astype(f32))
    h_m, C, n, m = mlstm_chunkwise(q_m, k_m, v_m, ig, lf, C0.astype(f32), n0.astype(f32), m0.astype(f32))
    h_m = rmsnorm(h_m, lw['hnorm']).astype(x.dtype) * jax.nn.sigmoid(o_m).reshape(B, S, H_M, DV_M)
    lam_init = 0.8 - 0.6 * math.exp(-0.3 * layer)
    lam = (jnp.exp(jnp.sum(lw['lq1'].astype(f32) * lw['lk1'].astype(f32)))
           - jnp.exp(jnp.sum(lw['lq2'].astype(f32) * lw['lk2'].astype(f32))) + lam_init)
    q_d = q_d.reshape(B, S, H_D, 2, DK_D)
    k_rows = k_d.reshape(B, S, H_D, 2 * DK_D)
    v_rows = v_d.reshape(B, S, H_D, DV_D)
    att = attend(q_d, k_rows, v_rows, lam)
    att = rmsnorm(att, lw['subln']) * (1.0 - lam_init)
    ya = h_m.reshape(B, S, W_M) @ lw['w_a']
    yb = att.reshape(B, S, W_D) @ lw['w_b']
    u = jax.nn.sigmoid(g_a) * ya + jax.nn.sigmoid(g_b) * yb
    x = x + u @ lw['w_out']
    hid = jnp.square(jax.nn.relu(rmsnorm(x, lw['norm2']) @ lw['w_ff1']))
    x = x + hid @ lw['w_ff2']
    return x, (k_rows, v_rows, C, n, m, new_conv)


def setup_inputs(seed: int = 0) -> dict:
    key = jax.random.key(seed)
    ks = jax.random.split(key, 32)
    f32 = jnp.float32
    n_pages = PAST_LEN // PAGE_SIZE
    n_used = DEC_BATCH * n_pages
    n_pool = n_used + n_used // 4

    def nrm(k, shape, s=1.0):
        return s * jax.random.normal(k, shape, f32)

    page_table = jax.random.permutation(ks[2], n_pool)[:n_used].reshape(DEC_BATCH, n_pages).astype(jnp.int32)
    return {
        'x_prompt': nrm(ks[0], (BATCH, SEQ, D_MODEL)),
        'x_sample': nrm(ks[1], (DEC_BATCH, DEC_SEQ, D_MODEL)),
        'cache_k': nrm(ks[3], (DEPTH, n_pool, PAGE_SIZE, H_D, 2 * DK_D)),
        'cache_v': nrm(ks[4], (DEPTH, n_pool, PAGE_SIZE, H_D, DV_D)),
        'page_table': page_table,
        'state_C': nrm(ks[5], (DEPTH, DEC_BATCH, H_M, DV_M, DK_M), 0.3),
        'state_n': nrm(ks[6], (DEPTH, DEC_BATCH, H_M, DK_M), 0.3),
        'state_m': nrm(ks[7], (DEPTH, DEC_BATCH, H_M), 0.5),
        'state_conv': nrm(ks[8], (DEPTH, DEC_BATCH, CONV_W - 1, 2 * QK_M)),
        'norm1_w': 1.0 + nrm(ks[9], (DEPTH, D_MODEL), 0.02),
        'w_in': nrm(ks[10], (DEPTH, D_MODEL, N_IN), D_MODEL ** -0.5),
        'b_i': nrm(ks[11], (DEPTH, H_M), 0.1),
        'b_f': 3.0 + nrm(ks[12], (DEPTH, H_M), 0.5),
        'conv_w': nrm(ks[13], (DEPTH, CONV_W, 2 * QK_M), CONV_W ** -0.5),
        'conv_b': nrm(ks[14], (DEPTH, 2 * QK_M), 0.02),
        'hnorm_w': 1.0 + nrm(ks[15], (DEPTH, DV_M), 0.02),
        'lambda_q1': nrm(ks[16], (DEPTH, DK_D), 0.1),
        'lambda_k1': nrm(ks[17], (DEPTH, DK_D), 0.1),
        'lambda_q2': nrm(ks[18], (DEPTH, DK_D), 0.1),
        'lambda_k2': nrm(ks[19], (DEPTH, DK_D), 0.1),
        'subln_w': 1.0 + nrm(ks[20], (DEPTH, DV_D), 0.02),
        'rel_bias': nrm(ks[21], (N_BUCKETS, H_D), 0.5),
        'w_a': nrm(ks[22], (DEPTH, W_M, D_MODEL), W_M ** -0.5),
        'w_b': nrm(ks[23], (DEPTH, W_D, D_MODEL), W_D ** -0.5),
        'w_out': nrm(ks[24], (DEPTH, D_MODEL, D_MODEL), D_MODEL ** -0.5),
        'norm2_w': 1.0 + nrm(ks[25], (DEPTH, D_MODEL), 0.02),
        'w_ff1': nrm(ks[26], (DEPTH, D_MODEL, D_FF), D_MODEL ** -0.5),
        'w_ff2': nrm(ks[27], (DEPTH, D_FF, D_MODEL), D_FF ** -0.5),
        'final_norm_w': 1.0 + nrm(ks[28], (D_MODEL,), 0.02),
    }


def reference(x_prompt, x_sample, cache_k, cache_v, page_table, state_C, state_n, state_m, state_conv,
              norm1_w, w_in, b_i, b_f, conv_w, conv_b, hnorm_w, lambda_q1, lambda_k1, lambda_q2, lambda_k2,
              subln_w, rel_bias, w_a, w_b, w_out, norm2_w, w_ff1, w_ff2, final_norm_w):
    B = x_prompt.shape[0]
    dt = x_prompt.dtype
    past = page_table.shape[1] * cache_k.shape[2]

    def prompt_attend(q, k, v, lam):
        kk = k.reshape(k.shape[0], k.shape[1], H_D, 2, DK_D)
        return diff_attention(q, kk, v, 0, lam, rel_bias)

    def make_sample_attend(ck, cv):
        def attend(q, k, v, lam):
            def one(args):
                qb, kb, vb, pt = args
                kp = ck[pt].reshape(past, H_D, 2 * DK_D).astype(kb.dtype)
                vp = cv[pt].reshape(past, H_D, DV_D).astype(vb.dtype)
                kall = jnp.concatenate([kp, kb], axis=0).reshape(1, past + kb.shape[0], H_D, 2, DK_D)
                vall = jnp.concatenate([vp, vb], axis=0)[None]
                return diff_attention(qb[None], kall, vall, past, lam, rel_bias)[0]
            return lax.map(one, (q, k, v, page_table))
        return attend

    xp, xs = x_prompt, x_sample
    st_p, st_s = [], []
    for layer in range(DEPTH):
        lw = {'norm1': norm1_w[layer], 'w_in': w_in[layer], 'b_i': b_i[layer], 'b_f': b_f[layer],
              'conv_w': conv_w[layer], 'conv_b': conv_b[layer], 'hnorm': hnorm_w[layer],
              'lq1': lambda_q1[layer], 'lk1': lambda_k1[layer], 'lq2': lambda_q2[layer], 'lk2': lambda_k2[layer],
              'subln': subln_w[layer], 'w_a': w_a[layer], 'w_b': w_b[layer], 'w_out': w_out[layer],
              'norm2': norm2_w[layer], 'w_ff1': w_ff1[layer], 'w_ff2': w_ff2[layer]}
        conv0 = jnp.zeros((B, CONV_W - 1, 2 * QK_M), dt)
        C0 = jnp.zeros((B, H_M, DV_M, DK_M), jnp.float32)
        n0 = jnp.zeros((B, H_M, DK_M), jnp.float32)
        m0 = jnp.zeros((B, H_M), jnp.float32)
        xp, sp = trunk_layer(xp, conv0, C0, n0, m0, prompt_attend, lw, layer)
        xs, ss = trunk_layer(xs, state_conv[layer], state_C[layer], state_n[layer], state_m[layer],
                             make_sample_attend(cache_k[layer], cache_v[layer]), lw, layer)
        st_p.append(sp)
        st_s.append(ss)

    y_prompt = rmsnorm(xp, final_norm_w)
    y_sample = rmsnorm(xs, final_norm_w)
    k_prompt = jnp.stack([s[0] for s in st_p])
    v_prompt = jnp.stack([s[1] for s in st_p])
    C_prompt = jnp.stack([s[2] for s in st_p])
    n_prompt = jnp.stack([s[3] for s in st_p])
    m_prompt = jnp.stack([s[4] for s in st_p])
    conv_prompt = jnp.stack([s[5] for s in st_p])
    k_sample = jnp.stack([s[0] for s in st_s])
    v_sample = jnp.stack([s[1] for s in st_s])
    C_sample = jnp.stack([s[2] for s in st_s])
    n_sample = jnp.stack([s[3] for s in st_s])
    m_sample = jnp.stack([s[4] for s in st_s])
    conv_sample = jnp.stack([s[5] for s in st_s])
    return (y_prompt, y_sample, k_prompt, v_prompt, C_prompt, n_prompt, m_prompt, conv_prompt,
            k_sample, v_sample, C_sample, n_sample, m_sample, conv_sample)
```

```python
import functools
import math

import numpy as np
import jax
import jax.numpy as jnp
from jax import lax
from jax.experimental import pallas as pl
from jax.experimental.pallas import tpu as pltpu

F32 = jnp.float32
BF16 = jnp.bfloat16
HIGHEST = lax.Precision.HIGHEST

H_M = 8
DK_M = 128
DV_M = 128
QK_M = H_M * DK_M
W_M = H_M * DV_M
CONV_W = 4
H_D = 8
DK_D = 64
DV_D = 128
QK_D = H_D * 2 * DK_D
W_D = H_D * DV_D
N_BUCKETS = 32
MAX_DIST = 128
EPS = 1e-6
LAM_INIT = 0.8 - 0.6 * math.exp(-0.3 * 0)
NEG = -1e30

ZC_QK = 0
ZC_VM = 2 * QK_M
ZC_OM = ZC_VM + W_M
ZC_QD = ZC_OM + W_M
ZC_KD = ZC_QD + QK_D
ZC_VD = ZC_KD + QK_D
ZC_GA = ZC_VD + W_D

LANES = 128
SUBLANES = 8
VMEM_LIMIT = 56 * 1024 * 1024

SEQ_BLOCK = 16


def _params(*sem):
    return pltpu.CompilerParams(dimension_semantics=sem, vmem_limit_bytes=VMEM_LIMIT)


def _pick(n, prefs):
    for p in prefs:
        if n % p == 0:
            return p
    return n


def _sigmoid(x):
    return 1.0 / (1.0 + jnp.exp(-x))


def _log_sigmoid(x):
    return jnp.minimum(x, 0.0) - jnp.log(1.0 + jnp.exp(-jnp.abs(x)))


def _dot(a, b):
    return jnp.dot(a, b, preferred_element_type=F32)


def _dot_nt(a, b):
    return lax.dot_general(a, b, (((1,), (1,)), ((), ())), preferred_element_type=F32)


def _dot_tn(a, b):
    return lax.dot_general(a, b, (((0,), (0,)), ((), ())), preferred_element_type=F32)


def _dot_exact(a, b):
    return jnp.dot(a, b, preferred_element_type=F32, precision=HIGHEST)


def _in_proj_kernel(x_ref, nw_ref, w_ref, wgc_ref, wgr_ref, z_ref, gc_ref, gr_ref, xn_ref):
    @pl.when(pl.program_id(1) == 0)
    def _():
        x = x_ref[...]
        ms = jnp.mean(x * x, axis=-1, keepdims=True)
        xn = (x * lax.rsqrt(ms + EPS) * nw_ref[...]).astype(BF16)
        xn_ref[...] = xn
        gc_ref[...] = _dot(xn, wgc_ref[...])
        gr_ref[...] = _dot_nt(wgr_ref[...], xn)

    z_ref[...] = _dot(xn_ref[...], w_ref[...])


def _in_proj(x, norm_w, w_main, w_gate_col, w_gate_row):
    R, D = x.shape
    NZ = w_main.shape[1]
    tm = _pick(R, (1024, 512, 256, 128))
    tn = _pick(NZ, (512, 256, 128))
    return pl.pallas_call(
        _in_proj_kernel,
        grid=(R // tm, NZ // tn),
        in_specs=[
            pl.BlockSpec((tm, D), lambda i, j: (i, 0)),
            pl.BlockSpec((1, D), lambda i, j: (0, 0)),
            pl.BlockSpec((D, tn), lambda i, j: (0, j)),
            pl.BlockSpec((D, 2 * LANES), lambda i, j: (0, 0)),
            pl.BlockSpec((2 * SUBLANES, D), lambda i, j: (0, 0)),
        ],
        out_specs=[
            pl.BlockSpec((tm, tn), lambda i, j: (i, j)),
            pl.BlockSpec((tm, 2 * LANES), lambda i, j: (i, 0)),
            pl.BlockSpec((2 * SUBLANES, tm), lambda i, j: (0, i)),
        ],
        out_shape=[
            jax.ShapeDtypeStruct((R, NZ), F32),
            jax.ShapeDtypeStruct((R, 2 * LANES), F32),
            jax.ShapeDtypeStruct((2 * SUBLANES, R), F32),
        ],
        scratch_shapes=[pltpu.VMEM((tm, D), BF16)],
        compiler_params=_params("parallel", "arbitrary"),
        name="in_proj",
    )(x, norm_w, w_main, w_gate_col, w_gate_row)


def _conv_silu(x, hist, hist_shift, cw_ref, cb_ref, row_in_seq):
    acc = cb_ref[...] + cw_ref[CONV_W - 1:CONV_W, :] * x
    for j in range(1, CONV_W):
        xr = pltpu.roll(x, j, axis=0)
        hr = pltpu.roll(hist, (j + hist_shift) % hist.shape[0], axis=0)
        if hist.shape[0] != x.shape[0]:
            first = jnp.where(row_in_seq[0:SUBLANES] < j, hr, xr[0:SUBLANES])
            xs = jnp.concatenate([first, xr[SUBLANES:]], axis=0)
        else:
            xs = jnp.where(row_in_seq < j, hr, xr)
        acc = acc + cw_ref[CONV_W - 1 - j:CONV_W - j, :] * xs
    return acc * _sigmoid(acc)


def _mlstm_intra(qb, kb, vb, mask, bt_c, bt_r, ig_r, inter_c):
    dlog = jnp.where(mask, bt_c - bt_r + ig_r, -jnp.inf)
    m_t = jnp.maximum(inter_c, jnp.max(dlog, axis=1, keepdims=True))
    dw = jnp.exp(dlog - m_t)
    iw = jnp.exp(inter_c - m_t)
    s = _dot_nt(qb, kb) * dw
    sv = _dot(s.astype(BF16), vb)
    return sv, jnp.sum(s, axis=1, keepdims=True), m_t, iw


def _head_out(num, den, m_t, o, hw_ref):
    den = jnp.maximum(jnp.abs(den), jnp.exp(-m_t))
    h = num / den
    hn = h * lax.rsqrt(jnp.mean(h * h, axis=-1, keepdims=True) + EPS) * hw_ref[...]
    return (hn * _sigmoid(o)).astype(BF16)


def _mlstm_prompt_kernel(zq_ref, zv_ref, zo_ref, gc_ref, gr_ref, cw_ref, cb_ref, bc_ref, br_ref, hw_ref,
                         h_ref, c_out, n_out, m_out, conv_out,
                         c_s, n_s, m_s, hist_s):
    c = pl.program_id(0)
    L = zq_ref.shape[0]

    @pl.when(c == 0)
    def _():
        c_s[...] = jnp.zeros_like(c_s)
        n_s[...] = jnp.zeros_like(n_s)
        m_s[...] = jnp.zeros_like(m_s)
        hist_s[...] = jnp.zeros_like(hist_s)

    x = zq_ref[...]
    row = lax.broadcasted_iota(jnp.int32, (L, 1), 0)
    qk = _conv_silu(x, hist_s[...], 0, cw_ref, cb_ref, row)
    hist_s[...] = x[L - SUBLANES:L, :]
    conv_out[...] = x[L - SUBLANES:L, :]

    gcol = gc_ref[...] + bc_ref[...]
    grow = gr_ref[...] + br_ref[...]
    ig_c = gcol[:, 0:LANES]
    lf_c = _log_sigmoid(gcol[:, LANES:2 * LANES])
    ig_r = grow[0:SUBLANES, :]
    lf_r = _log_sigmoid(grow[SUBLANES:2 * SUBLANES, :])
    ri = lax.broadcasted_iota(jnp.int32, (L, L), 0)
    ci = lax.broadcasted_iota(jnp.int32, (L, L), 1)
    mask = ci <= ri
    bt_c = _dot_exact(mask.astype(F32), lf_c)
    bt_r = _dot_exact(lf_r, (ri <= ci).astype(F32))
    m_prev = m_s[...]
    inter = bt_c + m_prev
    b_last = bt_c[L - 1:L, :]
    wlog = b_last - bt_c + ig_c
    m_new = jnp.maximum(b_last + m_prev, jnp.max(wlog, axis=0, keepdims=True))
    ws = jnp.exp(wlog - m_new)
    decay = jnp.exp(b_last + m_prev - m_new)
    m_s[...] = m_new
    m_out[...] = m_new

    for h in range(H_M):
        q = qk[:, h * DK_M:(h + 1) * DK_M]
        k = qk[:, QK_M + h * DK_M:QK_M + (h + 1) * DK_M] * (DK_M ** -0.5)
        v = zv_ref[:, h * DV_M:(h + 1) * DV_M]
        qb, kb, vb = q.astype(BF16), k.astype(BF16), v.astype(BF16)
        sv, ssum, m_t, iw = _mlstm_intra(qb, kb, vb, mask, bt_c[:, h:h + 1], bt_r[h:h + 1, :],
                                         ig_r[h:h + 1, :], inter[:, h:h + 1])
        C = c_s[h]
        n_row = n_s[h:h + 1, :]
        num = sv + iw * _dot_nt(qb, C.astype(BF16))
        den = ssum + iw * jnp.sum(q * n_row, axis=1, keepdims=True)
        h_ref[:, h * DV_M:(h + 1) * DV_M] = _head_out(num, den, m_t, zo_ref[:, h * DV_M:(h + 1) * DV_M], hw_ref)
        ws_h = ws[:, h:h + 1]
        dc = decay[:, h:h + 1]
        c_new = dc * C + _dot_tn((v * ws_h).astype(BF16), kb)
        n_new = dc * n_row + jnp.sum(ws_h * k, axis=0, keepdims=True)
        c_s[h] = c_new
        n_s[h:h + 1, :] = n_new
        c_out[h] = c_new
        n_out[h:h + 1, :] = n_new


def _mlstm_prompt(z, gc, gr, conv_w, conv_b, bias_c, bias_r, hnorm_w, S):
    L = _pick(S, (256, 128))
    nz = lambda col, width: col // width
    return pl.pallas_call(
        _mlstm_prompt_kernel,
        grid=(S // L,),
        in_specs=[
            pl.BlockSpec((L, 2 * QK_M), lambda c: (c, nz(ZC_QK, 2 * QK_M))),
            pl.BlockSpec((L, W_M), lambda c: (c, nz(ZC_VM, W_M))),
            pl.BlockSpec((L, W_M), lambda c: (c, nz(ZC_OM, W_M))),
            pl.BlockSpec((L, 2 * LANES), lambda c: (c, 0)),
            pl.BlockSpec((2 * SUBLANES, L), lambda c: (0, c)),
            pl.BlockSpec((CONV_W, 2 * QK_M), lambda c: (0, 0)),
            pl.BlockSpec((1, 2 * QK_M), lambda c: (0, 0)),
            pl.BlockSpec((1, 2 * LANES), lambda c: (0, 0)),
            pl.BlockSpec((2 * SUBLANES, 1), lambda c: (0, 0)),
            pl.BlockSpec((1, DV_M), lambda c: (0, 0)),
        ],
        out_specs=[
            pl.BlockSpec((L, W_M), lambda c: (c, 0)),
            pl.BlockSpec((H_M, DV_M, DK_M), lambda c: (0, 0, 0)),
            pl.BlockSpec((H_M, DK_M), lambda c: (0, 0)),
            pl.BlockSpec((1, LANES), lambda c: (0, 0)),
            pl.BlockSpec((SUBLANES, 2 * QK_M), lambda c: (0, 0)),
        ],
        out_shape=[
            jax.ShapeDtypeStruct((S, W_M), BF16),
            jax.ShapeDtypeStruct((H_M, DV_M, DK_M), F32),
            jax.ShapeDtypeStruct((H_M, DK_M), F32),
            jax.ShapeDtypeStruct((1, LANES), F32),
            jax.ShapeDtypeStruct((SUBLANES, 2 * QK_M), F32),
        ],
        scratch_shapes=[
            pltpu.VMEM((H_M, DV_M, DK_M), F32),
            pltpu.VMEM((H_M, DK_M), F32),
            pltpu.VMEM((1, LANES), F32),
            pltpu.VMEM((SUBLANES, 2 * QK_M), F32),
        ],
        compiler_params=_params("arbitrary"),
        name="mlstm_prompt",
    )(z, z, z, gc, gr, conv_w, conv_b, bias_c, bias_r, hnorm_w)


def _mlstm_sample_kernel(zq_ref, zv_ref, zo_ref, gc_ref, gr_ref, hist_ref, c0_ref, n0_ref, m0_ref,
                         cw_ref, cb_ref, bc_ref, br_ref, hw_ref,
                         h_ref, c_out, n_out, m_out, *, T):
    L = zq_ref.shape[0]
    NB = L // T
    x = zq_ref[...]
    ri = lax.broadcasted_iota(jnp.int32, (L, L), 0)
    ci = lax.broadcasted_iota(jnp.int32, (L, L), 1)
    same = (ri // T) == (ci // T)
    mask = same & (ci <= ri)
    row_t = lax.broadcasted_iota(jnp.int32, (L, 1), 0) % T
    qk = _conv_silu(x, hist_ref[...], L - T, cw_ref, cb_ref, row_t)

    gcol = gc_ref[...] + bc_ref[...]
    grow = gr_ref[...] + br_ref[...]
    ig_c = gcol[:, 0:LANES]
    lf_c = _log_sigmoid(gcol[:, LANES:2 * LANES])
    ig_r = grow[0:SUBLANES, :]
    lf_r = _log_sigmoid(grow[SUBLANES:2 * SUBLANES, :])
    bt_c = _dot_exact(mask.astype(F32), lf_c)
    bt_r = _dot_exact(lf_r, (same & (ri <= ci)).astype(F32))
    m_prev = m0_ref[...]
    last = same & (ci % T == T - 1)
    b_last = _dot_exact(last.astype(F32), bt_c)
    inter = bt_c + m_prev
    wlog = b_last - bt_c + ig_c
    wmax = jnp.max(wlog.reshape(NB, T, LANES), axis=1, keepdims=True)
    wmax = jnp.broadcast_to(wmax, (NB, T, LANES)).reshape(L, LANES)
    m_new = jnp.maximum(b_last + m_prev, wmax)
    ws = jnp.exp(wlog - m_new)
    decay = jnp.exp(b_last + m_prev - m_new)
    m_out[...] = m_new

    lane_seq = lax.broadcasted_iota(jnp.int32, (L, NB * DV_M), 1) // DV_M
    row_seq = lax.broadcasted_iota(jnp.int32, (L, NB * DV_M), 0) // T
    blockdiag = lane_seq == row_seq

    for h in range(H_M):
        q = qk[:, h * DK_M:(h + 1) * DK_M]
        k = qk[:, QK_M + h * DK_M:QK_M + (h + 1) * DK_M] * (DK_M ** -0.5)
        v = zv_ref[:, h * DV_M:(h + 1) * DV_M]
        qb, kb, vb = q.astype(BF16), k.astype(BF16), v.astype(BF16)
        sv, ssum, m_t, iw = _mlstm_intra(qb, kb, vb, mask, bt_c[:, h:h + 1], bt_r[h:h + 1, :],
                                         ig_r[h:h + 1, :], inter[:, h:h + 1])
        C = c0_ref[:, h]
        c_flat = C.reshape(NB * DV_M, DK_M)
        qc_all = _dot_nt(qb, c_flat.astype(BF16))
        qc = jnp.concatenate([qc_all[b * T:(b + 1) * T, b * DV_M:(b + 1) * DV_M] for b in range(NB)], axis=0)
        n_rows = jnp.broadcast_to(n0_ref[:, h:h + 1, :], (NB, T, DK_M)).reshape(L, DK_M)
        num = sv + iw * qc
        den = ssum + iw * jnp.sum(q * n_rows, axis=1, keepdims=True)
        h_ref[:, h * DV_M:(h + 1) * DV_M] = _head_out(num, den, m_t, zo_ref[:, h * DV_M:(h + 1) * DV_M], hw_ref)
        ws_h = ws[:, h:h + 1]
        vw = v * ws_h
        vw_exp = jnp.where(blockdiag, jnp.concatenate([vw] * NB, axis=1), 0.0).astype(BF16)
        upd = _dot_tn(vw_exp, kb).reshape(NB, DV_M, DK_M)
        dc = decay[:, h:h + 1].reshape(NB, T, 1)[:, 0:1, :]
        c_out[:, h] = dc * C + upd
        kw = (ws_h * k).reshape(NB, T, DK_M)
        n_out[:, h:h + 1, :] = dc * n0_ref[:, h:h + 1, :] + jnp.sum(kw, axis=1, keepdims=True)


def _mlstm_sample(z, gc, gr, hist, c0, n0, m0p, conv_w, conv_b, bias_c, bias_r, hnorm_w, S, B, T):
    NB = SEQ_BLOCK
    L = NB * T
    assert L == LANES and B % NB == 0 and S % L == 0
    r0 = S // L
    nz = lambda col, width: col // width
    return pl.pallas_call(
        functools.partial(_mlstm_sample_kernel, T=T),
        grid=(B // NB,),
        in_specs=[
            pl.BlockSpec((L, 2 * QK_M), lambda i: (r0 + i, nz(ZC_QK, 2 * QK_M))),
            pl.BlockSpec((L, W_M), lambda i: (r0 + i, nz(ZC_VM, W_M))),
            pl.BlockSpec((L, W_M), lambda i: (r0 + i, nz(ZC_OM, W_M))),
            pl.BlockSpec((L, 2 * LANES), lambda i: (r0 + i, 0)),
            pl.BlockSpec((2 * SUBLANES, L), lambda i: (0, r0 + i)),
            pl.BlockSpec((L, 2 * QK_M), lambda i: (i, 0)),
            pl.BlockSpec((NB, H_M, DV_M, DK_M), lambda i: (i, 0, 0, 0)),
            pl.BlockSpec((NB, H_M, DK_M), lambda i: (i, 0, 0)),
            pl.BlockSpec((L, LANES), lambda i: (i, 0)),
            pl.BlockSpec((CONV_W, 2 * QK_M), lambda i: (0, 0)),
            pl.BlockSpec((1, 2 * QK_M), lambda i: (0, 0)),
            pl.BlockSpec((1, 2 * LANES), lambda i: (0, 0)),
            pl.BlockSpec((2 * SUBLANES, 1), lambda i: (0, 0)),
            pl.BlockSpec((1, DV_M), lambda i: (0, 0)),
        ],
        out_specs=[
            pl.BlockSpec((L, W_M), lambda i: (i, 0)),
            pl.BlockSpec((NB, H_M, DV_M, DK_M), lambda i: (i, 0, 0, 0)),
            pl.BlockSpec((NB, H_M, DK_M), lambda i: (i, 0, 0)),
            pl.BlockSpec((L, LANES), lambda i: (i, 0)),
        ],
        out_shape=[
            jax.ShapeDtypeStruct((B * T, W_M), BF16),
            jax.ShapeDtypeStruct((B, H_M, DV_M, DK_M), F32),
            jax.ShapeDtypeStruct((B, H_M, DK_M), F32),
            jax.ShapeDtypeStruct((B * T, LANES), F32),
        ],
        compiler_params=_params("parallel"),
        name="mlstm_sample",
    )(z, z, z, gc, gr, hist, c0, n0, m0p, conv_w, conv_b, bias_c, bias_r, hnorm_w)


def _lambda(lq1, lk1, lq2, lk2):
    a = jnp.sum(lq1[...] * lk1[...], axis=-1, keepdims=True)
    b = jnp.sum(lq2[...] * lk2[...], axis=-1, keepdims=True)
    return jnp.exp(a) - jnp.exp(b) + LAM_INIT


def _subln(att, w_ref):
    y = att * lax.rsqrt(jnp.mean(att * att, axis=-1, keepdims=True) + EPS) * w_ref[...]
    return (y * (1.0 - LAM_INIT)).astype(BF16)


def _attn_prompt_kernel(q_ref, k_ref, v_ref, b0_ref, b1_ref, cf_ref, lq1, lk1, lq2, lk2, sw_ref,
                        o_ref, kb_s, vb_s, m_s, l_s, acc_s):
    qi = pl.program_id(1)
    T = q_ref.shape[0]

    @pl.when(qi == 0)
    def _():
        kb_s[...] = k_ref[...].astype(BF16)
        vb_s[...] = v_ref[...].astype(BF16)

    q = q_ref[...] * (DK_D ** -0.5)
    lane = lax.broadcasted_iota(jnp.int32, q.shape, 1)
    qpad = (jnp.where(lane < DK_D, q, 0.0).astype(BF16), jnp.where(lane >= DK_D, q, 0.0).astype(BF16))
    m_s[...] = jnp.full_like(m_s, -jnp.inf)
    l_s[...] = jnp.zeros_like(l_s)
    acc_s[...] = jnp.zeros_like(acc_s)

    def tile(kj, bias):
        start = pl.multiple_of(kj * T, T)
        kt = kb_s[pl.ds(start, T), :]
        vt = vb_s[pl.ds(start, T), :]
        for c in range(2):
            s = _dot_nt(qpad[c], kt) + bias
            m_old = m_s[c]
            m_new = jnp.maximum(m_old, jnp.max(s, axis=1, keepdims=True))
            alpha = jnp.exp(m_old - m_new)
            p = jnp.exp(s - m_new)
            l_s[c] = alpha * l_s[c] + jnp.sum(p, axis=1, keepdims=True)
            acc_s[c] = alpha * acc_s[c] + _dot(p.astype(BF16), vt)
            m_s[c] = m_new

    def far(kj, carry):
        tile(kj, cf_ref[0, :, 0:1])
        return carry

    lax.fori_loop(0, jnp.maximum(qi - 1, 0), far, 0)

    @pl.when(qi >= 1)
    def _():
        tile(qi - 1, b1_ref[0])

    tile(qi, b0_ref[0])

    lam = _lambda(lq1, lk1, lq2, lk2)
    att = acc_s[0] / l_s[0] - lam * (acc_s[1] / l_s[1])
    o_ref[...] = _subln(att, sw_ref)


def _attn_prompt(z, bias0, bias1, cfar, lq1, lk1, lq2, lk2, subln_w, S):
    T = bias0.shape[-1]
    hw = 2 * DK_D
    small = lambda shape: pl.BlockSpec(shape, lambda h, i: (0,) * len(shape))
    return pl.pallas_call(
        _attn_prompt_kernel,
        grid=(H_D, S // T),
        in_specs=[
            pl.BlockSpec((T, hw), lambda h, i: (i, ZC_QD // hw + h)),
            pl.BlockSpec((S, hw), lambda h, i: (0, ZC_KD // hw + h)),
            pl.BlockSpec((S, DV_D), lambda h, i: (0, ZC_VD // DV_D + h)),
            pl.BlockSpec((1, T, T), lambda h, i: (h, 0, 0)),
            pl.BlockSpec((1, T, T), lambda h, i: (h, 0, 0)),
            pl.BlockSpec((1, 1, LANES), lambda h, i: (h, 0, 0)),
            small((1, DK_D)), small((1, DK_D)), small((1, DK_D)), small((1, DK_D)),
            small((1, DV_D)),
        ],
        out_specs=pl.BlockSpec((T, DV_D), lambda h, i: (i, h)),
        out_shape=jax.ShapeDtypeStruct((S, W_D), BF16),
        scratch_shapes=[
            pltpu.VMEM((S, hw), BF16),
            pltpu.VMEM((S, DV_D), BF16),
            pltpu.VMEM((2, T, 1), F32),
            pltpu.VMEM((2, T, 1), F32),
            pltpu.VMEM((2, T, DV_D), F32),
        ],
        compiler_params=_params("parallel", "arbitrary"),
        name="attn_prompt",
    )(z, z, z, bias0, bias1, cfar, lq1, lk1, lq2, lk2, subln_w)


def _attn_sample_kernel(pt_ref, q_ref, kc_ref, vc_ref, kn_ref, vn_ref, bfar_ref, blast_ref, bnew_ref,
                        lq1, lk1, lq2, lk2, sw_ref, o_ref, qbd_s, m_s, l_s, acc_s):
    p = pl.program_id(1)
    n_pages = pl.num_programs(1) - 1
    HT = q_ref.shape[1]

    @pl.when(p == 0)
    def _():
        q = q_ref[0] * (DK_D ** -0.5)
        lane = lax.broadcasted_iota(jnp.int32, q.shape, 1)
        qbd_s[0:HT, :] = jnp.where(lane < DK_D, q, 0.0).astype(BF16)
        qbd_s[HT:2 * HT, :] = jnp.where(lane >= DK_D, q, 0.0).astype(BF16)
        m_s[...] = jnp.full_like(m_s, -jnp.inf)
        l_s[...] = jnp.zeros_like(l_s)
        acc_s[...] = jnp.zeros_like(acc_s)

    def step(kt, vt, bias):
        s = _dot_nt(qbd_s[...], kt.astype(BF16)) + bias
        m_old = m_s[...]
        m_new = jnp.maximum(m_old, jnp.max(s, axis=1, keepdims=True))
        alpha = jnp.exp(m_old - m_new)
        pr = jnp.exp(s - m_new)
        l_s[...] = alpha * l_s[...] + jnp.sum(pr, axis=1, keepdims=True)
        acc_s[...] = alpha * acc_s[...] + _dot(pr.astype(BF16), vt.astype(BF16))
        m_s[...] = m_new

    def page(bias):
        rows = kc_ref.shape[0] * kc_ref.shape[1]
        step(kc_ref[...].reshape(rows, 2 * DK_D), vc_ref[...].reshape(rows, DV_D), bias)

    @pl.when(p < n_pages - 1)
    def _():
        page(bfar_ref[...])

    @pl.when(p == n_pages - 1)
    def _():
        page(blast_ref[...])

    @pl.when(p == n_pages)
    def _():
        step(kn_ref[0], vn_ref[0], bnew_ref[...])
        lam = _lambda(lq1, lk1, lq2, lk2)
        r = acc_s[...] / l_s[...]
        o_ref[0] = _subln(r[0:HT] - lam * r[HT:2 * HT], sw_ref)


def _attn_sample(page_table, qs, cache_k, cache_v, kn, vn, bfar, blast, bnew, lq1, lk1, lq2, lk2, subln_w):
    B, HT, _ = qs.shape
    n_pages = page_table.shape[1]
    PG = cache_k.shape[2]
    last = n_pages - 1
    small = lambda shape: pl.BlockSpec(shape, lambda b, p, pt: (0,) * len(shape))
    grid_spec = pltpu.PrefetchScalarGridSpec(
        num_scalar_prefetch=1,
        grid=(B, n_pages + 1),
        in_specs=[
            pl.BlockSpec((1, HT, 2 * DK_D), lambda b, p, pt: (b, 0, 0)),
            pl.BlockSpec((None, None, PG, H_D, 2 * DK_D),
                         lambda b, p, pt: (0, pt[b, jnp.minimum(p, last)], 0, 0, 0)),
            pl.BlockSpec((None, None, PG, H_D, DV_D),
                         lambda b, p, pt: (0, pt[b, jnp.minimum(p, last)], 0, 0, 0)),
            pl.BlockSpec((1, HT, 2 * DK_D), lambda b, p, pt: (b, 0, 0)),
            pl.BlockSpec((1, HT, DV_D), lambda b, p, pt: (b, 0, 0)),
            small(bfar.shape), small(blast.shape), small(bnew.shape),
            small((1, DK_D)), small((1, DK_D)), small((1, DK_D)), small((1, DK_D)),
            small((1, DV_D)),
        ],
        out_specs=pl.BlockSpec((1, HT, DV_D), lambda b, p, pt: (b, 0, 0)),
        scratch_shapes=[
            pltpu.VMEM((2 * HT, 2 * DK_D), BF16),
            pltpu.VMEM((2 * HT, 1), F32),
            pltpu.VMEM((2 * HT, 1), F32),
            pltpu.VMEM((2 * HT, DV_D), F32),
        ],
    )
    return pl.pallas_call(
        _attn_sample_kernel,
        grid_spec=grid_spec,
        out_shape=jax.ShapeDtypeStruct((B, HT, DV_D), BF16),
        compiler_params=_params("parallel", "arbitrary"),
        name="attn_sample",
    )(page_table, qs, cache_k, cache_v, kn, vn, bfar, blast, bnew, lq1, lk1, lq2, lk2, subln_w)


def _merge_kernel(hm_ref, at_ref, wa_ref, wb_ref, ga_ref, gb_ref, u_ref):
    ya = _dot(hm_ref[...], wa_ref[...])
    yb = _dot(at_ref[...], wb_ref[...])
    u_ref[...] = (_sigmoid(ga_ref[...]) * ya + _sigmoid(gb_ref[...]) * yb).astype(BF16)


def _merge(hm, att, w_a, w_b, z, D):
    R = hm.shape[0]
    tm = _pick(R, (512, 256, 128))
    tn = _pick(D, (512, 256, 128))
    ga0, gb0 = ZC_GA // tn, (ZC_GA + D) // tn
    return pl.pallas_call(
        _merge_kernel,
        grid=(R // tm, D // tn),
        in_specs=[
            pl.BlockSpec((tm, W_M), lambda i, j: (i, 0)),
            pl.BlockSpec((tm, W_D), lambda i, j: (i, 0)),
            pl.BlockSpec((W_M, tn), lambda i, j: (0, j)),
            pl.BlockSpec((W_D, tn), lambda i, j: (0, j)),
            pl.BlockSpec((tm, tn), lambda i, j: (i, ga0 + j)),
            pl.BlockSpec((tm, tn), lambda i, j: (i, gb0 + j)),
        ],
        out_specs=pl.BlockSpec((tm, tn), lambda i, j: (i, j)),
        out_shape=jax.ShapeDtypeStruct((R, D), BF16),
        compiler_params=_params("parallel", "parallel"),
        name="merge",
    )(hm, att, w_a, w_b, z, z)


def _out_proj_kernel(u_ref, w_ref, x_ref, o_ref):
    o_ref[...] = x_ref[...] + _dot(u_ref[...], w_ref[...])


def _out_proj(u, w_out, x):
    R, D = x.shape
    tm = _pick(R, (512, 256, 128))
    tn = _pick(D, (512, 256, 128))
    return pl.pallas_call(
        _out_proj_kernel,
        grid=(R // tm, D // tn),
        in_specs=[
            pl.BlockSpec((tm, D), lambda i, j: (i, 0)),
            pl.BlockSpec((D, tn), lambda i, j: (0, j)),
            pl.BlockSpec((tm, tn), lambda i, j: (i, j)),
        ],
        out_specs=pl.BlockSpec((tm, tn), lambda i, j: (i, j)),
        out_shape=jax.ShapeDtypeStruct((R, D), F32),
        compiler_params=_params("parallel", "parallel"),
        name="out_proj",
    )(u, w_out, x)


def _ffn_kernel(x_ref, nw_ref, w1_ref, w2_ref, fw_ref, y_ref, xn_s, acc_s):
    f = pl.program_id(1)

    @pl.when(f == 0)
    def _():
        x = x_ref[...]
        ms = jnp.mean(x * x, axis=-1, keepdims=True)
        xn_s[...] = (x * lax.rsqrt(ms + EPS) * nw_ref[...]).astype(BF16)
        acc_s[...] = jnp.zeros_like(acc_s)

    hid = jnp.maximum(_dot(xn_s[...], w1_ref[...]), 0.0)
    acc_s[...] += _dot((hid * hid).astype(BF16), w2_ref[...])

    @pl.when(f == pl.num_programs(1) - 1)
    def _():
        x2 = x_ref[...] + acc_s[...]
        ms = jnp.mean(x2 * x2, axis=-1, keepdims=True)
        y_ref[...] = x2 * lax.rsqrt(ms + EPS) * fw_ref[...]


def _ffn(x, norm_w, w1, w2, final_w):
    R, D = x.shape
    DF = w1.shape[1]
    tm = _pick(R, (512, 256, 128))
    tf = _pick(DF, (512, 256, 128))
    return pl.pallas_call(
        _ffn_kernel,
        grid=(R // tm, DF // tf),
        in_specs=[
            pl.BlockSpec((tm, D), lambda i, f: (i, 0)),
            pl.BlockSpec((1, D), lambda i, f: (0, 0)),
            pl.BlockSpec((D, tf), lambda i, f: (0, f)),
            pl.BlockSpec((tf, D), lambda i, f: (f, 0)),
            pl.BlockSpec((1, D), lambda i, f: (0, 0)),
        ],
        out_specs=pl.BlockSpec((tm, D), lambda i, f: (i, 0)),
        out_shape=jax.ShapeDtypeStruct((R, D), F32),
        scratch_shapes=[pltpu.VMEM((tm, D), BF16), pltpu.VMEM((tm, D), F32)],
        compiler_params=_params("parallel", "arbitrary"),
        name="ffn",
    )(x, norm_w, w1, w2, final_w)


def _bias_by_distance(rel_bias, n):
    d = jnp.arange(n, dtype=jnp.int32)
    max_exact = N_BUCKETS // 2
    nf = jnp.maximum(d, 1).astype(F32)
    large = max_exact + (jnp.log(nf / max_exact) / math.log(MAX_DIST / max_exact)
                         * (N_BUCKETS - max_exact)).astype(jnp.int32)
    large = jnp.minimum(large, N_BUCKETS - 1)
    bucket = jnp.where(d < max_exact, d, large)
    return rel_bias.astype(F32)[bucket].T


def _prompt_bias_tiles(rel_bias, T):
    assert T + 1 >= MAX_DIST
    bd = _bias_by_distance(rel_bias, 2 * T)
    i = np.arange(T)[:, None]
    j = np.arange(T)[None, :]
    d0 = i - j
    t0 = jnp.where(jnp.asarray(d0 >= 0), bd[:, np.maximum(d0, 0)], NEG)
    t1 = bd[:, T + d0]
    cfar = jnp.broadcast_to(bd[:, 2 * T - 1][:, None, None], (H_D, 1, LANES))
    return t0, t1, cfar


def _sample_bias_tables(rel_bias, T, PG):
    assert PG + 1 >= MAX_DIST
    bd = _bias_by_distance(rel_bias, 2 * PG + T)
    HT = T * H_D
    row = np.arange(2 * HT)
    rt, rh = (row % HT) // H_D, row % H_D
    lane = np.arange(PG * H_D)
    lr, lh = lane // H_D, lane % H_D
    ok = jnp.asarray(rh[:, None] == lh[None, :])
    far = jnp.where(ok, bd[:, 2 * PG + T - 1][rh][:, None], NEG)
    d_last = PG + rt[:, None] - lr[None, :]
    last = jnp.where(ok, bd[rh[:, None], d_last], NEG)
    ln = np.arange(HT)
    nt, nh = ln // H_D, ln % H_D
    d_new = rt[:, None] - nt[None, :]
    okn = jnp.asarray((rh[:, None] == nh[None, :]) & (d_new >= 0))
    new = jnp.where(okn, bd[rh[:, None], np.maximum(d_new, 0)], NEG)
    return far.astype(F32), last.astype(F32), new.astype(F32)


def kernel(x_prompt, x_sample, cache_k, cache_v, page_table, state_C, state_n, state_m, state_conv,
           norm1_w, w_in, b_i, b_f, conv_w, conv_b, hnorm_w, lambda_q1, lambda_k1, lambda_q2, lambda_k2,
           subln_w, rel_bias, w_a, w_b, w_out, norm2_w, w_ff1, w_ff2, final_norm_w):
    assert w_in.shape[0] == 1 and x_prompt.shape[0] == 1
    _, S, D = x_prompt.shape
    B, T, _ = x_sample.shape
    PG = cache_k.shape[2]
    R = S + B * T

    wi = w_in[0]
    o_i = 2 * QK_M + 2 * W_M
    o_qd = o_i + 2 * H_M
    w_main = jnp.concatenate([wi[:, :o_i], wi[:, o_qd:]], axis=1).astype(BF16)
    wg = wi[:, o_i:o_qd]
    w_gate_col = jnp.zeros((D, 2 * LANES), F32).at[:, 0:H_M].set(wg[:, :H_M]) \
        .at[:, LANES:LANES + H_M].set(wg[:, H_M:]).astype(BF16)
    w_gate_row = wg.T.astype(BF16)
    bias_c = jnp.zeros((1, 2 * LANES), F32).at[0, 0:H_M].set(b_i[0]).at[0, LANES:LANES + H_M].set(b_f[0])
    bias_r = jnp.concatenate([b_i[0], b_f[0]])[:, None]

    x = jnp.concatenate([x_prompt[0], x_sample.reshape(B * T, D)], axis=0)
    z, gc, gr = _in_proj(x, norm1_w, w_main, w_gate_col, w_gate_row)

    hm_p, c_p, n_p, m_p, conv_p = _mlstm_prompt(z, gc, gr, conv_w[0], conv_b, bias_c, bias_r, hnorm_w, S)
    hist = jnp.pad(state_conv[0], ((0, 0), (T - (CONV_W - 1), 0), (0, 0))).reshape(B * T, 2 * QK_M)
    m0p = jnp.repeat(jnp.pad(state_m[0], ((0, 0), (0, LANES - H_M))), T, axis=0)
    hm_s, c_s, n_s, m_s = _mlstm_sample(z, gc, gr, hist, state_C[0], state_n[0], m0p, conv_w[0], conv_b,
                                        bias_c, bias_r, hnorm_w, S, B, T)

    TQ = _pick(S, (256, 128))
    t0, t1, cfar = _prompt_bias_tiles(rel_bias, TQ)
    lq1, lk1, lq2, lk2 = lambda_q1, lambda_k1, lambda_q2, lambda_k2
    att_p = _attn_prompt(z, t0, t1, cfar, lq1, lk1, lq2, lk2, subln_w, S)
    bfar, blast, bnew = _sample_bias_tables(rel_bias, T, PG)
    zs = z[S:]
    qs = zs[:, ZC_QD:ZC_QD + QK_D].reshape(B, T * H_D, 2 * DK_D)
    kn = zs[:, ZC_KD:ZC_KD + QK_D].reshape(B, T * H_D, 2 * DK_D)
    vn = zs[:, ZC_VD:ZC_VD + W_D].reshape(B, T * H_D, DV_D)
    att_s = _attn_sample(page_table, qs, cache_k, cache_v, kn, vn, bfar, blast, bnew,
                         lq1, lk1, lq2, lk2, subln_w)

    hm = jnp.concatenate([hm_p, hm_s], axis=0)
    att = jnp.concatenate([att_p, att_s.reshape(B * T, W_D)], axis=0)
    u = _merge(hm, att, w_a[0].astype(BF16), w_b[0].astype(BF16), z, D)
    x1 = _out_proj(u, w_out[0].astype(BF16), x)
    y = _ffn(x1, norm2_w, w_ff1[0].astype(BF16), w_ff2[0].astype(BF16), final_norm_w[None, :])

    zp = z[:S]
    k_prompt = zp[:, ZC_KD:ZC_KD + QK_D].reshape(1, 1, S, H_D, 2 * DK_D)
    v_prompt = zp[:, ZC_VD:ZC_VD + W_D].reshape(1, 1, S, H_D, DV_D)
    k_sample = kn.reshape(1, B, T, H_D, 2 * DK_D)
    v_sample = vn.reshape(1, B, T, H_D, DV_D)
    conv_prompt = conv_p[SUBLANES - (CONV_W - 1):].reshape(1, 1, CONV_W - 1, 2 * QK_M)
    conv_sample = zs[:, :2 * QK_M].reshape(B, T, 2 * QK_M)[:, T - (CONV_W - 1):][None]
    return (y[:S].reshape(1, S, D), y[S:].reshape(B, T, D),
            k_prompt, v_prompt, c_p[None, None], n_p[None, None], m_p[:, :H_M][None], conv_prompt,
            k_sample, v_sample, c_s[None], n_s[None], m_s[::T, :H_M][None], conv_sample)
```

```python
import functools
import math

import numpy as np
import jax
import jax.numpy as jnp
from jax import lax
from jax.experimental import pallas as pl
from jax.experimental.pallas import tpu as pltpu

F32 = jnp.float32
BF16 = jnp.bfloat16
HIGHEST = lax.Precision.HIGHEST

H_M = 8
DK_M = 128
DV_M = 128
QK_M = H_M * DK_M
W_M = H_M * DV_M
CONV_W = 4
H_D = 8
DK_D = 64
DV_D = 128
QK_D = H_D * 2 * DK_D
W_D = H_D * DV_D
N_BUCKETS = 32
MAX_DIST = 128
EPS = 1e-6
LAM_INIT = 0.8 - 0.6 * math.exp(-0.3 * 0)
NEG = -1e30

ZC_QK = 0
ZC_VM = 2 * QK_M
ZC_OM = ZC_VM + W_M
ZC_QD = ZC_OM + W_M
ZC_GA = ZC_QD + QK_D

LANES = 128
SUBLANES = 8
VMEM_LIMIT = 56 * 1024 * 1024

SEQ_BLOCK = 16
PAGES_PER_STEP = 8
FAR_GROUP = 4


def _params(*sem):
    return pltpu.CompilerParams(dimension_semantics=sem, vmem_limit_bytes=VMEM_LIMIT)


def _pick(n, prefs):
    for p in prefs:
        if n % p == 0:
            return p
    return n


def _sigmoid(x):
    return 1.0 / (1.0 + jnp.exp(-x))


def _log_sigmoid(x):
    return jnp.minimum(x, 0.0) - jnp.log(1.0 + jnp.exp(-jnp.abs(x)))


def _dot(a, b):
    return jnp.dot(a, b, preferred_element_type=F32)


def _dot_nt(a, b):
    return lax.dot_general(a, b, (((1,), (1,)), ((), ())), preferred_element_type=F32)


def _dot_tn(a, b):
    return lax.dot_general(a, b, (((0,), (0,)), ((), ())), preferred_element_type=F32)


def _dot_exact(a, b):
    return jnp.dot(a, b, preferred_element_type=F32, precision=HIGHEST)


def _two_way_rows(S, BT, prefs):
    tm = _pick(math.gcd(S, BT), prefs)
    n_prompt = S // tm
    prow = lambda i: jnp.minimum(i, n_prompt - 1)
    srow = lambda i: jnp.maximum(i - n_prompt, 0)
    return tm, n_prompt, BT // tm, prow, srow


def _in_proj_kernel(xp_ref, xs_ref, nw_ref, w_ref, wgc_ref, wgr_ref,
                    z_ref, kp_ref, ks_ref, vp_ref, vs_ref, gc_ref, gr_ref, xn_ref, *, n_prompt, nzt, nkt):
    i = pl.program_id(0)
    j = pl.program_id(1)
    is_p = i < n_prompt
    is_s = jnp.logical_not(is_p)

    def norm(x_ref):
        x = x_ref[...]
        ms = jnp.mean(x * x, axis=-1, keepdims=True)
        xn = (x * lax.rsqrt(ms + EPS) * nw_ref[...]).astype(BF16)
        xn_ref[...] = xn
        gc_ref[...] = _dot(xn, wgc_ref[...])
        gr_ref[...] = _dot_nt(wgr_ref[...], xn)

    @pl.when((j == 0) & is_p)
    def _():
        norm(xp_ref)

    @pl.when((j == 0) & is_s)
    def _():
        norm(xs_ref)

    y = _dot(xn_ref[...], w_ref[...])
    in_k = (j >= nzt) & (j < nzt + nkt)
    in_v = j >= nzt + nkt

    @pl.when(j < nzt)
    def _():
        z_ref[...] = y

    for cond, ref in ((in_k & is_p, kp_ref), (in_k & is_s, ks_ref), (in_v & is_p, vp_ref), (in_v & is_s, vs_ref)):
        @pl.when(cond)
        def _(ref=ref):
            ref[...] = y


def _in_proj(xp, xs, norm_w, w_main, w_gate_col, w_gate_row):
    S, D = xp.shape
    BT = xs.shape[0]
    NT = w_main.shape[1]
    tm, n_prompt, n_sample, prow, srow = _two_way_rows(S, BT, (1024, 512, 256, 128))
    tn = _pick(math.gcd(NT, QK_D), (512, 256, 128))
    nkt = QK_D // tn
    nzt = NT // tn - 2 * nkt
    kcol = lambda j: jnp.clip(j - nzt, 0, nkt - 1)
    vcol = lambda j: jnp.clip(j - nzt - nkt, 0, nkt - 1)
    p_spec = lambda col: pl.BlockSpec((tm, tn), lambda i, j: (prow(i), jnp.where(i < n_prompt, col(j), nkt - 1)))
    s_spec = lambda col: pl.BlockSpec((tm, tn), lambda i, j: (srow(i), jnp.where(i < n_prompt, 0, col(j))))
    return pl.pallas_call(
        functools.partial(_in_proj_kernel, n_prompt=n_prompt, nzt=nzt, nkt=nkt),
        grid=(n_prompt + n_sample, NT // tn),
        in_specs=[
            pl.BlockSpec((tm, D), lambda i, j: (prow(i), 0), pipeline_mode=pl.Buffered(1)),
            pl.BlockSpec((tm, D), lambda i, j: (srow(i), 0), pipeline_mode=pl.Buffered(1)),
            pl.BlockSpec((1, D), lambda i, j: (0, 0)),
            pl.BlockSpec((D, tn), lambda i, j: (0, j)),
            pl.BlockSpec((D, 2 * LANES), lambda i, j: (0, 0)),
            pl.BlockSpec((2 * SUBLANES, D), lambda i, j: (0, 0)),
        ],
        out_specs=[
            pl.BlockSpec((tm, tn), lambda i, j: (i, jnp.minimum(j, nzt - 1))),
            p_spec(kcol), s_spec(kcol), p_spec(vcol), s_spec(vcol),
            pl.BlockSpec((tm, 2 * LANES), lambda i, j: (i, 0)),
            pl.BlockSpec((2 * SUBLANES, tm), lambda i, j: (0, i)),
        ],
        out_shape=[
            jax.ShapeDtypeStruct((S + BT, nzt * tn), F32),
            jax.ShapeDtypeStruct((S, QK_D), F32),
            jax.ShapeDtypeStruct((BT, QK_D), F32),
            jax.ShapeDtypeStruct((S, W_D), F32),
            jax.ShapeDtypeStruct((BT, W_D), F32),
            jax.ShapeDtypeStruct((S + BT, 2 * LANES), F32),
            jax.ShapeDtypeStruct((2 * SUBLANES, S + BT), F32),
        ],
        scratch_shapes=[pltpu.VMEM((tm, D), BF16)],
        compiler_params=_params("arbitrary", "arbitrary"),
        name="in_proj",
    )(xp, xs, norm_w, w_main, w_gate_col, w_gate_row)


def _conv_silu(x, hist, hist_shift, cw_ref, cb_ref, row_in_seq):
    acc = cb_ref[...] + cw_ref[CONV_W - 1:CONV_W, :] * x
    for j in range(1, CONV_W):
        xr = pltpu.roll(x, j, axis=0)
        hr = pltpu.roll(hist, (j + hist_shift) % hist.shape[0], axis=0)
        if hist.shape[0] != x.shape[0]:
            first = jnp.where(row_in_seq[0:SUBLANES] < j, hr, xr[0:SUBLANES])
            xs = jnp.concatenate([first, xr[SUBLANES:]], axis=0)
        else:
            xs = jnp.where(row_in_seq < j, hr, xr)
        acc = acc + cw_ref[CONV_W - 1 - j:CONV_W - j, :] * xs
    return acc * _sigmoid(acc)


def _mlstm_intra(qb, kb, vb, mask, bt_c, bt_r, ig_r, inter_c):
    dlog = jnp.where(mask, bt_c - bt_r + ig_r, -jnp.inf)
    m_t = jnp.maximum(inter_c, jnp.max(dlog, axis=1, keepdims=True))
    dw = jnp.exp(dlog - m_t)
    iw = jnp.exp(inter_c - m_t)
    s = _dot_nt(qb, kb) * dw
    sv = _dot(s.astype(BF16), vb)
    return sv, jnp.sum(s, axis=1, keepdims=True), m_t, iw


def _head_out(num, den, m_t, o, hw_ref):
    den = jnp.maximum(jnp.abs(den), jnp.exp(-m_t))
    h = num / den
    hn = h * lax.rsqrt(jnp.mean(h * h, axis=-1, keepdims=True) + EPS) * hw_ref[...]
    return (hn * _sigmoid(o)).astype(BF16)


def _mlstm_prompt_kernel(zq_ref, zv_ref, zo_ref, gc_ref, gr_ref, cw_ref, cb_ref, bc_ref, br_ref, hw_ref,
                         h_ref, c_out, n_out, m_out, conv_out,
                         c_s, n_s, m_s, hist_s):
    c = pl.program_id(0)
    L = zq_ref.shape[0]

    @pl.when(c == 0)
    def _():
        c_s[...] = jnp.zeros_like(c_s)
        n_s[...] = jnp.zeros_like(n_s)
        m_s[...] = jnp.zeros_like(m_s)
        hist_s[...] = jnp.zeros_like(hist_s)

    x = zq_ref[...]
    row = lax.broadcasted_iota(jnp.int32, (L, 1), 0)
    qk = _conv_silu(x, hist_s[...], 0, cw_ref, cb_ref, row)
    hist_s[...] = x[L - SUBLANES:L, :]
    conv_out[...] = x[L - SUBLANES:L, :]

    gcol = gc_ref[...] + bc_ref[...]
    grow = gr_ref[...] + br_ref[...]
    ig_c = gcol[:, 0:LANES]
    lf_c = _log_sigmoid(gcol[:, LANES:2 * LANES])
    ig_r = grow[0:SUBLANES, :]
    lf_r = _log_sigmoid(grow[SUBLANES:2 * SUBLANES, :])
    ri = lax.broadcasted_iota(jnp.int32, (L, L), 0)
    ci = lax.broadcasted_iota(jnp.int32, (L, L), 1)
    mask = ci <= ri
    bt_c = _dot_exact(mask.astype(F32), lf_c)
    bt_r = _dot_exact(lf_r, (ri <= ci).astype(F32))
    m_prev = m_s[...]
    inter = bt_c + m_prev
    b_last = bt_c[L - 1:L, :]
    wlog = b_last - bt_c + ig_c
    m_new = jnp.maximum(b_last + m_prev, jnp.max(wlog, axis=0, keepdims=True))
    ws = jnp.exp(wlog - m_new)
    decay = jnp.exp(b_last + m_prev - m_new)
    m_s[...] = m_new
    m_out[...] = m_new

    for h in range(H_M):
        q = qk[:, h * DK_M:(h + 1) * DK_M]
        k = qk[:, QK_M + h * DK_M:QK_M + (h + 1) * DK_M] * (DK_M ** -0.5)
        v = zv_ref[:, h * DV_M:(h + 1) * DV_M]
        qb, kb, vb = q.astype(BF16), k.astype(BF16), v.astype(BF16)
        sv, ssum, m_t, iw = _mlstm_intra(qb, kb, vb, mask, bt_c[:, h:h + 1], bt_r[h:h + 1, :],
                                         ig_r[h:h + 1, :], inter[:, h:h + 1])
        C = c_s[h]
        n_row = n_s[h:h + 1, :]
        num = sv + iw * _dot_nt(qb, C.astype(BF16))
        den = ssum + iw * jnp.sum(q * n_row, axis=1, keepdims=True)
        h_ref[:, h * DV_M:(h + 1) * DV_M] = _head_out(num, den, m_t, zo_ref[:, h * DV_M:(h + 1) * DV_M], hw_ref)
        ws_h = ws[:, h:h + 1]
        dc = decay[:, h:h + 1]
        c_new = dc * C + _dot_tn((v * ws_h).astype(BF16), kb)
        n_new = dc * n_row + jnp.sum(ws_h * k, axis=0, keepdims=True)
        c_s[h] = c_new
        n_s[h:h + 1, :] = n_new
        c_out[h] = c_new
        n_out[h:h + 1, :] = n_new


def _mlstm_prompt(z, gc, gr, conv_w, conv_b, bias_c, bias_r, hnorm_w, S):
    L = _pick(S, (256, 128))
    nz = lambda col, width: col // width
    return pl.pallas_call(
        _mlstm_prompt_kernel,
        grid=(S // L,),
        in_specs=[
            pl.BlockSpec((L, 2 * QK_M), lambda c: (c, nz(ZC_QK, 2 * QK_M))),
            pl.BlockSpec((L, W_M), lambda c: (c, nz(ZC_VM, W_M))),
            pl.BlockSpec((L, W_M), lambda c: (c, nz(ZC_OM, W_M))),
            pl.BlockSpec((L, 2 * LANES), lambda c: (c, 0)),
            pl.BlockSpec((2 * SUBLANES, L), lambda c: (0, c)),
            pl.BlockSpec((CONV_W, 2 * QK_M), lambda c: (0, 0)),
            pl.BlockSpec((1, 2 * QK_M), lambda c: (0, 0)),
            pl.BlockSpec((1, 2 * LANES), lambda c: (0, 0)),
            pl.BlockSpec((2 * SUBLANES, 1), lambda c: (0, 0)),
            pl.BlockSpec((1, DV_M), lambda c: (0, 0)),
        ],
        out_specs=[
            pl.BlockSpec((L, W_M), lambda c: (c, 0)),
            pl.BlockSpec((H_M, DV_M, DK_M), lambda c: (0, 0, 0)),
            pl.BlockSpec((H_M, DK_M), lambda c: (0, 0)),
            pl.BlockSpec((1, LANES), lambda c: (0, 0)),
            pl.BlockSpec((SUBLANES, 2 * QK_M), lambda c: (0, 0)),
        ],
        out_shape=[
            jax.ShapeDtypeStruct((S, W_M), BF16),
            jax.ShapeDtypeStruct((H_M, DV_M, DK_M), F32),
            jax.ShapeDtypeStruct((H_M, DK_M), F32),
            jax.ShapeDtypeStruct((1, LANES), F32),
            jax.ShapeDtypeStruct((SUBLANES, 2 * QK_M), F32),
        ],
        scratch_shapes=[
            pltpu.VMEM((H_M, DV_M, DK_M), F32),
            pltpu.VMEM((H_M, DK_M), F32),
            pltpu.VMEM((1, LANES), F32),
            pltpu.VMEM((SUBLANES, 2 * QK_M), F32),
        ],
        compiler_params=_params("arbitrary"),
        name="mlstm_prompt",
    )(z, z, z, gc, gr, conv_w, conv_b, bias_c, bias_r, hnorm_w)


def _mlstm_sample_kernel(zq_ref, zv_ref, zo_ref, gc_ref, gr_ref, hist_ref, c0_ref, n0_ref, m0_ref,
                         cw_ref, cb_ref, bc_ref, br_ref, hw_ref,
                         h_ref, c_out, n_out, m_out, *, T):
    L = zq_ref.shape[0]
    NB = L // T
    x = zq_ref[...]
    ri = lax.broadcasted_iota(jnp.int32, (L, L), 0)
    ci = lax.broadcasted_iota(jnp.int32, (L, L), 1)
    same = (ri // T) == (ci // T)
    mask = same & (ci <= ri)
    row_t = lax.broadcasted_iota(jnp.int32, (L, 1), 0) % T
    qk = _conv_silu(x, hist_ref[...], L - T, cw_ref, cb_ref, row_t)

    gcol = gc_ref[...] + bc_ref[...]
    grow = gr_ref[...] + br_ref[...]
    ig_c = gcol[:, 0:LANES]
    lf_c = _log_sigmoid(gcol[:, LANES:2 * LANES])
    ig_r = grow[0:SUBLANES, :]
    lf_r = _log_sigmoid(grow[SUBLANES:2 * SUBLANES, :])
    bt_c = _dot_exact(mask.astype(F32), lf_c)
    bt_r = _dot_exact(lf_r, (same & (ri <= ci)).astype(F32))
    m_prev = m0_ref[...]
    last = same & (ci % T == T - 1)
    b_last = _dot_exact(last.astype(F32), bt_c)
    inter = bt_c + m_prev
    wlog = b_last - bt_c + ig_c
    wmax = jnp.max(wlog.reshape(NB, T, LANES), axis=1, keepdims=True)
    wmax = jnp.broadcast_to(wmax, (NB, T, LANES)).reshape(L, LANES)
    m_new = jnp.maximum(b_last + m_prev, wmax)
    ws = jnp.exp(wlog - m_new)
    decay = jnp.exp(b_last + m_prev - m_new)
    m_out[...] = m_new

    lane_seq = lax.broadcasted_iota(jnp.int32, (L, NB * DV_M), 1) // DV_M
    row_seq = lax.broadcasted_iota(jnp.int32, (L, NB * DV_M), 0) // T
    blockdiag = lane_seq == row_seq

    for h in range(H_M):
        q = qk[:, h * DK_M:(h + 1) * DK_M]
        k = qk[:, QK_M + h * DK_M:QK_M + (h + 1) * DK_M] * (DK_M ** -0.5)
        v = zv_ref[:, h * DV_M:(h + 1) * DV_M]
        qb, kb, vb = q.astype(BF16), k.astype(BF16), v.astype(BF16)
        sv, ssum, m_t, iw = _mlstm_intra(qb, kb, vb, mask, bt_c[:, h:h + 1], bt_r[h:h + 1, :],
                                         ig_r[h:h + 1, :], inter[:, h:h + 1])
        C = c0_ref[:, h]
        c_flat = C.reshape(NB * DV_M, DK_M)
        qc_all = _dot_nt(qb, c_flat.astype(BF16))
        qc = jnp.concatenate([qc_all[b * T:(b + 1) * T, b * DV_M:(b + 1) * DV_M] for b in range(NB)], axis=0)
        n_rows = jnp.broadcast_to(n0_ref[:, h:h + 1, :], (NB, T, DK_M)).reshape(L, DK_M)
        num = sv + iw * qc
        den = ssum + iw * jnp.sum(q * n_rows, axis=1, keepdims=True)
        h_ref[:, h * DV_M:(h + 1) * DV_M] = _head_out(num, den, m_t, zo_ref[:, h * DV_M:(h + 1) * DV_M], hw_ref)
        ws_h = ws[:, h:h + 1]
        vw = v * ws_h
        vw_exp = jnp.where(blockdiag, jnp.concatenate([vw] * NB, axis=1), 0.0).astype(BF16)
        upd = _dot_tn(vw_exp, kb).reshape(NB, DV_M, DK_M)
        dc = decay[:, h:h + 1].reshape(NB, T, 1)[:, 0:1, :]
        c_out[:, h] = dc * C + upd
        kw = (ws_h * k).reshape(NB, T, DK_M)
        n_out[:, h:h + 1, :] = dc * n0_ref[:, h:h + 1, :] + jnp.sum(kw, axis=1, keepdims=True)


def _mlstm_sample(z, gc, gr, hist, c0, n0, m0p, conv_w, conv_b, bias_c, bias_r, hnorm_w, S, B, T):
    NB = SEQ_BLOCK
    L = NB * T
    assert L == LANES and B % NB == 0 and S % L == 0
    r0 = S // L
    nz = lambda col, width: col // width
    return pl.pallas_call(
        functools.partial(_mlstm_sample_kernel, T=T),
        grid=(B // NB,),
        in_specs=[
            pl.BlockSpec((L, 2 * QK_M), lambda i: (r0 + i, nz(ZC_QK, 2 * QK_M))),
            pl.BlockSpec((L, W_M), lambda i: (r0 + i, nz(ZC_VM, W_M))),
            pl.BlockSpec((L, W_M), lambda i: (r0 + i, nz(ZC_OM, W_M))),
            pl.BlockSpec((L, 2 * LANES), lambda i: (r0 + i, 0)),
            pl.BlockSpec((2 * SUBLANES, L), lambda i: (0, r0 + i)),
            pl.BlockSpec((L, 2 * QK_M), lambda i: (i, 0)),
            pl.BlockSpec((NB, H_M, DV_M, DK_M), lambda i: (i, 0, 0, 0)),
            pl.BlockSpec((NB, H_M, DK_M), lambda i: (i, 0, 0)),
            pl.BlockSpec((L, LANES), lambda i: (i, 0)),
            pl.BlockSpec((CONV_W, 2 * QK_M), lambda i: (0, 0)),
            pl.BlockSpec((1, 2 * QK_M), lambda i: (0, 0)),
            pl.BlockSpec((1, 2 * LANES), lambda i: (0, 0)),
            pl.BlockSpec((2 * SUBLANES, 1), lambda i: (0, 0)),
            pl.BlockSpec((1, DV_M), lambda i: (0, 0)),
        ],
        out_specs=[
            pl.BlockSpec((L, W_M), lambda i: (i, 0)),
            pl.BlockSpec((NB, H_M, DV_M, DK_M), lambda i: (i, 0, 0, 0)),
            pl.BlockSpec((NB, H_M, DK_M), lambda i: (i, 0, 0)),
            pl.BlockSpec((L, LANES), lambda i: (i, 0)),
        ],
        out_shape=[
            jax.ShapeDtypeStruct((B * T, W_M), BF16),
            jax.ShapeDtypeStruct((B, H_M, DV_M, DK_M), F32),
            jax.ShapeDtypeStruct((B, H_M, DK_M), F32),
            jax.ShapeDtypeStruct((B * T, LANES), F32),
        ],
        compiler_params=_params("parallel"),
        name="mlstm_sample",
    )(z, z, z, gc, gr, hist, c0, n0, m0p, conv_w, conv_b, bias_c, bias_r, hnorm_w)


def _lambda(lq1, lk1, lq2, lk2):
    a = jnp.sum(lq1[...] * lk1[...], axis=-1, keepdims=True)
    b = jnp.sum(lq2[...] * lk2[...], axis=-1, keepdims=True)
    return jnp.exp(a) - jnp.exp(b) + LAM_INIT


def _subln(att, w_ref):
    y = att * lax.rsqrt(jnp.mean(att * att, axis=-1, keepdims=True) + EPS) * w_ref[...]
    return (y * (1.0 - LAM_INIT)).astype(BF16)


def _attn_prompt_kernel(q_ref, k_ref, v_ref, b0_ref, b1_ref, lq1, lk1, lq2, lk2, sw_ref,
                        o_ref, kb_s, vt_s, m_s, l_s, acc_s):
    qi = pl.program_id(1)
    T = q_ref.shape[0]
    n_tiles = kb_s.shape[0]

    @pl.when(qi == 0)
    def _():
        for t in range(n_tiles):
            kb_s[t] = k_ref[t * T:(t + 1) * T, :].astype(BF16)
            vt_s[t] = v_ref[t * T:(t + 1) * T, :].T.astype(BF16)

    q = q_ref[...] * (DK_D ** -0.5)
    lane = lax.broadcasted_iota(jnp.int32, q.shape, 1)
    qpad = (jnp.where(lane < DK_D, q, 0.0).astype(BF16), jnp.where(lane >= DK_D, q, 0.0).astype(BF16))
    def group(tiles, state):
        scores = []
        for kj, bias in tiles:
            kt = kb_s[kj]
            for c in range(2):
                s = _dot_nt(kt, qpad[c])
                scores.append(s if bias is None else s + bias)
        parts = ([], [])
        for t, (kj, _) in enumerate(tiles):
            vt = vt_s[kj]
            for c in range(2):
                s = scores[2 * t + c]
                m = jnp.max(s, axis=0, keepdims=True)
                p = jnp.exp(s - m)
                parts[c].append((m, jnp.sum(p, axis=0, keepdims=True), _dot(vt, p.astype(BF16))))
        out = []
        for c in range(2):
            m_old, l_old, acc_old = state[c]
            m_new = m_old
            for m, _, _ in parts[c]:
                m_new = jnp.maximum(m_new, m)
            a = jnp.exp(m_old - m_new)
            l_new, acc_new = a * l_old, a * acc_old
            for m, l, pv in parts[c]:
                a = jnp.exp(m - m_new)
                l_new, acc_new = l_new + a * l, acc_new + a * pv
            out.append((m_new, l_new, acc_new))
        return tuple(out)

    def load():
        return tuple((m_s[c], l_s[c], acc_s[c]) for c in range(2))

    def store(state):
        for c in range(2):
            m_s[c], l_s[c], acc_s[c] = state[c]

    init = tuple((jnp.full((1, T), -jnp.inf, F32), jnp.zeros((1, T), F32), jnp.zeros((DV_D, T), F32))
                 for _ in range(2))
    G = FAR_GROUP
    n_far = jnp.maximum(qi - 1, 0)
    n_grp = n_far // G
    state = lax.fori_loop(0, n_grp, lambda g, st: group([(g * G + t, None) for t in range(G)], st), init)
    state = lax.fori_loop(n_grp * G, n_far, lambda kj, st: group([(kj, None)], st), state)
    store(state)

    @pl.when(qi == 0)
    def _():
        store(group([(0, b0_ref[0])], load()))

    @pl.when(qi >= 1)
    def _():
        store(group([(qi - 1, b1_ref[0]), (qi, b0_ref[0])], load()))

    (_, l0, a0), (_, l1, a1) = load()
    lam = _lambda(lq1, lk1, lq2, lk2)
    att_t = a0 / l0 - lam * (a1 / l1)
    o_ref[...] = _subln(att_t.T, sw_ref)


def _attn_prompt(z, kp, vp, bias0, bias1, lq1, lk1, lq2, lk2, subln_w, S):
    T = bias0.shape[-1]
    hw = 2 * DK_D
    small = lambda shape: pl.BlockSpec(shape, lambda h, i: (0,) * len(shape))
    return pl.pallas_call(
        _attn_prompt_kernel,
        grid=(H_D, S // T),
        in_specs=[
            pl.BlockSpec((T, hw), lambda h, i: (i, ZC_QD // hw + h)),
            pl.BlockSpec((S, hw), lambda h, i: (0, h)),
            pl.BlockSpec((S, DV_D), lambda h, i: (0, h)),
            pl.BlockSpec((1, T, T), lambda h, i: (h, 0, 0)),
            pl.BlockSpec((1, T, T), lambda h, i: (h, 0, 0)),
            small((1, DK_D)), small((1, DK_D)), small((1, DK_D)), small((1, DK_D)),
            small((1, DV_D)),
        ],
        out_specs=pl.BlockSpec((T, DV_D), lambda h, i: (i, h)),
        out_shape=jax.ShapeDtypeStruct((S, W_D), BF16),
        scratch_shapes=[
            pltpu.VMEM((S // T, T, hw), BF16),
            pltpu.VMEM((S // T, DV_D, T), BF16),
            pltpu.VMEM((2, 1, T), F32),
            pltpu.VMEM((2, 1, T), F32),
            pltpu.VMEM((2, DV_D, T), F32),
        ],
        compiler_params=_params("arbitrary", "arbitrary"),
        name="attn_prompt",
    )(z, kp, vp, bias0, bias1, lq1, lk1, lq2, lk2, subln_w)


def _attn_sample_kernel(pt_ref, q_ref, *refs, pps):
    kc = refs[0:pps]
    vc = refs[pps:2 * pps]
    (kn_ref, vn_ref, bfar_ref, blast_ref, bnew_ref, lq1, lk1, lq2, lk2, sw_ref,
     o_ref, qbd_s, m_s, l_s, acc_s) = refs[2 * pps:]
    p = pl.program_id(1)
    last_step = p == pl.num_programs(1) - 1
    HT = q_ref.shape[1]

    @pl.when(p == 0)
    def _():
        q = q_ref[0] * (DK_D ** -0.5)
        lane = lax.broadcasted_iota(jnp.int32, q.shape, 1)
        qbd_s[0:HT, :] = jnp.where(lane < DK_D, q, 0.0).astype(BF16)
        qbd_s[HT:2 * HT, :] = jnp.where(lane >= DK_D, q, 0.0).astype(BF16)
        m_s[...] = jnp.full_like(m_s, -jnp.inf)
        l_s[...] = jnp.zeros_like(l_s)
        acc_s[...] = jnp.zeros_like(acc_s)

    def update(tiles):
        qbd = qbd_s[...]
        scores = [_dot_nt(qbd, k.astype(BF16)) + b for k, _, b in tiles]
        m_old = m_s[...]
        m_new = m_old
        for s in scores:
            m_new = jnp.maximum(m_new, jnp.max(s, axis=1, keepdims=True))
        alpha = jnp.exp(m_old - m_new)
        l = alpha * l_s[...]
        acc = alpha * acc_s[...]
        for s, (_, v, _) in zip(scores, tiles):
            pr = jnp.exp(s - m_new)
            l = l + jnp.sum(pr, axis=1, keepdims=True)
            acc = acc + _dot(pr.astype(BF16), v.astype(BF16))
        l_s[...] = l
        acc_s[...] = acc
        m_s[...] = m_new

    rows = kc[0].shape[0] * kc[0].shape[1]
    bfar = bfar_ref[...]
    b_end = jnp.where(last_step, blast_ref[...], bfar)
    update([(kc[i][...].reshape(rows, 2 * DK_D), vc[i][...].reshape(rows, DV_D), bfar if i < pps - 1 else b_end)
            for i in range(pps)])

    @pl.when(last_step)
    def _():
        update([(kn_ref[0], vn_ref[0], bnew_ref[...])])
        lam = _lambda(lq1, lk1, lq2, lk2)
        r = acc_s[...] / l_s[...]
        o_ref[0] = _subln(r[0:HT] - lam * r[HT:2 * HT], sw_ref)


def _attn_sample(page_table, qs, cache_k, cache_v, kn, vn, bfar, blast, bnew, lq1, lk1, lq2, lk2, subln_w):
    B, HT, _ = qs.shape
    n_pages = page_table.shape[1]
    PG = cache_k.shape[2]
    pps = _pick(n_pages, (PAGES_PER_STEP, 4, 2, 1))
    small = lambda shape: pl.BlockSpec(shape, lambda b, p, pt: (0,) * len(shape))

    def page_spec(i, width):
        return pl.BlockSpec((None, None, PG, H_D, width), lambda b, p, pt: (0, pt[b, p * pps + i], 0, 0, 0))

    grid_spec = pltpu.PrefetchScalarGridSpec(
        num_scalar_prefetch=1,
        grid=(B, n_pages // pps),
        in_specs=(
            [pl.BlockSpec((1, HT, 2 * DK_D), lambda b, p, pt: (b, 0, 0))]
            + [page_spec(i, 2 * DK_D) for i in range(pps)]
            + [page_spec(i, DV_D) for i in range(pps)]
            + [pl.BlockSpec((1, HT, 2 * DK_D), lambda b, p, pt: (b, 0, 0)),
               pl.BlockSpec((1, HT, DV_D), lambda b, p, pt: (b, 0, 0)),
               small(bfar.shape), small(blast.shape), small(bnew.shape),
               small((1, DK_D)), small((1, DK_D)), small((1, DK_D)), small((1, DK_D)),
               small((1, DV_D))]),
        out_specs=pl.BlockSpec((1, HT, DV_D), lambda b, p, pt: (b, 0, 0)),
        scratch_shapes=[
            pltpu.VMEM((2 * HT, 2 * DK_D), BF16),
            pltpu.VMEM((2 * HT, 1), F32),
            pltpu.VMEM((2 * HT, 1), F32),
            pltpu.VMEM((2 * HT, DV_D), F32),
        ],
    )
    return pl.pallas_call(
        functools.partial(_attn_sample_kernel, pps=pps),
        grid_spec=grid_spec,
        out_shape=jax.ShapeDtypeStruct((B, HT, DV_D), BF16),
        compiler_params=_params("arbitrary", "arbitrary"),
        name="attn_sample",
    )(page_table, qs, *([cache_k] * pps), *([cache_v] * pps), kn, vn, bfar, blast, bnew,
      lq1, lk1, lq2, lk2, subln_w)


def _merge_kernel(hmp_ref, hms_ref, atp_ref, ats_ref, wa_ref, wb_ref, ga_ref, gb_ref, u_ref, *, n_prompt):
    def body(hm_ref, at_ref):
        ya = _dot(hm_ref[...], wa_ref[...])
        yb = _dot(at_ref[...], wb_ref[...])
        u_ref[...] = (_sigmoid(ga_ref[...]) * ya + _sigmoid(gb_ref[...]) * yb).astype(BF16)

    @pl.when(pl.program_id(0) < n_prompt)
    def _():
        body(hmp_ref, atp_ref)

    @pl.when(pl.program_id(0) >= n_prompt)
    def _():
        body(hms_ref, ats_ref)


def _merge(hm_p, hm_s, att_p, att_s, w_a, w_b, z, D):
    S, BT = hm_p.shape[0], hm_s.shape[0]
    tm, n_prompt, n_sample, prow, srow = _two_way_rows(S, BT, (512, 256, 128))
    tn = _pick(D, (512, 256, 128))
    ga0, gb0 = ZC_GA // tn, (ZC_GA + D) // tn
    return pl.pallas_call(
        functools.partial(_merge_kernel, n_prompt=n_prompt),
        grid=(n_prompt + n_sample, D // tn),
        in_specs=[
            pl.BlockSpec((tm, W_M), lambda i, j: (prow(i), 0)),
            pl.BlockSpec((tm, W_M), lambda i, j: (srow(i), 0)),
            pl.BlockSpec((tm, W_D), lambda i, j: (prow(i), 0)),
            pl.BlockSpec((tm, W_D), lambda i, j: (srow(i), 0)),
            pl.BlockSpec((W_M, tn), lambda i, j: (0, j)),
            pl.BlockSpec((W_D, tn), lambda i, j: (0, j)),
            pl.BlockSpec((tm, tn), lambda i, j: (i, ga0 + j)),
            pl.BlockSpec((tm, tn), lambda i, j: (i, gb0 + j)),
        ],
        out_specs=pl.BlockSpec((tm, tn), lambda i, j: (i, j)),
        out_shape=jax.ShapeDtypeStruct((S + BT, D), BF16),
        compiler_params=_params("parallel", "parallel"),
        name="merge",
    )(hm_p, hm_s, att_p, att_s, w_a, w_b, z, z)


def _out_proj_kernel(u_ref, w_ref, xp_ref, xs_ref, o_ref, *, n_prompt):
    y = _dot(u_ref[...], w_ref[...])

    @pl.when(pl.program_id(0) < n_prompt)
    def _():
        o_ref[...] = xp_ref[...] + y

    @pl.when(pl.program_id(0) >= n_prompt)
    def _():
        o_ref[...] = xs_ref[...] + y


def _out_proj(u, w_out, xp, xs):
    S, D = xp.shape
    BT = xs.shape[0]
    tm, n_prompt, n_sample, prow, srow = _two_way_rows(S, BT, (512, 256, 128))
    tn = _pick(D, (512, 256, 128))
    return pl.pallas_call(
        functools.partial(_out_proj_kernel, n_prompt=n_prompt),
        grid=(n_prompt + n_sample, D // tn),
        in_specs=[
            pl.BlockSpec((tm, D), lambda i, j: (i, 0)),
            pl.BlockSpec((D, tn), lambda i, j: (0, j)),
            pl.BlockSpec((tm, tn), lambda i, j: (prow(i), jnp.where(i < n_prompt, j, D // tn - 1))),
            pl.BlockSpec((tm, tn), lambda i, j: (srow(i), jnp.where(i < n_prompt, 0, j))),
        ],
        out_specs=pl.BlockSpec((tm, tn), lambda i, j: (i, j)),
        out_shape=jax.ShapeDtypeStruct((S + BT, D), F32),
        compiler_params=_params("parallel", "parallel"),
        name="out_proj",
    )(u, w_out, xp, xs)


def _ffn_kernel(x_ref, nw_ref, w1_ref, w2_ref, fw_ref, yp_ref, ys_ref, xn_s, acc_s, *, n_prompt):
    f = pl.program_id(1)

    @pl.when(f == 0)
    def _():
        x = x_ref[...]
        ms = jnp.mean(x * x, axis=-1, keepdims=True)
        xn_s[...] = (x * lax.rsqrt(ms + EPS) * nw_ref[...]).astype(BF16)
        acc_s[...] = jnp.zeros_like(acc_s)

    hid = jnp.maximum(_dot(xn_s[...], w1_ref[...]), 0.0)
    acc_s[...] += _dot((hid * hid).astype(BF16), w2_ref[...])

    def final(y_ref):
        x2 = x_ref[...] + acc_s[...]
        ms = jnp.mean(x2 * x2, axis=-1, keepdims=True)
        y_ref[...] = x2 * lax.rsqrt(ms + EPS) * fw_ref[...]

    last = f == pl.num_programs(1) - 1

    @pl.when(last & (pl.program_id(0) < n_prompt))
    def _():
        final(yp_ref)

    @pl.when(last & (pl.program_id(0) >= n_prompt))
    def _():
        final(ys_ref)


def _ffn(x, norm_w, w1, w2, final_w, S):
    R, D = x.shape
    DF = w1.shape[1]
    tm, n_prompt, n_sample, prow, srow = _two_way_rows(S, R - S, (512, 256, 128))
    tf = _pick(DF, (512, 256, 128))
    return pl.pallas_call(
        functools.partial(_ffn_kernel, n_prompt=n_prompt),
        grid=(n_prompt + n_sample, DF // tf),
        in_specs=[
            pl.BlockSpec((tm, D), lambda i, f: (i, 0)),
            pl.BlockSpec((1, D), lambda i, f: (0, 0)),
            pl.BlockSpec((D, tf), lambda i, f: (0, f)),
            pl.BlockSpec((tf, D), lambda i, f: (f, 0)),
            pl.BlockSpec((1, D), lambda i, f: (0, 0)),
        ],
        out_specs=[
            pl.BlockSpec((tm, D), lambda i, f: (prow(i), 0)),
            pl.BlockSpec((tm, D), lambda i, f: (srow(i), 0)),
        ],
        out_shape=[jax.ShapeDtypeStruct((S, D), F32), jax.ShapeDtypeStruct((R - S, D), F32)],
        scratch_shapes=[pltpu.VMEM((tm, D), BF16), pltpu.VMEM((tm, D), F32)],
        compiler_params=_params("arbitrary", "arbitrary"),
        name="ffn",
    )(x, norm_w, w1, w2, final_w)


def _bias_by_distance(rel_bias, n):
    d = jnp.arange(n, dtype=jnp.int32)
    max_exact = N_BUCKETS // 2
    nf = jnp.maximum(d, 1).astype(F32)
    large = max_exact + (jnp.log(nf / max_exact) / math.log(MAX_DIST / max_exact)
                         * (N_BUCKETS - max_exact)).astype(jnp.int32)
    large = jnp.minimum(large, N_BUCKETS - 1)
    bucket = jnp.where(d < max_exact, d, large)
    onehot = (bucket[:, None] == jnp.arange(N_BUCKETS)[None, :]).astype(F32)
    return jnp.dot(onehot, rel_bias.astype(F32), precision=HIGHEST).T


def _toeplitz(w, rows, cols):
    n = w.shape[-1]
    assert cols <= n - 1
    lead = w.shape[:-1]
    flat = jnp.tile(w, (1,) * len(lead) + (rows,))[..., :rows * (n - 1)]
    return flat.reshape(lead + (rows, n - 1))[..., :cols]


def _prompt_bias_tiles(rel_bias, T):
    assert T + 1 >= MAX_DIST
    bd = _bias_by_distance(rel_bias, 2 * T)
    val = bd - bd[:, 2 * T - 1:]
    neg = jnp.full((H_D, T), NEG, F32)
    t0 = _toeplitz(jnp.concatenate([val[:, :T], neg], axis=1), T, T)
    t1 = _toeplitz(jnp.concatenate([val[:, T:], val[:, :T]], axis=1), T, T)
    return t0, t1


def _sample_bias_tables(rel_bias, T, PG):
    assert PG + 1 >= MAX_DIST
    bd = _bias_by_distance(rel_bias, 2 * PG + T)
    HT = T * H_D
    eye = jnp.asarray(np.eye(H_D, dtype=bool))

    def expand(tab):
        K = tab.shape[-1]
        full = jnp.where(eye[None, :, None, :], jnp.transpose(tab, (1, 0, 2))[:, :, :, None], NEG)
        full = full.reshape(HT, K * H_D)
        return jnp.concatenate([full, full], axis=0).astype(F32)

    far = expand(jnp.broadcast_to(bd[:, 2 * PG + T - 1][:, None, None], (H_D, T, PG)))
    w_last = jnp.concatenate([bd[:, PG:0:-1], bd[:, :1], bd[:, PG + T - 1:PG:-1]], axis=1)
    last = expand(_toeplitz(w_last, T, PG))
    w_new = jnp.concatenate([bd[:, :1], jnp.full((H_D, T), NEG, F32), bd[:, T - 1:0:-1]], axis=1)
    new = expand(_toeplitz(w_new, T, T))
    return far, last, new


def kernel(x_prompt, x_sample, cache_k, cache_v, page_table, state_C, state_n, state_m, state_conv,
           norm1_w, w_in, b_i, b_f, conv_w, conv_b, hnorm_w, lambda_q1, lambda_k1, lambda_q2, lambda_k2,
           subln_w, rel_bias, w_a, w_b, w_out, norm2_w, w_ff1, w_ff2, final_norm_w):
    assert w_in.shape[0] == 1 and x_prompt.shape[0] == 1
    _, S, D = x_prompt.shape
    B, T, _ = x_sample.shape
    PG = cache_k.shape[2]
    xp = x_prompt[0]
    xs = x_sample.reshape(B * T, D)

    wi = w_in[0]
    o_i = 2 * QK_M + 2 * W_M
    o_qd = o_i + 2 * H_M
    o_kd = o_qd + QK_D
    o_ga = o_kd + QK_D + W_D
    w_main = jnp.concatenate([wi[:, :o_i], wi[:, o_qd:o_kd], wi[:, o_ga:], wi[:, o_kd:o_ga]], axis=1).astype(BF16)
    wg = wi[:, o_i:o_qd]
    w_gate_col = jnp.zeros((D, 2 * LANES), F32).at[:, 0:H_M].set(wg[:, :H_M]) \
        .at[:, LANES:LANES + H_M].set(wg[:, H_M:]).astype(BF16)
    w_gate_row = wg.T.astype(BF16)
    bias_c = jnp.zeros((1, 2 * LANES), F32).at[0, 0:H_M].set(b_i[0]).at[0, LANES:LANES + H_M].set(b_f[0])
    bias_r = jnp.concatenate([b_i[0], b_f[0]])[:, None]

    z, kp, ks, vp, vs, gc, gr = _in_proj(xp, xs, norm1_w, w_main, w_gate_col, w_gate_row)

    hm_p, c_p, n_p, m_p, conv_p = _mlstm_prompt(z, gc, gr, conv_w[0], conv_b, bias_c, bias_r, hnorm_w, S)
    hist = jnp.pad(state_conv[0], ((0, 0), (T - (CONV_W - 1), 0), (0, 0))).reshape(B * T, 2 * QK_M)
    m0p = jnp.repeat(jnp.pad(state_m[0], ((0, 0), (0, LANES - H_M))), T, axis=0)
    hm_s, c_s, n_s, m_s = _mlstm_sample(z, gc, gr, hist, state_C[0], state_n[0], m0p, conv_w[0], conv_b,
                                        bias_c, bias_r, hnorm_w, S, B, T)

    TQ = _pick(S, (256, 128))
    t0, t1 = _prompt_bias_tiles(rel_bias, TQ)
    lq1, lk1, lq2, lk2 = lambda_q1, lambda_k1, lambda_q2, lambda_k2
    att_p = _attn_prompt(z, kp, vp, t0, t1, lq1, lk1, lq2, lk2, subln_w, S)
    bfar, blast, bnew = _sample_bias_tables(rel_bias, T, PG)
    zs = z[S:]
    qs = zs[:, ZC_QD:ZC_QD + QK_D].reshape(B, T * H_D, 2 * DK_D)
    kn = ks.reshape(B, T * H_D, 2 * DK_D)
    vn = vs.reshape(B, T * H_D, DV_D)
    att_s = _attn_sample(page_table, qs, cache_k, cache_v, kn, vn, bfar, blast, bnew,
                         lq1, lk1, lq2, lk2, subln_w)

    u = _merge(hm_p, hm_s, att_p, att_s.reshape(B * T, W_D), w_a[0].astype(BF16), w_b[0].astype(BF16), z, D)
    x1 = _out_proj(u, w_out[0].astype(BF16), xp, xs)
    y_p, y_s = _ffn(x1, norm2_w, w_ff1[0].astype(BF16), w_ff2[0].astype(BF16), final_norm_w[None, :], S)

    conv_prompt = conv_p[SUBLANES - (CONV_W - 1):].reshape(1, 1, CONV_W - 1, 2 * QK_M)
    conv_sample = zs[:, :2 * QK_M].reshape(B, T, 2 * QK_M)[:, T - (CONV_W - 1):][None]
    return (y_p.reshape(1, S, D), y_s.reshape(B, T, D),
            kp.reshape(1, 1, S, H_D, 2 * DK_D), vp.reshape(1, 1, S, H_D, DV_D),
            c_p[None, None], n_p[None, None], m_p[:, :H_M][None], conv_prompt,
            ks.reshape(1, B, T, H_D, 2 * DK_D), vs.reshape(1, B, T, H_D, DV_D),
            c_s[None], n_s[None], m_s[::T, :H_M][None], conv_sample)
```

```python
import functools
import math

import numpy as np
import jax
import jax.numpy as jnp
from jax import lax
from jax.experimental import pallas as pl
from jax.experimental.pallas import tpu as pltpu

F32 = jnp.float32
BF16 = jnp.bfloat16
HIGHEST = lax.Precision.HIGHEST

H_M = 8
DK_M = 128
DV_M = 128
QK_M = H_M * DK_M
W_M = H_M * DV_M
CONV_W = 4
H_D = 8
DK_D = 64
DV_D = 128
QK_D = H_D * 2 * DK_D
W_D = H_D * DV_D
N_BUCKETS = 32
MAX_DIST = 128
EPS = 1e-6
LAM_INIT = 0.8 - 0.6 * math.exp(-0.3 * 0)
NEG = -1e30
LOG2E = math.log2(math.e)

ZC_QK = 0
ZC_VM = 2 * QK_M
ZC_OM = ZC_VM + W_M
ZC_QD = ZC_OM + W_M
ZC_GA = ZC_QD + QK_D

LANES = 128
SUBLANES = 8
VMEM_LIMIT = 56 * 1024 * 1024

SEQ_BLOCK = 16
PAGES_PER_STEP = 8
FAR_GROUPS = (8, 4, 1)


def _params(*sem):
    return pltpu.CompilerParams(dimension_semantics=sem, vmem_limit_bytes=VMEM_LIMIT)


def _pick(n, prefs):
    for p in prefs:
        if n % p == 0:
            return p
    return n


def _sigmoid(x):
    return 1.0 / (1.0 + jnp.exp(-x))


def _log_sigmoid(x):
    return jnp.minimum(x, 0.0) - jnp.log(1.0 + jnp.exp(-jnp.abs(x)))


def _dot(a, b):
    return jnp.dot(a, b, preferred_element_type=F32)


def _dot_nt(a, b):
    return lax.dot_general(a, b, (((1,), (1,)), ((), ())), preferred_element_type=F32)


def _dot_tn(a, b):
    return lax.dot_general(a, b, (((0,), (0,)), ((), ())), preferred_element_type=F32)


def _dot_exact(a, b):
    return jnp.dot(a, b, preferred_element_type=F32, precision=HIGHEST)


def _two_way_rows(S, BT, prefs):
    tm = _pick(math.gcd(S, BT), prefs)
    n_prompt = S // tm
    prow = lambda i: jnp.minimum(i, n_prompt - 1)
    srow = lambda i: jnp.maximum(i - n_prompt, 0)
    return tm, n_prompt, BT // tm, prow, srow


def _in_proj_kernel(xp_ref, xs_ref, nw_ref, wa_ref, wb_ref, wgc_ref, wgr_ref,
                    z_ref, kp_ref, ks_ref, vp_ref, vs_ref, gc_ref, gr_ref, xn_ref, *, n_prompt, n_a, n_z1, nkt):
    i = pl.program_id(0)
    j = pl.program_id(1)
    is_p = i < n_prompt
    is_s = jnp.logical_not(is_p)

    def norm(x_ref):
        x = x_ref[...]
        ms = jnp.mean(x * x, axis=-1, keepdims=True)
        xn = (x * lax.rsqrt(ms + EPS) * nw_ref[...]).astype(BF16)
        xn_ref[...] = xn
        gc_ref[...] = _dot(xn, wgc_ref[...])
        gr_ref[...] = _dot_nt(wgr_ref[...], xn)

    @pl.when((j == 0) & is_p)
    def _():
        norm(xp_ref)

    @pl.when((j == 0) & is_s)
    def _():
        norm(xs_ref)

    in_k = (j >= n_z1) & (j < n_z1 + nkt)
    in_v = (j >= n_z1 + nkt) & (j < n_z1 + 2 * nkt)
    in_zb = (j >= n_a) & jnp.logical_not(in_k | in_v)
    for cond, w_ref, o_ref in ((j < n_a, wa_ref, z_ref), (in_zb, wb_ref, z_ref),
                               (in_k & is_p, wb_ref, kp_ref), (in_k & is_s, wb_ref, ks_ref),
                               (in_v & is_p, wb_ref, vp_ref), (in_v & is_s, wb_ref, vs_ref)):
        @pl.when(cond)
        def _(w_ref=w_ref, o_ref=o_ref):
            o_ref[...] = _dot(xn_ref[...], w_ref[...])


def _in_proj(xp, xs, norm_w, w_all, w_rest, w_gate_col, w_gate_row):
    S, D = xp.shape
    BT = xs.shape[0]
    tm, n_prompt, n_sample, prow, srow = _two_way_rows(S, BT, (1024, 512, 256, 128))
    tn = _pick(math.gcd(D, QK_D), (512, 256, 128))
    n_a = ZC_QD // tn
    n_z1 = ZC_GA // tn
    nkt = QK_D // tn
    n_tiles = n_a + w_rest.shape[1] // tn
    nzt = n_tiles - 2 * nkt
    zcol = lambda j: jnp.where(j < n_z1, j, jnp.where(j < n_z1 + 2 * nkt, n_z1 - 1, j - 2 * nkt))
    kcol = lambda j: jnp.clip(j - n_z1, 0, nkt - 1)
    vcol = lambda j: jnp.clip(j - n_z1 - nkt, 0, nkt - 1)
    p_spec = lambda col: pl.BlockSpec((tm, tn), lambda i, j: (prow(i), jnp.where(i < n_prompt, col(j), nkt - 1)))
    s_spec = lambda col: pl.BlockSpec((tm, tn), lambda i, j: (srow(i), jnp.where(i < n_prompt, 0, col(j))))
    return pl.pallas_call(
        functools.partial(_in_proj_kernel, n_prompt=n_prompt, n_a=n_a, n_z1=n_z1, nkt=nkt),
        grid=(n_prompt + n_sample, n_tiles),
        in_specs=[
            pl.BlockSpec((tm, D), lambda i, j: (prow(i), 0), pipeline_mode=pl.Buffered(1)),
            pl.BlockSpec((tm, D), lambda i, j: (srow(i), 0), pipeline_mode=pl.Buffered(1)),
            pl.BlockSpec((1, D), lambda i, j: (0, 0)),
            pl.BlockSpec((D, tn), lambda i, j: (0, jnp.minimum(j, n_a - 1))),
            pl.BlockSpec((D, tn), lambda i, j: (0, jnp.maximum(j - n_a, 0))),
            pl.BlockSpec((D, 2 * LANES), lambda i, j: (0, 0)),
            pl.BlockSpec((2 * SUBLANES, D), lambda i, j: (0, 0)),
        ],
        out_specs=[
            pl.BlockSpec((tm, tn), lambda i, j: (i, zcol(j))),
            p_spec(kcol), s_spec(kcol), p_spec(vcol), s_spec(vcol),
            pl.BlockSpec((tm, 2 * LANES), lambda i, j: (i, 0)),
            pl.BlockSpec((2 * SUBLANES, tm), lambda i, j: (0, i)),
        ],
        out_shape=[
            jax.ShapeDtypeStruct((S + BT, nzt * tn), F32),
            jax.ShapeDtypeStruct((S, QK_D), F32),
            jax.ShapeDtypeStruct((BT, QK_D), F32),
            jax.ShapeDtypeStruct((S, W_D), F32),
            jax.ShapeDtypeStruct((BT, W_D), F32),
            jax.ShapeDtypeStruct((S + BT, 2 * LANES), F32),
            jax.ShapeDtypeStruct((2 * SUBLANES, S + BT), F32),
        ],
        scratch_shapes=[pltpu.VMEM((tm, D), BF16)],
        compiler_params=_params("arbitrary", "arbitrary"),
        name="in_proj",
    )(xp, xs, norm_w, w_all, w_rest, w_gate_col, w_gate_row)


def _conv_silu(x, hist, hist_shift, cw_ref, cb_ref, row_in_seq):
    acc = cb_ref[...] + cw_ref[CONV_W - 1:CONV_W, :] * x
    for j in range(1, CONV_W):
        xr = pltpu.roll(x, j, axis=0)
        hr = pltpu.roll(hist, (j + hist_shift) % hist.shape[0], axis=0)
        if hist.shape[0] != x.shape[0]:
            first = jnp.where(row_in_seq[0:SUBLANES] < j, hr, xr[0:SUBLANES])
            xs = jnp.concatenate([first, xr[SUBLANES:]], axis=0)
        else:
            xs = jnp.where(row_in_seq < j, hr, xr)
        acc = acc + cw_ref[CONV_W - 1 - j:CONV_W - j, :] * xs
    return acc * _sigmoid(acc)


def _mlstm_intra(qb, kb, vb, mask, bt_c, bt_r, ig_r, inter_c):
    dlog = jnp.where(mask, bt_c - bt_r + ig_r, -jnp.inf)
    m_t = jnp.maximum(inter_c, jnp.max(dlog, axis=1, keepdims=True))
    dw = jnp.exp(dlog - m_t)
    iw = jnp.exp(inter_c - m_t)
    s = _dot_nt(qb, kb) * dw
    sv = _dot(s.astype(BF16), vb)
    return sv, jnp.sum(s, axis=1, keepdims=True), m_t, iw


def _head_out(num, den, m_t, o, hw_ref):
    den = jnp.maximum(jnp.abs(den), jnp.exp(-m_t))
    h = num / den
    hn = h * lax.rsqrt(jnp.mean(h * h, axis=-1, keepdims=True) + EPS) * hw_ref[...]
    return (hn * _sigmoid(o)).astype(BF16)


def _mlstm_prompt_kernel(zq_ref, zv_ref, zo_ref, gc_ref, gr_ref, cw_ref, cb_ref, bc_ref, br_ref, hw_ref,
                         h_ref, c_out, n_out, m_out, conv_out,
                         c_s, n_s, m_s, hist_s):
    c = pl.program_id(0)
    L = zq_ref.shape[0]

    @pl.when(c == 0)
    def _():
        c_s[...] = jnp.zeros_like(c_s)
        n_s[...] = jnp.zeros_like(n_s)
        m_s[...] = jnp.zeros_like(m_s)
        hist_s[...] = jnp.zeros_like(hist_s)

    x = zq_ref[...]
    row = lax.broadcasted_iota(jnp.int32, (L, 1), 0)
    qk = _conv_silu(x, hist_s[...], 0, cw_ref, cb_ref, row)
    hist_s[...] = x[L - SUBLANES:L, :]
    conv_out[...] = x[L - SUBLANES:L, :]

    gcol = gc_ref[...] + bc_ref[...]
    grow = gr_ref[...] + br_ref[...]
    ig_c = gcol[:, 0:LANES]
    lf_c = _log_sigmoid(gcol[:, LANES:2 * LANES])
    ig_r = grow[0:SUBLANES, :]
    lf_r = _log_sigmoid(grow[SUBLANES:2 * SUBLANES, :])
    ri = lax.broadcasted_iota(jnp.int32, (L, L), 0)
    ci = lax.broadcasted_iota(jnp.int32, (L, L), 1)
    mask = ci <= ri
    bt_c = _dot_exact(mask.astype(F32), lf_c)
    bt_r = _dot_exact(lf_r, (ri <= ci).astype(F32))
    m_prev = m_s[...]
    inter = bt_c + m_prev
    b_last = bt_c[L - 1:L, :]
    wlog = b_last - bt_c + ig_c
    m_new = jnp.maximum(b_last + m_prev, jnp.max(wlog, axis=0, keepdims=True))
    ws = jnp.exp(wlog - m_new)
    decay = jnp.exp(b_last + m_prev - m_new)
    m_s[...] = m_new
    m_out[...] = m_new

    for h in range(H_M):
        q = qk[:, h * DK_M:(h + 1) * DK_M]
        k = qk[:, QK_M + h * DK_M:QK_M + (h + 1) * DK_M] * (DK_M ** -0.5)
        v = zv_ref[:, h * DV_M:(h + 1) * DV_M]
        qb, kb, vb = q.astype(BF16), k.astype(BF16), v.astype(BF16)
        sv, ssum, m_t, iw = _mlstm_intra(qb, kb, vb, mask, bt_c[:, h:h + 1], bt_r[h:h + 1, :],
                                         ig_r[h:h + 1, :], inter[:, h:h + 1])
        C = c_s[h]
        n_row = n_s[h:h + 1, :]
        num = sv + iw * _dot_nt(qb, C.astype(BF16))
        den = ssum + iw * jnp.sum(q * n_row, axis=1, keepdims=True)
        h_ref[:, h * DV_M:(h + 1) * DV_M] = _head_out(num, den, m_t, zo_ref[:, h * DV_M:(h + 1) * DV_M], hw_ref)
        ws_h = ws[:, h:h + 1]
        dc = decay[:, h:h + 1]
        c_new = dc * C + _dot_tn((v * ws_h).astype(BF16), kb)
        n_new = dc * n_row + jnp.sum(ws_h * k, axis=0, keepdims=True)
        c_s[h] = c_new
        n_s[h:h + 1, :] = n_new
        c_out[h] = c_new
        n_out[h:h + 1, :] = n_new


def _mlstm_prompt(z, gc, gr, conv_w, conv_b, bias_c, bias_r, hnorm_w, S):
    L = _pick(S, (256, 128))
    nz = lambda col, width: col // width
    return pl.pallas_call(
        _mlstm_prompt_kernel,
        grid=(S // L,),
        in_specs=[
            pl.BlockSpec((L, 2 * QK_M), lambda c: (c, nz(ZC_QK, 2 * QK_M))),
            pl.BlockSpec((L, W_M), lambda c: (c, nz(ZC_VM, W_M))),
            pl.BlockSpec((L, W_M), lambda c: (c, nz(ZC_OM, W_M))),
            pl.BlockSpec((L, 2 * LANES), lambda c: (c, 0)),
            pl.BlockSpec((2 * SUBLANES, L), lambda c: (0, c)),
            pl.BlockSpec((CONV_W, 2 * QK_M), lambda c: (0, 0)),
            pl.BlockSpec((1, 2 * QK_M), lambda c: (0, 0)),
            pl.BlockSpec((1, 2 * LANES), lambda c: (0, 0)),
            pl.BlockSpec((2 * SUBLANES, 1), lambda c: (0, 0)),
            pl.BlockSpec((1, DV_M), lambda c: (0, 0)),
        ],
        out_specs=[
            pl.BlockSpec((L, W_M), lambda c: (c, 0)),
            pl.BlockSpec((H_M, DV_M, DK_M), lambda c: (0, 0, 0)),
            pl.BlockSpec((H_M, DK_M), lambda c: (0, 0)),
            pl.BlockSpec((1, LANES), lambda c: (0, 0)),
            pl.BlockSpec((SUBLANES, 2 * QK_M), lambda c: (0, 0)),
        ],
        out_shape=[
            jax.ShapeDtypeStruct((S, W_M), BF16),
            jax.ShapeDtypeStruct((H_M, DV_M, DK_M), F32),
            jax.ShapeDtypeStruct((H_M, DK_M), F32),
            jax.ShapeDtypeStruct((1, LANES), F32),
            jax.ShapeDtypeStruct((SUBLANES, 2 * QK_M), F32),
        ],
        scratch_shapes=[
            pltpu.VMEM((H_M, DV_M, DK_M), F32),
            pltpu.VMEM((H_M, DK_M), F32),
            pltpu.VMEM((1, LANES), F32),
            pltpu.VMEM((SUBLANES, 2 * QK_M), F32),
        ],
        compiler_params=_params("arbitrary"),
        name="mlstm_prompt",
    )(z, z, z, gc, gr, conv_w, conv_b, bias_c, bias_r, hnorm_w)


def _mlstm_sample_kernel(zq_ref, zv_ref, zo_ref, gc_ref, gr_ref, hist_ref, c0_ref, n0_ref, m0_ref,
                         cw_ref, cb_ref, bc_ref, br_ref, hw_ref,
                         h_ref, c_out, n_out, m_out, *, T):
    L = zq_ref.shape[0]
    NB = L // T
    x = zq_ref[...]
    ri = lax.broadcasted_iota(jnp.int32, (L, L), 0)
    ci = lax.broadcasted_iota(jnp.int32, (L, L), 1)
    same = (ri // T) == (ci // T)
    mask = same & (ci <= ri)
    row_t = lax.broadcasted_iota(jnp.int32, (L, 1), 0) % T
    qk = _conv_silu(x, hist_ref[...], L - T, cw_ref, cb_ref, row_t)

    gcol = gc_ref[...] + bc_ref[...]
    grow = gr_ref[...] + br_ref[...]
    ig_c = gcol[:, 0:LANES]
    lf_c = _log_sigmoid(gcol[:, LANES:2 * LANES])
    ig_r = grow[0:SUBLANES, :]
    lf_r = _log_sigmoid(grow[SUBLANES:2 * SUBLANES, :])
    bt_c = _dot_exact(mask.astype(F32), lf_c)
    bt_r = _dot_exact(lf_r, (same & (ri <= ci)).astype(F32))
    m_prev = m0_ref[...]
    last = same & (ci % T == T - 1)
    b_last = _dot_exact(last.astype(F32), bt_c)
    inter = bt_c + m_prev
    wlog = b_last - bt_c + ig_c
    wmax = jnp.max(wlog.reshape(NB, T, LANES), axis=1, keepdims=True)
    wmax = jnp.broadcast_to(wmax, (NB, T, LANES)).reshape(L, LANES)
    m_new = jnp.maximum(b_last + m_prev, wmax)
    ws = jnp.exp(wlog - m_new)
    decay = jnp.exp(b_last + m_prev - m_new)
    m_out[...] = m_new

    lane_seq = lax.broadcasted_iota(jnp.int32, (L, NB * DV_M), 1) // DV_M
    row_seq = lax.broadcasted_iota(jnp.int32, (L, NB * DV_M), 0) // T
    blockdiag = lane_seq == row_seq

    for h in range(H_M):
        q = qk[:, h * DK_M:(h + 1) * DK_M]
        k = qk[:, QK_M + h * DK_M:QK_M + (h + 1) * DK_M] * (DK_M ** -0.5)
        v = zv_ref[:, h * DV_M:(h + 1) * DV_M]
        qb, kb, vb = q.astype(BF16), k.astype(BF16), v.astype(BF16)
        sv, ssum, m_t, iw = _mlstm_intra(qb, kb, vb, mask, bt_c[:, h:h + 1], bt_r[h:h + 1, :],
                                         ig_r[h:h + 1, :], inter[:, h:h + 1])
        C = c0_ref[:, h]
        c_flat = C.reshape(NB * DV_M, DK_M)
        qc_all = _dot_nt(qb, c_flat.astype(BF16))
        qc = jnp.concatenate([qc_all[b * T:(b + 1) * T, b * DV_M:(b + 1) * DV_M] for b in range(NB)], axis=0)
        n_rows = jnp.broadcast_to(n0_ref[:, h:h + 1, :], (NB, T, DK_M)).reshape(L, DK_M)
        num = sv + iw * qc
        den = ssum + iw * jnp.sum(q * n_rows, axis=1, keepdims=True)
        h_ref[:, h * DV_M:(h + 1) * DV_M] = _head_out(num, den, m_t, zo_ref[:, h * DV_M:(h + 1) * DV_M], hw_ref)
        ws_h = ws[:, h:h + 1]
        vw = v * ws_h
        vw_exp = jnp.where(blockdiag, jnp.concatenate([vw] * NB, axis=1), 0.0).astype(BF16)
        upd = _dot_tn(vw_exp, kb).reshape(NB, DV_M, DK_M)
        dc = decay[:, h:h + 1].reshape(NB, T, 1)[:, 0:1, :]
        c_out[:, h] = dc * C + upd
        kw = (ws_h * k).reshape(NB, T, DK_M)
        n_out[:, h:h + 1, :] = dc * n0_ref[:, h:h + 1, :] + jnp.sum(kw, axis=1, keepdims=True)


def _mlstm_sample(z, gc, gr, hist, c0, n0, m0p, conv_w, conv_b, bias_c, bias_r, hnorm_w, S, B, T):
    NB = SEQ_BLOCK
    L = NB * T
    assert L == LANES and B % NB == 0 and S % L == 0
    r0 = S // L
    nz = lambda col, width: col // width
    return pl.pallas_call(
        functools.partial(_mlstm_sample_kernel, T=T),
        grid=(B // NB,),
        in_specs=[
            pl.BlockSpec((L, 2 * QK_M), lambda i: (r0 + i, nz(ZC_QK, 2 * QK_M))),
            pl.BlockSpec((L, W_M), lambda i: (r0 + i, nz(ZC_VM, W_M))),
            pl.BlockSpec((L, W_M), lambda i: (r0 + i, nz(ZC_OM, W_M))),
            pl.BlockSpec((L, 2 * LANES), lambda i: (r0 + i, 0)),
            pl.BlockSpec((2 * SUBLANES, L), lambda i: (0, r0 + i)),
            pl.BlockSpec((L, 2 * QK_M), lambda i: (i, 0)),
            pl.BlockSpec((NB, H_M, DV_M, DK_M), lambda i: (i, 0, 0, 0)),
            pl.BlockSpec((NB, H_M, DK_M), lambda i: (i, 0, 0)),
            pl.BlockSpec((L, LANES), lambda i: (i, 0)),
            pl.BlockSpec((CONV_W, 2 * QK_M), lambda i: (0, 0)),
            pl.BlockSpec((1, 2 * QK_M), lambda i: (0, 0)),
            pl.BlockSpec((1, 2 * LANES), lambda i: (0, 0)),
            pl.BlockSpec((2 * SUBLANES, 1), lambda i: (0, 0)),
            pl.BlockSpec((1, DV_M), lambda i: (0, 0)),
        ],
        out_specs=[
            pl.BlockSpec((L, W_M), lambda i: (i, 0)),
            pl.BlockSpec((NB, H_M, DV_M, DK_M), lambda i: (i, 0, 0, 0)),
            pl.BlockSpec((NB, H_M, DK_M), lambda i: (i, 0, 0)),
            pl.BlockSpec((L, LANES), lambda i: (i, 0)),
        ],
        out_shape=[
            jax.ShapeDtypeStruct((B * T, W_M), BF16),
            jax.ShapeDtypeStruct((B, H_M, DV_M, DK_M), F32),
            jax.ShapeDtypeStruct((B, H_M, DK_M), F32),
            jax.ShapeDtypeStruct((B * T, LANES), F32),
        ],
        compiler_params=_params("parallel"),
        name="mlstm_sample",
    )(z, z, z, gc, gr, hist, c0, n0, m0p, conv_w, conv_b, bias_c, bias_r, hnorm_w)


def _lambda(lq1, lk1, lq2, lk2):
    a = jnp.sum(lq1[...] * lk1[...], axis=-1, keepdims=True)
    b = jnp.sum(lq2[...] * lk2[...], axis=-1, keepdims=True)
    return jnp.exp(a) - jnp.exp(b) + LAM_INIT


def _subln(att, w_ref):
    y = att * lax.rsqrt(jnp.mean(att * att, axis=-1, keepdims=True) + EPS) * w_ref[...]
    return (y * (1.0 - LAM_INIT)).astype(BF16)


def _attn_prompt_kernel(q_ref, k_ref, v_ref, b0_ref, b1_ref, lq1, lk1, lq2, lk2, sw_ref,
                        o_ref, kb_s, vt_s, m_s, l_s, acc_s):
    qi = pl.program_id(1)
    T = q_ref.shape[0]
    n_tiles = kb_s.shape[0]

    @pl.when(qi == 0)
    def _():
        for t in range(n_tiles):
            kb_s[t] = k_ref[t * T:(t + 1) * T, :].astype(BF16)
            vt_s[t] = v_ref[t * T:(t + 1) * T, :].T.astype(BF16)

    q = q_ref[...] * (DK_D ** -0.5 * LOG2E)
    lane = lax.broadcasted_iota(jnp.int32, q.shape, 1)
    qpad = (jnp.where(lane < DK_D, q, 0.0).astype(BF16), jnp.where(lane >= DK_D, q, 0.0).astype(BF16))

    def group(tiles, state):
        scores = []
        for kj, bias in tiles:
            kt = kb_s[kj]
            for c in range(2):
                s = _dot_nt(kt, qpad[c])
                scores.append(s if bias is None else s + bias)
        parts = ([], [])
        for t, (kj, _) in enumerate(tiles):
            vt = vt_s[kj]
            for c in range(2):
                s = scores[2 * t + c]
                m = jnp.max(s, axis=0, keepdims=True)
                p = jnp.exp2(s - m)
                parts[c].append((m, jnp.sum(p, axis=0, keepdims=True), _dot(vt, p.astype(BF16))))
        out = []
        for c in range(2):
            m_old, l_old, acc_old = state[c]
            m_new = m_old
            for m, _, _ in parts[c]:
                m_new = jnp.maximum(m_new, m)
            a = jnp.exp2(m_old - m_new)
            l_new, acc_new = a * l_old, a * acc_old
            for m, l, pv in parts[c]:
                a = jnp.exp2(m - m_new)
                l_new, acc_new = l_new + a * l, acc_new + a * pv
            out.append((m_new, l_new, acc_new))
        return tuple(out)

    def load():
        return tuple((m_s[c], l_s[c], acc_s[c]) for c in range(2))

    def store(state):
        for c in range(2):
            m_s[c], l_s[c], acc_s[c] = state[c]

    init = tuple((jnp.full((1, T), -jnp.inf, F32), jnp.zeros((1, T), F32), jnp.zeros((DV_D, T), F32))
                 for _ in range(2))
    n_far = jnp.maximum(qi - 1, 0)
    state, done = init, 0
    for G in FAR_GROUPS:
        n_grp = (n_far - done) // G
        state = lax.fori_loop(0, n_grp, lambda g, st, G=G, done=done: group(
            [(done + g * G + t, None) for t in range(G)], st), state)
        done = done + n_grp * G
    store(state)

    @pl.when(qi == 0)
    def _():
        store(group([(0, b0_ref[0])], load()))

    @pl.when(qi >= 1)
    def _():
        store(group([(qi - 1, b1_ref[0]), (qi, b0_ref[0])], load()))

    (_, l0, a0), (_, l1, a1) = load()
    lam = _lambda(lq1, lk1, lq2, lk2)
    att_t = a0 / l0 - lam * (a1 / l1)
    o_ref[...] = _subln(att_t.T, sw_ref)


def _attn_prompt(z, kp, vp, bias0, bias1, lq1, lk1, lq2, lk2, subln_w, S):
    T = bias0.shape[-1]
    hw = 2 * DK_D
    small = lambda shape: pl.BlockSpec(shape, lambda h, i: (0,) * len(shape))
    return pl.pallas_call(
        _attn_prompt_kernel,
        grid=(H_D, S // T),
        in_specs=[
            pl.BlockSpec((T, hw), lambda h, i: (i, ZC_QD // hw + h)),
            pl.BlockSpec((S, hw), lambda h, i: (0, h)),
            pl.BlockSpec((S, DV_D), lambda h, i: (0, h)),
            pl.BlockSpec((1, T, T), lambda h, i: (h, 0, 0)),
            pl.BlockSpec((1, T, T), lambda h, i: (h, 0, 0)),
            small((1, DK_D)), small((1, DK_D)), small((1, DK_D)), small((1, DK_D)),
            small((1, DV_D)),
        ],
        out_specs=pl.BlockSpec((T, DV_D), lambda h, i: (i, h)),
        out_shape=jax.ShapeDtypeStruct((S, W_D), BF16),
        scratch_shapes=[
            pltpu.VMEM((S // T, T, hw), BF16),
            pltpu.VMEM((S // T, DV_D, T), BF16),
            pltpu.VMEM((2, 1, T), F32),
            pltpu.VMEM((2, 1, T), F32),
            pltpu.VMEM((2, DV_D, T), F32),
        ],
        compiler_params=_params("arbitrary", "arbitrary"),
        name="attn_prompt",
    )(z, kp, vp, bias0, bias1, lq1, lk1, lq2, lk2, subln_w)


def _attn_sample_kernel(pt_ref, q_ref, *refs, pps):
    kc = refs[0:pps]
    vc = refs[pps:2 * pps]
    (kn_ref, vn_ref, bfar_ref, blast_ref, bnew_ref, lq1, lk1, lq2, lk2, sw_ref,
     o_ref, qbd_s, m_s, l_s, acc_s) = refs[2 * pps:]
    p = pl.program_id(1)
    last_step = p == pl.num_programs(1) - 1
    HT = q_ref.shape[1]

    @pl.when(p == 0)
    def _():
        q = q_ref[0] * (DK_D ** -0.5 * LOG2E)
        lane = lax.broadcasted_iota(jnp.int32, q.shape, 1)
        qbd_s[0:HT, :] = jnp.where(lane < DK_D, q, 0.0).astype(BF16)
        qbd_s[HT:2 * HT, :] = jnp.where(lane >= DK_D, q, 0.0).astype(BF16)
        m_s[...] = jnp.full_like(m_s, -jnp.inf)
        l_s[...] = jnp.zeros_like(l_s)
        acc_s[...] = jnp.zeros_like(acc_s)

    def update(tiles):
        qbd = qbd_s[...]
        scores = [_dot_nt(qbd, k.astype(BF16)) + b for k, _, b in tiles]
        m_old = m_s[...]
        m_new = m_old
        for s in scores:
            m_new = jnp.maximum(m_new, jnp.max(s, axis=1, keepdims=True))
        alpha = jnp.exp2(m_old - m_new)
        l = alpha * l_s[...]
        acc = alpha * acc_s[...]
        for s, (_, v, _) in zip(scores, tiles):
            pr = jnp.exp2(s - m_new)
            l = l + jnp.sum(pr, axis=1, keepdims=True)
            acc = acc + _dot(pr.astype(BF16), v.astype(BF16))
        l_s[...] = l
        acc_s[...] = acc
        m_s[...] = m_new

    rows = kc[0].shape[0] * kc[0].shape[1]
    bfar = bfar_ref[...]
    b_end = jnp.where(last_step, blast_ref[...], bfar)
    update([(kc[i][...].reshape(rows, 2 * DK_D), vc[i][...].reshape(rows, DV_D), bfar if i < pps - 1 else b_end)
            for i in range(pps)])

    @pl.when(last_step)
    def _():
        update([(kn_ref[0], vn_ref[0], bnew_ref[...])])
        lam = _lambda(lq1, lk1, lq2, lk2)
        r = acc_s[...] / l_s[...]
        o_ref[0] = _subln(r[0:HT] - lam * r[HT:2 * HT], sw_ref)


def _attn_sample(page_table, qs, cache_k, cache_v, kn, vn, bfar, blast, bnew, lq1, lk1, lq2, lk2, subln_w):
    B, HT, _ = qs.shape
    n_pages = page_table.shape[1]
    PG = cache_k.shape[2]
    pps = _pick(n_pages, (PAGES_PER_STEP, 4, 2, 1))
    small = lambda shape: pl.BlockSpec(shape, lambda b, p, pt: (0,) * len(shape))

    def page_spec(i, width):
        return pl.BlockSpec((None, None, PG, H_D, width), lambda b, p, pt: (0, pt[b, p * pps + i], 0, 0, 0))

    grid_spec = pltpu.PrefetchScalarGridSpec(
        num_scalar_prefetch=1,
        grid=(B, n_pages // pps),
        in_specs=(
            [pl.BlockSpec((1, HT, 2 * DK_D), lambda b, p, pt: (b, 0, 0))]
            + [page_spec(i, 2 * DK_D) for i in range(pps)]
            + [page_spec(i, DV_D) for i in range(pps)]
            + [pl.BlockSpec((1, HT, 2 * DK_D), lambda b, p, pt: (b, 0, 0)),
               pl.BlockSpec((1, HT, DV_D), lambda b, p, pt: (b, 0, 0)),
               small(bfar.shape), small(blast.shape), small(bnew.shape),
               small((1, DK_D)), small((1, DK_D)), small((1, DK_D)), small((1, DK_D)),
               small((1, DV_D))]),
        out_specs=pl.BlockSpec((1, HT, DV_D), lambda b, p, pt: (b, 0, 0)),
        scratch_shapes=[
            pltpu.VMEM((2 * HT, 2 * DK_D), BF16),
            pltpu.VMEM((2 * HT, 1), F32),
            pltpu.VMEM((2 * HT, 1), F32),
            pltpu.VMEM((2 * HT, DV_D), F32),
        ],
    )
    return pl.pallas_call(
        functools.partial(_attn_sample_kernel, pps=pps),
        grid_spec=grid_spec,
        out_shape=jax.ShapeDtypeStruct((B, HT, DV_D), BF16),
        compiler_params=_params("arbitrary", "arbitrary"),
        name="attn_sample",
    )(page_table, qs, *([cache_k] * pps), *([cache_v] * pps), kn, vn, bfar, blast, bnew,
      lq1, lk1, lq2, lk2, subln_w)


def _merge_kernel(hmp_ref, hms_ref, atp_ref, ats_ref, wa_ref, wb_ref, ga_ref, gb_ref, u_ref, *, n_prompt):
    def body(hm_ref, at_ref):
        ya = _dot(hm_ref[...], wa_ref[...])
        yb = _dot(at_ref[...], wb_ref[...])
        u_ref[...] = (_sigmoid(ga_ref[...]) * ya + _sigmoid(gb_ref[...]) * yb).astype(BF16)

    @pl.when(pl.program_id(0) < n_prompt)
    def _():
        body(hmp_ref, atp_ref)

    @pl.when(pl.program_id(0) >= n_prompt)
    def _():
        body(hms_ref, ats_ref)


def _merge(hm_p, hm_s, att_p, att_s, w_a, w_b, z, D):
    S, BT = hm_p.shape[0], hm_s.shape[0]
    tm, n_prompt, n_sample, prow, srow = _two_way_rows(S, BT, (1024, 512, 256, 128))
    tn = _pick(D, (512, 256, 128))
    ga0, gb0 = ZC_GA // tn, (ZC_GA + D) // tn
    return pl.pallas_call(
        functools.partial(_merge_kernel, n_prompt=n_prompt),
        grid=(n_prompt + n_sample, D // tn),
        in_specs=[
            pl.BlockSpec((tm, W_M), lambda i, j: (prow(i), 0)),
            pl.BlockSpec((tm, W_M), lambda i, j: (srow(i), 0)),
            pl.BlockSpec((tm, W_D), lambda i, j: (prow(i), 0)),
            pl.BlockSpec((tm, W_D), lambda i, j: (srow(i), 0)),
            pl.BlockSpec((W_M, tn), lambda i, j: (0, j)),
            pl.BlockSpec((W_D, tn), lambda i, j: (0, j)),
            pl.BlockSpec((tm, tn), lambda i, j: (i, ga0 + j)),
            pl.BlockSpec((tm, tn), lambda i, j: (i, gb0 + j)),
        ],
        out_specs=pl.BlockSpec((tm, tn), lambda i, j: (i, j)),
        out_shape=jax.ShapeDtypeStruct((S + BT, D), BF16),
        compiler_params=_params("parallel", "parallel"),
        name="merge",
    )(hm_p, hm_s, att_p, att_s, w_a, w_b, z, z)


def _out_proj_kernel(u_ref, w_ref, xp_ref, xs_ref, o_ref, *, n_prompt):
    y = _dot(u_ref[...], w_ref[...])

    @pl.when(pl.program_id(0) < n_prompt)
    def _():
        o_ref[...] = xp_ref[...] + y

    @pl.when(pl.program_id(0) >= n_prompt)
    def _():
        o_ref[...] = xs_ref[...] + y


def _out_proj(u, w_out, xp, xs):
    S, D = xp.shape
    BT = xs.shape[0]
    tm, n_prompt, n_sample, prow, srow = _two_way_rows(S, BT, (1024, 512, 256, 128))
    tn = _pick(D, (512, 256, 128))
    return pl.pallas_call(
        functools.partial(_out_proj_kernel, n_prompt=n_prompt),
        grid=(n_prompt + n_sample, D // tn),
        in_specs=[
            pl.BlockSpec((tm, D), lambda i, j: (i, 0)),
            pl.BlockSpec((D, tn), lambda i, j: (0, j)),
            pl.BlockSpec((tm, tn), lambda i, j: (prow(i), jnp.where(i < n_prompt, j, D // tn - 1))),
            pl.BlockSpec((tm, tn), lambda i, j: (srow(i), jnp.where(i < n_prompt, 0, j))),
        ],
        out_specs=pl.BlockSpec((tm, tn), lambda i, j: (i, j)),
        out_shape=jax.ShapeDtypeStruct((S + BT, D), F32),
        compiler_params=_params("parallel", "parallel"),
        name="out_proj",
    )(u, w_out, xp, xs)


def _ffn_kernel(x_ref, nw_ref, w1_ref, w2_ref, fw_ref, yp_ref, ys_ref, xn_s, acc_s, *, n_prompt):
    f = pl.program_id(1)

    @pl.when(f == 0)
    def _():
        x = x_ref[...]
        ms = jnp.mean(x * x, axis=-1, keepdims=True)
        xn_s[...] = (x * lax.rsqrt(ms + EPS) * nw_ref[...]).astype(BF16)
        acc_s[...] = jnp.zeros_like(acc_s)

    hid = jnp.maximum(_dot(xn_s[...], w1_ref[...]), 0.0)
    acc_s[...] += _dot((hid * hid).astype(BF16), w2_ref[...])

    def final(y_ref):
        x2 = x_ref[...] + acc_s[...]
        ms = jnp.mean(x2 * x2, axis=-1, keepdims=True)
        y_ref[...] = x2 * lax.rsqrt(ms + EPS) * fw_ref[...]

    last = f == pl.num_programs(1) - 1

    @pl.when(last & (pl.program_id(0) < n_prompt))
    def _():
        final(yp_ref)

    @pl.when(last & (pl.program_id(0) >= n_prompt))
    def _():
        final(ys_ref)


def _ffn(x, norm_w, w1, w2, final_w, S):
    R, D = x.shape
    DF = w1.shape[1]
    tm, n_prompt, n_sample, prow, srow = _two_way_rows(S, R - S, (512, 256, 128))
    tf = _pick(DF, (1024, 512, 256, 128))
    return pl.pallas_call(
        functools.partial(_ffn_kernel, n_prompt=n_prompt),
        grid=(n_prompt + n_sample, DF // tf),
        in_specs=[
            pl.BlockSpec((tm, D), lambda i, f: (i, 0)),
            pl.BlockSpec((1, D), lambda i, f: (0, 0)),
            pl.BlockSpec((D, tf), lambda i, f: (0, f)),
            pl.BlockSpec((tf, D), lambda i, f: (f, 0)),
            pl.BlockSpec((1, D), lambda i, f: (0, 0)),
        ],
        out_specs=[
            pl.BlockSpec((tm, D), lambda i, f: (prow(i), 0)),
            pl.BlockSpec((tm, D), lambda i, f: (srow(i), 0)),
        ],
        out_shape=[jax.ShapeDtypeStruct((S, D), F32), jax.ShapeDtypeStruct((R - S, D), F32)],
        scratch_shapes=[pltpu.VMEM((tm, D), BF16), pltpu.VMEM((tm, D), F32)],
        compiler_params=_params("arbitrary", "arbitrary"),
        name="ffn",
    )(x, norm_w, w1, w2, final_w)


def _bias_by_distance(rel_bias, n):
    d = jnp.arange(n, dtype=jnp.int32)
    max_exact = N_BUCKETS // 2
    nf = jnp.maximum(d, 1).astype(F32)
    large = max_exact + (jnp.log(nf / max_exact) / math.log(MAX_DIST / max_exact)
                         * (N_BUCKETS - max_exact)).astype(jnp.int32)
    large = jnp.minimum(large, N_BUCKETS - 1)
    bucket = jnp.where(d < max_exact, d, large)
    onehot = (bucket[:, None] == jnp.arange(N_BUCKETS)[None, :]).astype(F32)
    return jnp.dot(onehot, rel_bias.astype(F32), precision=HIGHEST).T


def _toeplitz(w, rows, cols):
    n = w.shape[-1]
    assert cols <= n - 1
    lead = w.shape[:-1]
    flat = jnp.tile(w, (1,) * len(lead) + (rows,))[..., :rows * (n - 1)]
    return flat.reshape(lead + (rows, n - 1))[..., :cols]


def _prompt_bias_tiles(rel_bias, T):
    assert T + 1 >= MAX_DIST
    bd = _bias_by_distance(rel_bias, 2 * T)
    val = (bd - bd[:, 2 * T - 1:]) * LOG2E
    neg = jnp.full((H_D, T), NEG, F32)
    t0 = _toeplitz(jnp.concatenate([val[:, :T], neg], axis=1), T, T)
    t1 = _toeplitz(jnp.concatenate([val[:, T:], val[:, :T]], axis=1), T, T)
    return t0, t1


def _sample_bias_tables(rel_bias, T, PG):
    assert PG + 1 >= MAX_DIST
    bd = _bias_by_distance(rel_bias, 2 * PG + T) * LOG2E
    HT = T * H_D
    eye = jnp.asarray(np.eye(H_D, dtype=bool))

    def expand(tab):
        K = tab.shape[-1]
        full = jnp.where(eye[None, :, None, :], jnp.transpose(tab, (1, 0, 2))[:, :, :, None], NEG)
        full = full.reshape(HT, K * H_D)
        return jnp.concatenate([full, full], axis=0).astype(F32)

    far = expand(jnp.broadcast_to(bd[:, 2 * PG + T - 1][:, None, None], (H_D, T, PG)))
    w_last = jnp.concatenate([bd[:, PG:0:-1], bd[:, :1], bd[:, PG + T - 1:PG:-1]], axis=1)
    last = expand(_toeplitz(w_last, T, PG))
    w_new = jnp.concatenate([bd[:, :1], jnp.full((H_D, T), NEG, F32), bd[:, T - 1:0:-1]], axis=1)
    new = expand(_toeplitz(w_new, T, T))
    return far, last, new


def kernel(x_prompt, x_sample, cache_k, cache_v, page_table, state_C, state_n, state_m, state_conv,
           norm1_w, w_in, b_i, b_f, conv_w, conv_b, hnorm_w, lambda_q1, lambda_k1, lambda_q2, lambda_k2,
           subln_w, rel_bias, w_a, w_b, w_out, norm2_w, w_ff1, w_ff2, final_norm_w):
    assert w_in.shape[0] == 1 and x_prompt.shape[0] == 1
    _, S, D = x_prompt.shape
    B, T, _ = x_sample.shape
    PG = cache_k.shape[2]
    xp = x_prompt[0]
    xs = x_sample.reshape(B * T, D)

    o_i = 2 * QK_M + 2 * W_M
    o_qd = o_i + 2 * H_M
    w_all = w_in[0].astype(BF16)
    w_rest = w_all[:, o_qd:]
    wg = w_in[0][:, o_i:o_qd]
    w_gate_col = jnp.zeros((D, 2 * LANES), F32).at[:, 0:H_M].set(wg[:, :H_M]) \
        .at[:, LANES:LANES + H_M].set(wg[:, H_M:]).astype(BF16)
    w_gate_row = wg.T.astype(BF16)
    bias_c = jnp.zeros((1, 2 * LANES), F32).at[0, 0:H_M].set(b_i[0]).at[0, LANES:LANES + H_M].set(b_f[0])
    bias_r = jnp.concatenate([b_i[0], b_f[0]])[:, None]

    z, kp, ks, vp, vs, gc, gr = _in_proj(xp, xs, norm1_w, w_all, w_rest, w_gate_col, w_gate_row)

    hm_p, c_p, n_p, m_p, conv_p = _mlstm_prompt(z, gc, gr, conv_w[0], conv_b, bias_c, bias_r, hnorm_w, S)
    hist = jnp.pad(state_conv[0], ((0, 0), (T - (CONV_W - 1), 0), (0, 0))).reshape(B * T, 2 * QK_M)
    m0p = jnp.repeat(jnp.pad(state_m[0], ((0, 0), (0, LANES - H_M))), T, axis=0)
    hm_s, c_s, n_s, m_s = _mlstm_sample(z, gc, gr, hist, state_C[0], state_n[0], m0p, conv_w[0], conv_b,
                                        bias_c, bias_r, hnorm_w, S, B, T)

    TQ = _pick(S, (256, 128))
    t0, t1 = _prompt_bias_tiles(rel_bias, TQ)
    lq1, lk1, lq2, lk2 = lambda_q1, lambda_k1, lambda_q2, lambda_k2
    att_p = _attn_prompt(z, kp, vp, t0, t1, lq1, lk1, lq2, lk2, subln_w, S)
    bfar, blast, bnew = _sample_bias_tables(rel_bias, T, PG)
    qs = z[S:, ZC_QD:ZC_QD + QK_D].reshape(B, T * H_D, 2 * DK_D)
    kn = ks.reshape(B, T * H_D, 2 * DK_D)
    vn = vs.reshape(B, T * H_D, DV_D)
    att_s = _attn_sample(page_table, qs, cache_k, cache_v, kn, vn, bfar, blast, bnew,
                         lq1, lk1, lq2, lk2, subln_w)

    u = _merge(hm_p, hm_s, att_p, att_s.reshape(B * T, W_D), w_a[0].astype(BF16), w_b[0].astype(BF16), z, D)
    x1 = _out_proj(u, w_out[0].astype(BF16), xp, xs)
    y_p, y_s = _ffn(x1, norm2_w, w_ff1[0].astype(BF16), w_ff2[0].astype(BF16), final_norm_w[None, :], S)

    conv_prompt = conv_p[SUBLANES - (CONV_W - 1):].reshape(1, 1, CONV_W - 1, 2 * QK_M)
    conv_sample = z[S:, :2 * QK_M].reshape(B, T, 2 * QK_M)[:, T - (CONV_W - 1):][None]
    return (y_p.reshape(1, S, D), y_s.reshape(B, T, D),
            kp.reshape(1, 1, S, H_D, 2 * DK_D), vp.reshape(1, 1, S, H_D, DV_D),
            c_p[None, None], n_p[None, None], m_p[:, :H_M][None], conv_prompt,
            ks.reshape(1, B, T, H_D, 2 * DK_D), vs.reshape(1, B, T, H_D, DV_D),
            c_s[None], n_s[None], m_s[::T, :H_M][None], conv_sample)
```

```python
import functools
import math

import numpy as np
import jax
import jax.numpy as jnp
from jax import lax
from jax.experimental import pallas as pl
from jax.experimental.pallas import tpu as pltpu

F32 = jnp.float32
BF16 = jnp.bfloat16
HIGHEST = lax.Precision.HIGHEST

H_M = 8
DK_M = 128
DV_M = 128
QK_M = H_M * DK_M
W_M = H_M * DV_M
CONV_W = 4
H_D = 8
DK_D = 64
DV_D = 128
QK_D = H_D * 2 * DK_D
W_D = H_D * DV_D
N_BUCKETS = 32
MAX_DIST = 128
EPS = 1e-6
LAM_INIT = 0.8 - 0.6 * math.exp(-0.3 * 0)
NEG = -1e30
LOG2E = math.log2(math.e)

ZC_QK = 0
ZC_VM = 2 * QK_M
ZC_OM = ZC_VM + W_M
ZC_QD = ZC_OM + W_M
ZC_GA = ZC_QD + QK_D

LANES = 128
SUBLANES = 8
VMEM_LIMIT = 56 * 1024 * 1024

SEQ_BLOCK = 16
PAGES_PER_STEP = 8
FAR_GROUPS = (8, 4, 2, 1)
HEAD_GROUPS = (8, 4, 2, 1)
ONES_ROWS = 16


def _params(*sem):
    return pltpu.CompilerParams(dimension_semantics=sem, vmem_limit_bytes=VMEM_LIMIT)


def _pick(n, prefs):
    for p in prefs:
        if n % p == 0:
            return p
    return n


def _sigmoid(x):
    return 1.0 / (1.0 + jnp.exp(-x))


def _log_sigmoid(x):
    return jnp.minimum(x, 0.0) - jnp.log(1.0 + jnp.exp(-jnp.abs(x)))


def _dot(a, b):
    return jnp.dot(a, b, preferred_element_type=F32)


def _dot_nt(a, b):
    return lax.dot_general(a, b, (((1,), (1,)), ((), ())), preferred_element_type=F32)


def _dot_tn(a, b):
    return lax.dot_general(a, b, (((0,), (0,)), ((), ())), preferred_element_type=F32)


def _dot_exact(a, b):
    return jnp.dot(a, b, preferred_element_type=F32, precision=HIGHEST)


def _two_way_rows(S, BT, prefs):
    tm = _pick(math.gcd(S, BT), prefs)
    n_prompt = S // tm
    prow = lambda i: jnp.minimum(i, n_prompt - 1)
    srow = lambda i: jnp.maximum(i - n_prompt, 0)
    return tm, n_prompt, BT // tm, prow, srow


def _in_proj_kernel(xp_ref, xs_ref, nw_ref, wa_ref, wb_ref, wgc_ref, wgr_ref,
                    z_ref, kp_ref, ks_ref, vp_ref, vs_ref, gc_ref, gr_ref, xn_ref, *, n_prompt, n_a, n_z1, nkt):
    i = pl.program_id(0)
    j = pl.program_id(1)
    is_p = i < n_prompt
    is_s = jnp.logical_not(is_p)

    def norm(x_ref):
        x = x_ref[...]
        ms = jnp.mean(x * x, axis=-1, keepdims=True)
        xn = (x * lax.rsqrt(ms + EPS) * nw_ref[...]).astype(BF16)
        xn_ref[...] = xn
        gc_ref[...] = _dot_nt(xn, wgc_ref[...])
        gr_ref[...] = _dot_nt(wgr_ref[...], xn)

    @pl.when((j == 0) & is_p)
    def _():
        norm(xp_ref)

    @pl.when((j == 0) & is_s)
    def _():
        norm(xs_ref)

    in_k = (j >= n_z1) & (j < n_z1 + nkt)
    in_v = (j >= n_z1 + nkt) & (j < n_z1 + 2 * nkt)
    in_zb = (j >= n_a) & jnp.logical_not(in_k | in_v)
    for cond, w_ref, o_ref in ((j < n_a, wa_ref, z_ref), (in_zb, wb_ref, z_ref),
                               (in_k & is_p, wb_ref, kp_ref), (in_k & is_s, wb_ref, ks_ref),
                               (in_v & is_p, wb_ref, vp_ref), (in_v & is_s, wb_ref, vs_ref)):
        @pl.when(cond)
        def _(w_ref=w_ref, o_ref=o_ref):
            o_ref[...] = _dot_nt(xn_ref[...], w_ref[...])


def _in_proj(xp, xs, norm_w, w_all, w_rest, w_gate_col, w_gate_row):
    S, D = xp.shape
    BT = xs.shape[0]
    tm, n_prompt, n_sample, prow, srow = _two_way_rows(S, BT, (1024, 512, 256, 128))
    tn = _pick(math.gcd(D, QK_D), (512, 256, 128))
    n_a = ZC_QD // tn
    n_z1 = ZC_GA // tn
    nkt = QK_D // tn
    n_tiles = n_a + w_rest.shape[0] // tn
    nzt = n_tiles - 2 * nkt
    zcol = lambda j: jnp.where(j < n_z1, j, jnp.where(j < n_z1 + 2 * nkt, n_z1 - 1, j - 2 * nkt))
    kcol = lambda j: jnp.clip(j - n_z1, 0, nkt - 1)
    vcol = lambda j: jnp.clip(j - n_z1 - nkt, 0, nkt - 1)
    p_spec = lambda col: pl.BlockSpec((tm, tn), lambda i, j: (prow(i), jnp.where(i < n_prompt, col(j), nkt - 1)))
    s_spec = lambda col: pl.BlockSpec((tm, tn), lambda i, j: (srow(i), jnp.where(i < n_prompt, 0, col(j))))
    return pl.pallas_call(
        functools.partial(_in_proj_kernel, n_prompt=n_prompt, n_a=n_a, n_z1=n_z1, nkt=nkt),
        grid=(n_prompt + n_sample, n_tiles),
        in_specs=[
            pl.BlockSpec((tm, D), lambda i, j: (prow(i), 0), pipeline_mode=pl.Buffered(1)),
            pl.BlockSpec((tm, D), lambda i, j: (srow(i), 0), pipeline_mode=pl.Buffered(1)),
            pl.BlockSpec((1, D), lambda i, j: (0, 0)),
            pl.BlockSpec((tn, D), lambda i, j: (jnp.minimum(j, n_a - 1), 0)),
            pl.BlockSpec((tn, D), lambda i, j: (jnp.maximum(j - n_a, 0), 0)),
            pl.BlockSpec((2 * LANES, D), lambda i, j: (0, 0)),
            pl.BlockSpec((2 * SUBLANES, D), lambda i, j: (0, 0)),
        ],
        out_specs=[
            pl.BlockSpec((tm, tn), lambda i, j: (i, zcol(j))),
            p_spec(kcol), s_spec(kcol), p_spec(vcol), s_spec(vcol),
            pl.BlockSpec((tm, 2 * LANES), lambda i, j: (i, 0)),
            pl.BlockSpec((2 * SUBLANES, tm), lambda i, j: (0, i)),
        ],
        out_shape=[
            jax.ShapeDtypeStruct((S + BT, nzt * tn), F32),
            jax.ShapeDtypeStruct((S, QK_D), F32),
            jax.ShapeDtypeStruct((BT, QK_D), F32),
            jax.ShapeDtypeStruct((S, W_D), F32),
            jax.ShapeDtypeStruct((BT, W_D), F32),
            jax.ShapeDtypeStruct((S + BT, 2 * LANES), F32),
            jax.ShapeDtypeStruct((2 * SUBLANES, S + BT), F32),
        ],
        scratch_shapes=[pltpu.VMEM((tm, D), BF16)],
        compiler_params=_params("arbitrary", "arbitrary"),
        name="in_proj",
    )(xp, xs, norm_w, w_all, w_rest, w_gate_col, w_gate_row)


def _conv_silu(x, hist, hist_shift, cw_ref, cb_ref, row_in_seq):
    acc = cb_ref[...] + cw_ref[CONV_W - 1:CONV_W, :] * x
    for j in range(1, CONV_W):
        xr = pltpu.roll(x, j, axis=0)
        hr = pltpu.roll(hist, (j + hist_shift) % hist.shape[0], axis=0)
        if hist.shape[0] != x.shape[0]:
            first = jnp.where(row_in_seq[0:SUBLANES] < j, hr, xr[0:SUBLANES])
            xs = jnp.concatenate([first, xr[SUBLANES:]], axis=0)
        else:
            xs = jnp.where(row_in_seq < j, hr, xr)
        acc = acc + cw_ref[CONV_W - 1 - j:CONV_W - j, :] * xs
    return acc * _sigmoid(acc)


def _mlstm_intra(qb, kb, vb, mask, bt_c, bt_r, ig_r, inter_c):
    dlog = jnp.where(mask, bt_c - bt_r + ig_r, -jnp.inf)
    m_t = jnp.maximum(inter_c, jnp.max(dlog, axis=1, keepdims=True))
    dw = jnp.exp(dlog - m_t)
    iw = jnp.exp(inter_c - m_t)
    s = _dot_nt(qb, kb) * dw
    sv = _dot(s.astype(BF16), vb)
    return sv, jnp.sum(s, axis=1, keepdims=True), m_t, iw


def _head_out(num, den, m_t, o, hw_ref):
    den = jnp.maximum(jnp.abs(den), jnp.exp(-m_t))
    h = num / den
    hn = h * lax.rsqrt(jnp.mean(h * h, axis=-1, keepdims=True) + EPS) * hw_ref[...]
    return (hn * _sigmoid(o)).astype(BF16)


def _mlstm_prompt_kernel(zq_ref, zv_ref, zo_ref, gc_ref, gr_ref, cw_ref, cb_ref, bc_ref, br_ref, hw_ref,
                         h_ref, c_out, n_out, m_out, conv_out,
                         c_s, n_s, m_s, hist_s):
    c = pl.program_id(0)
    L = zq_ref.shape[0]

    @pl.when(c == 0)
    def _():
        c_s[...] = jnp.zeros_like(c_s)
        n_s[...] = jnp.zeros_like(n_s)
        m_s[...] = jnp.zeros_like(m_s)
        hist_s[...] = jnp.zeros_like(hist_s)

    x = zq_ref[...]
    row = lax.broadcasted_iota(jnp.int32, (L, 1), 0)
    qk = _conv_silu(x, hist_s[...], 0, cw_ref, cb_ref, row)
    hist_s[...] = x[L - SUBLANES:L, :]
    conv_out[...] = x[L - SUBLANES:L, :]

    gcol = gc_ref[...] + bc_ref[...]
    grow = gr_ref[...] + br_ref[...]
    ig_c = gcol[:, 0:LANES]
    lf_c = _log_sigmoid(gcol[:, LANES:2 * LANES])
    ig_r = grow[0:SUBLANES, :]
    lf_r = _log_sigmoid(grow[SUBLANES:2 * SUBLANES, :])
    ri = lax.broadcasted_iota(jnp.int32, (L, L), 0)
    ci = lax.broadcasted_iota(jnp.int32, (L, L), 1)
    mask = ci <= ri
    bt_c = _dot_exact(mask.astype(F32), lf_c)
    bt_r = _dot_exact(lf_r, (ri <= ci).astype(F32))
    m_prev = m_s[...]
    inter = bt_c + m_prev
    b_last = bt_c[L - 1:L, :]
    wlog = b_last - bt_c + ig_c
    m_new = jnp.maximum(b_last + m_prev, jnp.max(wlog, axis=0, keepdims=True))
    ws = jnp.exp(wlog - m_new)
    decay = jnp.exp(b_last + m_prev - m_new)
    m_s[...] = m_new
    m_out[...] = m_new

    for h in range(H_M):
        q = qk[:, h * DK_M:(h + 1) * DK_M]
        k = qk[:, QK_M + h * DK_M:QK_M + (h + 1) * DK_M] * (DK_M ** -0.5)
        v = zv_ref[:, h * DV_M:(h + 1) * DV_M]
        qb, kb, vb = q.astype(BF16), k.astype(BF16), v.astype(BF16)
        sv, ssum, m_t, iw = _mlstm_intra(qb, kb, vb, mask, bt_c[:, h:h + 1], bt_r[h:h + 1, :],
                                         ig_r[h:h + 1, :], inter[:, h:h + 1])
        C = c_s[h]
        n_row = n_s[h:h + 1, :]
        num = sv + iw * _dot_nt(qb, C.astype(BF16))
        den = ssum + iw * jnp.sum(q * n_row, axis=1, keepdims=True)
        h_ref[:, h * DV_M:(h + 1) * DV_M] = _head_out(num, den, m_t, zo_ref[:, h * DV_M:(h + 1) * DV_M], hw_ref)
        ws_h = ws[:, h:h + 1]
        dc = decay[:, h:h + 1]
        c_new = dc * C + _dot_tn((v * ws_h).astype(BF16), kb)
        n_new = dc * n_row + jnp.sum(ws_h * k, axis=0, keepdims=True)
        c_s[h] = c_new
        n_s[h:h + 1, :] = n_new
        c_out[h] = c_new
        n_out[h:h + 1, :] = n_new


def _mlstm_prompt(z, gc, gr, conv_w, conv_b, bias_c, bias_r, hnorm_w, S):
    L = _pick(S, (256, 128))
    nz = lambda col, width: col // width
    return pl.pallas_call(
        _mlstm_prompt_kernel,
        grid=(S // L,),
        in_specs=[
            pl.BlockSpec((L, 2 * QK_M), lambda c: (c, nz(ZC_QK, 2 * QK_M))),
            pl.BlockSpec((L, W_M), lambda c: (c, nz(ZC_VM, W_M))),
            pl.BlockSpec((L, W_M), lambda c: (c, nz(ZC_OM, W_M))),
            pl.BlockSpec((L, 2 * LANES), lambda c: (c, 0)),
            pl.BlockSpec((2 * SUBLANES, L), lambda c: (0, c)),
            pl.BlockSpec((CONV_W, 2 * QK_M), lambda c: (0, 0)),
            pl.BlockSpec((1, 2 * QK_M), lambda c: (0, 0)),
            pl.BlockSpec((1, 2 * LANES), lambda c: (0, 0)),
            pl.BlockSpec((2 * SUBLANES, 1), lambda c: (0, 0)),
            pl.BlockSpec((1, DV_M), lambda c: (0, 0)),
        ],
        out_specs=[
            pl.BlockSpec((L, W_M), lambda c: (c, 0)),
            pl.BlockSpec((H_M, DV_M, DK_M), lambda c: (0, 0, 0)),
            pl.BlockSpec((H_M, DK_M), lambda c: (0, 0)),
            pl.BlockSpec((1, LANES), lambda c: (0, 0)),
            pl.BlockSpec((SUBLANES, 2 * QK_M), lambda c: (0, 0)),
        ],
        out_shape=[
            jax.ShapeDtypeStruct((S, W_M), BF16),
            jax.ShapeDtypeStruct((H_M, DV_M, DK_M), F32),
            jax.ShapeDtypeStruct((H_M, DK_M), F32),
            jax.ShapeDtypeStruct((1, LANES), F32),
            jax.ShapeDtypeStruct((SUBLANES, 2 * QK_M), F32),
        ],
        scratch_shapes=[
            pltpu.VMEM((H_M, DV_M, DK_M), F32),
            pltpu.VMEM((H_M, DK_M), F32),
            pltpu.VMEM((1, LANES), F32),
            pltpu.VMEM((SUBLANES, 2 * QK_M), F32),
        ],
        compiler_params=_params("arbitrary"),
        name="mlstm_prompt",
    )(z, z, z, gc, gr, conv_w, conv_b, bias_c, bias_r, hnorm_w)


def _mlstm_sample_kernel(zq_ref, zv_ref, zo_ref, gc_ref, gr_ref, hist_ref, c0_ref, n0_ref, m0_ref,
                         cw_ref, cb_ref, bc_ref, br_ref, hw_ref,
                         h_ref, c_out, n_out, m_out, *, T):
    L = zq_ref.shape[0]
    NB = L // T
    x = zq_ref[...]
    ri = lax.broadcasted_iota(jnp.int32, (L, L), 0)
    ci = lax.broadcasted_iota(jnp.int32, (L, L), 1)
    same = (ri // T) == (ci // T)
    mask = same & (ci <= ri)
    row_t = lax.broadcasted_iota(jnp.int32, (L, 1), 0) % T
    qk = _conv_silu(x, hist_ref[...], L - T, cw_ref, cb_ref, row_t)

    gcol = gc_ref[...] + bc_ref[...]
    grow = gr_ref[...] + br_ref[...]
    ig_c = gcol[:, 0:LANES]
    lf_c = _log_sigmoid(gcol[:, LANES:2 * LANES])
    ig_r = grow[0:SUBLANES, :]
    lf_r = _log_sigmoid(grow[SUBLANES:2 * SUBLANES, :])
    bt_c = _dot_exact(mask.astype(F32), lf_c)
    bt_r = _dot_exact(lf_r, (same & (ri <= ci)).astype(F32))
    m_prev = m0_ref[...]
    last = same & (ci % T == T - 1)
    b_last = _dot_exact(last.astype(F32), bt_c)
    inter = bt_c + m_prev
    wlog = b_last - bt_c + ig_c
    wmax = jnp.max(wlog.reshape(NB, T, LANES), axis=1, keepdims=True)
    wmax = jnp.broadcast_to(wmax, (NB, T, LANES)).reshape(L, LANES)
    m_new = jnp.maximum(b_last + m_prev, wmax)
    ws = jnp.exp(wlog - m_new)
    decay = jnp.exp(b_last + m_prev - m_new)
    m_out[...] = m_new

    lane_seq = lax.broadcasted_iota(jnp.int32, (L, NB * DV_M), 1) // DV_M
    row_seq = lax.broadcasted_iota(jnp.int32, (L, NB * DV_M), 0) // T
    blockdiag = lane_seq == row_seq

    for h in range(H_M):
        q = qk[:, h * DK_M:(h + 1) * DK_M]
        k = qk[:, QK_M + h * DK_M:QK_M + (h + 1) * DK_M] * (DK_M ** -0.5)
        v = zv_ref[:, h * DV_M:(h + 1) * DV_M]
        qb, kb, vb = q.astype(BF16), k.astype(BF16), v.astype(BF16)
        sv, ssum, m_t, iw = _mlstm_intra(qb, kb, vb, mask, bt_c[:, h:h + 1], bt_r[h:h + 1, :],
                                         ig_r[h:h + 1, :], inter[:, h:h + 1])
        C = c0_ref[:, h]
        c_flat = C.reshape(NB * DV_M, DK_M)
        qc_all = _dot_nt(qb, c_flat.astype(BF16))
        qc = jnp.concatenate([qc_all[b * T:(b + 1) * T, b * DV_M:(b + 1) * DV_M] for b in range(NB)], axis=0)
        n_rows = jnp.broadcast_to(n0_ref[:, h:h + 1, :], (NB, T, DK_M)).reshape(L, DK_M)
        num = sv + iw * qc
        den = ssum + iw * jnp.sum(q * n_rows, axis=1, keepdims=True)
        h_ref[:, h * DV_M:(h + 1) * DV_M] = _head_out(num, den, m_t, zo_ref[:, h * DV_M:(h + 1) * DV_M], hw_ref)
        ws_h = ws[:, h:h + 1]
        vw = v * ws_h
        vw_exp = jnp.where(blockdiag, jnp.concatenate([vw] * NB, axis=1), 0.0).astype(BF16)
        upd = _dot_tn(vw_exp, kb).reshape(NB, DV_M, DK_M)
        dc = decay[:, h:h + 1].reshape(NB, T, 1)[:, 0:1, :]
        c_out[:, h] = dc * C + upd
        kw = (ws_h * k).reshape(NB, T, DK_M)
        n_out[:, h:h + 1, :] = dc * n0_ref[:, h:h + 1, :] + jnp.sum(kw, axis=1, keepdims=True)


def _mlstm_sample(z, gc, gr, hist, c0, n0, m0p, conv_w, conv_b, bias_c, bias_r, hnorm_w, S, B, T):
    NB = SEQ_BLOCK
    L = NB * T
    assert L == LANES and B % NB == 0 and S % L == 0
    r0 = S // L
    nz = lambda col, width: col // width
    return pl.pallas_call(
        functools.partial(_mlstm_sample_kernel, T=T),
        grid=(B // NB,),
        in_specs=[
            pl.BlockSpec((L, 2 * QK_M), lambda i: (r0 + i, nz(ZC_QK, 2 * QK_M))),
            pl.BlockSpec((L, W_M), lambda i: (r0 + i, nz(ZC_VM, W_M))),
            pl.BlockSpec((L, W_M), lambda i: (r0 + i, nz(ZC_OM, W_M))),
            pl.BlockSpec((L, 2 * LANES), lambda i: (r0 + i, 0)),
            pl.BlockSpec((2 * SUBLANES, L), lambda i: (0, r0 + i)),
            pl.BlockSpec((L, 2 * QK_M), lambda i: (i, 0)),
            pl.BlockSpec((NB, H_M, DV_M, DK_M), lambda i: (i, 0, 0, 0)),
            pl.BlockSpec((NB, H_M, DK_M), lambda i: (i, 0, 0)),
            pl.BlockSpec((L, LANES), lambda i: (i, 0)),
            pl.BlockSpec((CONV_W, 2 * QK_M), lambda i: (0, 0)),
            pl.BlockSpec((1, 2 * QK_M), lambda i: (0, 0)),
            pl.BlockSpec((1, 2 * LANES), lambda i: (0, 0)),
            pl.BlockSpec((2 * SUBLANES, 1), lambda i: (0, 0)),
            pl.BlockSpec((1, DV_M), lambda i: (0, 0)),
        ],
        out_specs=[
            pl.BlockSpec((L, W_M), lambda i: (i, 0)),
            pl.BlockSpec((NB, H_M, DV_M, DK_M), lambda i: (i, 0, 0, 0)),
            pl.BlockSpec((NB, H_M, DK_M), lambda i: (i, 0, 0)),
            pl.BlockSpec((L, LANES), lambda i: (i, 0)),
        ],
        out_shape=[
            jax.ShapeDtypeStruct((B * T, W_M), BF16),
            jax.ShapeDtypeStruct((B, H_M, DV_M, DK_M), F32),
            jax.ShapeDtypeStruct((B, H_M, DK_M), F32),
            jax.ShapeDtypeStruct((B * T, LANES), F32),
        ],
        compiler_params=_params("parallel"),
        name="mlstm_sample",
    )(z, z, z, gc, gr, hist, c0, n0, m0p, conv_w, conv_b, bias_c, bias_r, hnorm_w)


def _lambda(lq1, lk1, lq2, lk2):
    a = jnp.sum(lq1[...] * lk1[...], axis=-1, keepdims=True)
    b = jnp.sum(lq2[...] * lk2[...], axis=-1, keepdims=True)
    return jnp.exp(a) - jnp.exp(b) + LAM_INIT


def _subln(att, w_ref):
    y = att * lax.rsqrt(jnp.mean(att * att, axis=-1, keepdims=True) + EPS) * w_ref[...]
    return (y * (1.0 - LAM_INIT)).astype(BF16)


def _attn_prompt_kernel(q_ref, k_ref, v_ref, b0_ref, b1_ref, lq1, lk1, lq2, lk2, sw_ref,
                        o_ref, kb_s, vt_s, m_s, acc_s):
    qi = pl.program_id(1)
    T = q_ref.shape[0]
    n_tiles = kb_s.shape[0]

    @pl.when(qi == 0)
    def _():
        for t in range(n_tiles):
            kb_s[t] = k_ref[t * T:(t + 1) * T, :].astype(BF16)
            vt_s[t, 0:DV_D, :] = v_ref[t * T:(t + 1) * T, :].T.astype(BF16)
            vt_s[t, DV_D:, :] = jnp.ones((ONES_ROWS, T), BF16)

    q = q_ref[...] * (DK_D ** -0.5 * LOG2E)
    lane = lax.broadcasted_iota(jnp.int32, q.shape, 1)
    qpad = (jnp.where(lane < DK_D, q, 0.0).astype(BF16), jnp.where(lane >= DK_D, q, 0.0).astype(BF16))

    def group(tiles, state):
        scores = []
        for kj, bias in tiles:
            kt = kb_s[kj]
            for c in range(2):
                s = _dot_nt(kt, qpad[c])
                scores.append(s if bias is None else s + bias)
        parts = ([], [])
        for t, (kj, _) in enumerate(tiles):
            vt = vt_s[kj]
            for c in range(2):
                s = scores[2 * t + c]
                m = jnp.max(s, axis=0, keepdims=True)
                parts[c].append((m, _dot(vt, jnp.exp2(s - m).astype(BF16))))
        out = []
        for c in range(2):
            m_old, acc_old = state[c]
            m_new = m_old
            for m, _ in parts[c]:
                m_new = jnp.maximum(m_new, m)
            acc_new = jnp.exp2(m_old - m_new) * acc_old
            for m, pv in parts[c]:
                acc_new = acc_new + jnp.exp2(m - m_new) * pv
            out.append((m_new, acc_new))
        return tuple(out)

    def load():
        return tuple((m_s[c], acc_s[c]) for c in range(2))

    def store(state):
        for c in range(2):
            m_s[c], acc_s[c] = state[c]

    n_all = qi + 1
    n_head = functools.reduce(lambda acc, G: jnp.where(n_all >= G, jnp.maximum(acc, G), acc), HEAD_GROUPS, 0)
    n_far = n_all - n_head
    state = tuple((jnp.full((1, T), -jnp.inf, F32), jnp.zeros((DV_D + ONES_ROWS, T), F32)) for _ in range(2))
    done = 0
    for G in FAR_GROUPS:
        n_grp = (n_far - done) // G
        state = lax.fori_loop(0, n_grp, lambda g, st, G=G, done=done: group(
            [(done + g * G + t, None) for t in range(G)], st), state)
        done = done + n_grp * G
    store(state)

    for G in HEAD_GROUPS:
        @pl.when(n_head == G)
        def _(G=G):
            bias = [None] * (G - 2) + [b1_ref[0], b0_ref[0]]
            store(group([(qi - (G - 1) + t, bias[-G:][t]) for t in range(G)], load()))

    (_, a0), (_, a1) = load()
    lam = _lambda(lq1, lk1, lq2, lk2)
    att_t = a0[0:DV_D] / a0[DV_D:DV_D + 1] - lam * (a1[0:DV_D] / a1[DV_D:DV_D + 1])
    o_ref[...] = _subln(att_t.T, sw_ref)


def _attn_prompt(z, kp, vp, bias0, bias1, lq1, lk1, lq2, lk2, subln_w, S):
    T = bias0.shape[-1]
    hw = 2 * DK_D
    small = lambda shape: pl.BlockSpec(shape, lambda h, i: (0,) * len(shape))
    return pl.pallas_call(
        _attn_prompt_kernel,
        grid=(H_D, S // T),
        in_specs=[
            pl.BlockSpec((T, hw), lambda h, i: (i, ZC_QD // hw + h)),
            pl.BlockSpec((S, hw), lambda h, i: (0, h)),
            pl.BlockSpec((S, DV_D), lambda h, i: (0, h)),
            pl.BlockSpec((1, T, T), lambda h, i: (h, 0, 0)),
            pl.BlockSpec((1, T, T), lambda h, i: (h, 0, 0)),
            small((1, DK_D)), small((1, DK_D)), small((1, DK_D)), small((1, DK_D)),
            small((1, DV_D)),
        ],
        out_specs=pl.BlockSpec((T, DV_D), lambda h, i: (i, h)),
        out_shape=jax.ShapeDtypeStruct((S, W_D), BF16),
        scratch_shapes=[
            pltpu.VMEM((S // T, T, hw), BF16),
            pltpu.VMEM((S // T, DV_D + ONES_ROWS, T), BF16),
            pltpu.VMEM((2, 1, T), F32),
            pltpu.VMEM((2, DV_D + ONES_ROWS, T), F32),
        ],
        compiler_params=_params("arbitrary", "arbitrary"),
        name="attn_prompt",
    )(z, kp, vp, bias0, bias1, lq1, lk1, lq2, lk2, subln_w)


def _attn_sample_kernel(pt_ref, q_ref, *refs, pps):
    kc = refs[0:pps]
    vc = refs[pps:2 * pps]
    (kn_ref, vn_ref, bfar_ref, blast_ref, bnew_ref, lq1, lk1, lq2, lk2, sw_ref,
     o_ref, qbd_s, m_s, acc_s) = refs[2 * pps:]
    p = pl.program_id(1)
    last_step = p == pl.num_programs(1) - 1
    HT = q_ref.shape[1]

    @pl.when(p == 0)
    def _():
        q = q_ref[0] * (DK_D ** -0.5 * LOG2E)
        lane = lax.broadcasted_iota(jnp.int32, q.shape, 1)
        qbd_s[0:HT, :] = jnp.where(lane < DK_D, q, 0.0).astype(BF16)
        qbd_s[HT:2 * HT, :] = jnp.where(lane >= DK_D, q, 0.0).astype(BF16)
        m_s[...] = jnp.full_like(m_s, -jnp.inf)
        acc_s[...] = jnp.zeros_like(acc_s)

    def update(tiles):
        qbd = qbd_s[...]
        parts = []
        for k, v, b in tiles:
            s = _dot_nt(qbd, k.astype(BF16)) + b
            m = jnp.max(s, axis=1, keepdims=True)
            pr = jnp.exp2(s - m).astype(BF16)
            v_ext = jnp.concatenate([v.astype(BF16), jnp.ones(v.shape, BF16)], axis=1)
            parts.append((m, _dot(pr, v_ext)))
        m_old = m_s[...]
        m_new = m_old
        for m, _ in parts:
            m_new = jnp.maximum(m_new, m)
        acc = jnp.exp2(m_old - m_new) * acc_s[...]
        for m, pv in parts:
            acc = acc + jnp.exp2(m - m_new) * pv
        acc_s[...] = acc
        m_s[...] = m_new

    rows = kc[0].shape[0] * kc[0].shape[1]
    bfar = bfar_ref[...]
    b_end = jnp.where(last_step, blast_ref[...], bfar)
    update([(kc[i][...].reshape(rows, 2 * DK_D), vc[i][...].reshape(rows, DV_D), bfar if i < pps - 1 else b_end)
            for i in range(pps)])

    @pl.when(last_step)
    def _():
        update([(kn_ref[0], vn_ref[0], bnew_ref[...])])
        lam = _lambda(lq1, lk1, lq2, lk2)
        r = acc_s[:, 0:DV_D] / acc_s[:, DV_D:DV_D + 1]
        o_ref[0] = _subln(r[0:HT] - lam * r[HT:2 * HT], sw_ref)


def _attn_sample(page_table, qs, cache_k, cache_v, kn, vn, bfar, blast, bnew, lq1, lk1, lq2, lk2, subln_w):
    B, HT, _ = qs.shape
    n_pages = page_table.shape[1]
    PG = cache_k.shape[2]
    pps = _pick(n_pages, (PAGES_PER_STEP, 4, 2, 1))
    small = lambda shape: pl.BlockSpec(shape, lambda b, p, pt: (0,) * len(shape))

    def page_spec(i, width):
        return pl.BlockSpec((None, None, PG, H_D, width), lambda b, p, pt: (0, pt[b, p * pps + i], 0, 0, 0))

    grid_spec = pltpu.PrefetchScalarGridSpec(
        num_scalar_prefetch=1,
        grid=(B, n_pages // pps),
        in_specs=(
            [pl.BlockSpec((1, HT, 2 * DK_D), lambda b, p, pt: (b, 0, 0))]
            + [page_spec(i, 2 * DK_D) for i in range(pps)]
            + [page_spec(i, DV_D) for i in range(pps)]
            + [pl.BlockSpec((1, HT, 2 * DK_D), lambda b, p, pt: (b, 0, 0)),
               pl.BlockSpec((1, HT, DV_D), lambda b, p, pt: (b, 0, 0)),
               small(bfar.shape), small(blast.shape), small(bnew.shape),
               small((1, DK_D)), small((1, DK_D)), small((1, DK_D)), small((1, DK_D)),
               small((1, DV_D))]),
        out_specs=pl.BlockSpec((1, HT, DV_D), lambda b, p, pt: (b, 0, 0)),
        scratch_shapes=[
            pltpu.VMEM((2 * HT, 2 * DK_D), BF16),
            pltpu.VMEM((2 * HT, 1), F32),
            pltpu.VMEM((2 * HT, 2 * DV_D), F32),
        ],
    )
    return pl.pallas_call(
        functools.partial(_attn_sample_kernel, pps=pps),
        grid_spec=grid_spec,
        out_shape=jax.ShapeDtypeStruct((B, HT, DV_D), BF16),
        compiler_params=_params("arbitrary", "arbitrary"),
        name="attn_sample",
    )(page_table, qs, *([cache_k] * pps), *([cache_v] * pps), kn, vn, bfar, blast, bnew,
      lq1, lk1, lq2, lk2, subln_w)


def _merge_kernel(hmp_ref, hms_ref, atp_ref, ats_ref, wa_ref, wb_ref, ga_ref, gb_ref, u_ref, *, n_prompt):
    def body(hm_ref, at_ref):
        ya = _dot(hm_ref[...], wa_ref[...])
        yb = _dot(at_ref[...], wb_ref[...])
        u_ref[...] = (_sigmoid(ga_ref[...]) * ya + _sigmoid(gb_ref[...]) * yb).astype(BF16)

    @pl.when(pl.program_id(0) < n_prompt)
    def _():
        body(hmp_ref, atp_ref)

    @pl.when(pl.program_id(0) >= n_prompt)
    def _():
        body(hms_ref, ats_ref)


def _merge(hm_p, hm_s, att_p, att_s, w_a, w_b, z, D):
    S, BT = hm_p.shape[0], hm_s.shape[0]
    tm, n_prompt, n_sample, prow, srow = _two_way_rows(S, BT, (1024, 512, 256, 128))
    tn = _pick(D, (512, 256, 128))
    ga0, gb0 = ZC_GA // tn, (ZC_GA + D) // tn
    return pl.pallas_call(
        functools.partial(_merge_kernel, n_prompt=n_prompt),
        grid=(n_prompt + n_sample, D // tn),
        in_specs=[
            pl.BlockSpec((tm, W_M), lambda i, j: (prow(i), 0)),
            pl.BlockSpec((tm, W_M), lambda i, j: (srow(i), 0)),
            pl.BlockSpec((tm, W_D), lambda i, j: (prow(i), 0)),
            pl.BlockSpec((tm, W_D), lambda i, j: (srow(i), 0)),
            pl.BlockSpec((W_M, tn), lambda i, j: (0, j)),
            pl.BlockSpec((W_D, tn), lambda i, j: (0, j)),
            pl.BlockSpec((tm, tn), lambda i, j: (i, ga0 + j)),
            pl.BlockSpec((tm, tn), lambda i, j: (i, gb0 + j)),
        ],
        out_specs=pl.BlockSpec((tm, tn), lambda i, j: (i, j)),
        out_shape=jax.ShapeDtypeStruct((S + BT, D), BF16),
        compiler_params=_params("parallel", "parallel"),
        name="merge",
    )(hm_p, hm_s, att_p, att_s, w_a, w_b, z, z)


def _out_proj_kernel(u_ref, w_ref, xp_ref, xs_ref, o_ref, *, n_prompt):
    y = _dot(u_ref[...], w_ref[...])

    @pl.when(pl.program_id(0) < n_prompt)
    def _():
        o_ref[...] = xp_ref[...] + y

    @pl.when(pl.program_id(0) >= n_prompt)
    def _():
        o_ref[...] = xs_ref[...] + y


def _out_proj(u, w_out, xp, xs):
    S, D = xp.shape
    BT = xs.shape[0]
    tm, n_prompt, n_sample, prow, srow = _two_way_rows(S, BT, (1024, 512, 256, 128))
    tn = _pick(D, (512, 256, 128))
    return pl.pallas_call(
        functools.partial(_out_proj_kernel, n_prompt=n_prompt),
        grid=(n_prompt + n_sample, D // tn),
        in_specs=[
            pl.BlockSpec((tm, D), lambda i, j: (i, 0)),
            pl.BlockSpec((D, tn), lambda i, j: (0, j)),
            pl.BlockSpec((tm, tn), lambda i, j: (prow(i), jnp.where(i < n_prompt, j, D // tn - 1))),
            pl.BlockSpec((tm, tn), lambda i, j: (srow(i), jnp.where(i < n_prompt, 0, j))),
        ],
        out_specs=pl.BlockSpec((tm, tn), lambda i, j: (i, j)),
        out_shape=jax.ShapeDtypeStruct((S + BT, D), F32),
        compiler_params=_params("parallel", "parallel"),
        name="out_proj",
    )(u, w_out, xp, xs)


def _ffn_kernel(x_ref, nw_ref, w1_ref, w2_ref, fw_ref, yp_ref, ys_ref, xn_s, acc_s, *, n_prompt):
    f = pl.program_id(1)

    @pl.when(f == 0)
    def _():
        x = x_ref[...]
        ms = jnp.mean(x * x, axis=-1, keepdims=True)
        xn_s[...] = (x * lax.rsqrt(ms + EPS) * nw_ref[...]).astype(BF16)
        acc_s[...] = jnp.zeros_like(acc_s)

    hid = jnp.maximum(_dot(xn_s[...], w1_ref[...]), 0.0)
    acc_s[...] += _dot((hid * hid).astype(BF16), w2_ref[...])

    def final(y_ref):
        x2 = x_ref[...] + acc_s[...]
        ms = jnp.mean(x2 * x2, axis=-1, keepdims=True)
        y_ref[...] = x2 * lax.rsqrt(ms + EPS) * fw_ref[...]

    last = f == pl.num_programs(1) - 1

    @pl.when(last & (pl.program_id(0) < n_prompt))
    def _():
        final(yp_ref)

    @pl.when(last & (pl.program_id(0) >= n_prompt))
    def _():
        final(ys_ref)


def _ffn(x, norm_w, w1, w2, final_w, S):
    R, D = x.shape
    DF = w1.shape[1]
    tm, n_prompt, n_sample, prow, srow = _two_way_rows(S, R - S, (512, 256, 128))
    tf = _pick(DF, (1024, 512, 256, 128))
    return pl.pallas_call(
        functools.partial(_ffn_kernel, n_prompt=n_prompt),
        grid=(n_prompt + n_sample, DF // tf),
        in_specs=[
            pl.BlockSpec((tm, D), lambda i, f: (i, 0)),
            pl.BlockSpec((1, D), lambda i, f: (0, 0)),
            pl.BlockSpec((D, tf), lambda i, f: (0, f)),
            pl.BlockSpec((tf, D), lambda i, f: (f, 0)),
            pl.BlockSpec((1, D), lambda i, f: (0, 0)),
        ],
        out_specs=[
            pl.BlockSpec((tm, D), lambda i, f: (prow(i), 0)),
            pl.BlockSpec((tm, D), lambda i, f: (srow(i), 0)),
        ],
        out_shape=[jax.ShapeDtypeStruct((S, D), F32), jax.ShapeDtypeStruct((R - S, D), F32)],
        scratch_shapes=[pltpu.VMEM((tm, D), BF16), pltpu.VMEM((tm, D), F32)],
        compiler_params=_params("arbitrary", "arbitrary"),
        name="ffn",
    )(x, norm_w, w1, w2, final_w)


def _bias_by_distance(rel_bias, n):
    d = jnp.arange(n, dtype=jnp.int32)
    max_exact = N_BUCKETS // 2
    nf = jnp.maximum(d, 1).astype(F32)
    large = max_exact + (jnp.log(nf / max_exact) / math.log(MAX_DIST / max_exact)
                         * (N_BUCKETS - max_exact)).astype(jnp.int32)
    large = jnp.minimum(large, N_BUCKETS - 1)
    bucket = jnp.where(d < max_exact, d, large)
    onehot = (bucket[:, None] == jnp.arange(N_BUCKETS)[None, :]).astype(F32)
    return jnp.dot(onehot, rel_bias.astype(F32), precision=HIGHEST).T


def _toeplitz(w, rows, cols):
    n = w.shape[-1]
    assert cols <= n - 1
    lead = w.shape[:-1]
    flat = jnp.tile(w, (1,) * len(lead) + (rows,))[..., :rows * (n - 1)]
    return flat.reshape(lead + (rows, n - 1))[..., :cols]


def _prompt_bias_tiles(rel_bias, T):
    assert T + 1 >= MAX_DIST
    bd = _bias_by_distance(rel_bias, 2 * T)
    val = (bd - bd[:, 2 * T - 1:]) * LOG2E
    neg = jnp.full((H_D, T), NEG, F32)
    t0 = _toeplitz(jnp.concatenate([val[:, :T], neg], axis=1), T, T)
    t1 = _toeplitz(jnp.concatenate([val[:, T:], val[:, :T]], axis=1), T, T)
    return t0, t1


def _sample_bias_tables(rel_bias, T, PG):
    assert PG + 1 >= MAX_DIST
    bd = _bias_by_distance(rel_bias, 2 * PG + T) * LOG2E
    HT = T * H_D
    eye = jnp.asarray(np.eye(H_D, dtype=bool))

    def expand(tab):
        K = tab.shape[-1]
        full = jnp.where(eye[None, :, None, :], jnp.transpose(tab, (1, 0, 2))[:, :, :, None], NEG)
        full = full.reshape(HT, K * H_D)
        return jnp.concatenate([full, full], axis=0).astype(F32)

    far = expand(jnp.broadcast_to(bd[:, 2 * PG + T - 1][:, None, None], (H_D, T, PG)))
    w_last = jnp.concatenate([bd[:, PG:0:-1], bd[:, :1], bd[:, PG + T - 1:PG:-1]], axis=1)
    last = expand(_toeplitz(w_last, T, PG))
    w_new = jnp.concatenate([bd[:, :1], jnp.full((H_D, T), NEG, F32), bd[:, T - 1:0:-1]], axis=1)
    new = expand(_toeplitz(w_new, T, T))
    return far, last, new


def kernel(x_prompt, x_sample, cache_k, cache_v, page_table, state_C, state_n, state_m, state_conv,
           norm1_w, w_in, b_i, b_f, conv_w, conv_b, hnorm_w, lambda_q1, lambda_k1, lambda_q2, lambda_k2,
           subln_w, rel_bias, w_a, w_b, w_out, norm2_w, w_ff1, w_ff2, final_norm_w):
    assert w_in.shape[0] == 1 and x_prompt.shape[0] == 1
    _, S, D = x_prompt.shape
    B, T, _ = x_sample.shape
    PG = cache_k.shape[2]
    xp = x_prompt[0]
    xs = x_sample.reshape(B * T, D)

    o_i = 2 * QK_M + 2 * W_M
    o_qd = o_i + 2 * H_M
    w_t = jnp.transpose(w_in[0])
    w_all = w_t.astype(BF16)
    w_rest = w_all[o_qd:]
    w_gate_row = w_all[o_i:o_qd]
    gate_pad = jnp.zeros((LANES - H_M, D), BF16)
    w_gate_col = jnp.concatenate([w_gate_row[:H_M], gate_pad, w_gate_row[H_M:], gate_pad], axis=0)
    bias_c = jnp.zeros((1, 2 * LANES), F32).at[0, 0:H_M].set(b_i[0]).at[0, LANES:LANES + H_M].set(b_f[0])
    bias_r = jnp.concatenate([b_i[0], b_f[0]])[:, None]

    z, kp, ks, vp, vs, gc, gr = _in_proj(xp, xs, norm1_w, w_all, w_rest, w_gate_col, w_gate_row)

    hm_p, c_p, n_p, m_p, conv_p = _mlstm_prompt(z, gc, gr, conv_w[0], conv_b, bias_c, bias_r, hnorm_w, S)
    hist = jnp.pad(state_conv[0], ((0, 0), (T - (CONV_W - 1), 0), (0, 0))).reshape(B * T, 2 * QK_M)
    m0p = jnp.repeat(jnp.pad(state_m[0], ((0, 0), (0, LANES - H_M))), T, axis=0)
    hm_s, c_s, n_s, m_s = _mlstm_sample(z, gc, gr, hist, state_C[0], state_n[0], m0p, conv_w[0], conv_b,
                                        bias_c, bias_r, hnorm_w, S, B, T)

    TQ = _pick(S, (256, 128))
    t0, t1 = _prompt_bias_tiles(rel_bias, TQ)
    lq1, lk1, lq2, lk2 = lambda_q1, lambda_k1, lambda_q2, lambda_k2
    att_p = _attn_prompt(z, kp, vp, t0, t1, lq1, lk1, lq2, lk2, subln_w, S)
    bfar, blast, bnew = _sample_bias_tables(rel_bias, T, PG)
    qs = z[S:, ZC_QD:ZC_QD + QK_D].reshape(B, T * H_D, 2 * DK_D)
    kn = ks.reshape(B, T * H_D, 2 * DK_D)
    vn = vs.reshape(B, T * H_D, DV_D)
    att_s = _attn_sample(page_table, qs, cache_k, cache_v, kn, vn, bfar, blast, bnew,
                         lq1, lk1, lq2, lk2, subln_w)

    u = _merge(hm_p, hm_s, att_p, att_s.reshape(B * T, W_D), w_a[0].astype(BF16), w_b[0].astype(BF16), z, D)
    x1 = _out_proj(u, w_out[0].astype(BF16), xp, xs)
    y_p, y_s = _ffn(x1, norm2_w, w_ff1[0].astype(BF16), w_ff2[0].astype(BF16), final_norm_w[None, :], S)

    conv_prompt = conv_p[SUBLANES - (CONV_W - 1):].reshape(1, 1, CONV_W - 1, 2 * QK_M)
    conv_sample = z[S:, :2 * QK_M].reshape(B, T, 2 * QK_M)[:, T - (CONV_W - 1):][None]
    return (y_p.reshape(1, S, D), y_s.reshape(B, T, D),
            kp.reshape(1, 1, S, H_D, 2 * DK_D), vp.reshape(1, 1, S, H_D, DV_D),
            c_p[None, None], n_p[None, None], m_p[:, :H_M][None], conv_prompt,
            ks.reshape(1, B, T, H_D, 2 * DK_D), vs.reshape(1, B, T, H_D, DV_D),
            c_s[None], n_s[None], m_s[::T, :H_M][None], conv_sample)
```

```python
import functools
import math

import numpy as np
import jax
import jax.numpy as jnp
from jax import lax
from jax.experimental import pallas as pl
from jax.experimental.pallas import tpu as pltpu

F32 = jnp.float32
BF16 = jnp.bfloat16
HIGHEST = lax.Precision.HIGHEST

H_M = 8
DK_M = 128
DV_M = 128
QK_M = H_M * DK_M
W_M = H_M * DV_M
CONV_W = 4
H_D = 8
DK_D = 64
DV_D = 128
QK_D = H_D * 2 * DK_D
W_D = H_D * DV_D
N_BUCKETS = 32
MAX_DIST = 128
EPS = 1e-6
LAM_INIT = 0.8 - 0.6 * math.exp(-0.3 * 0)
NEG = -1e30
LOG2E = math.log2(math.e)

ZC_QK = 0
ZC_VM = 2 * QK_M
ZC_OM = ZC_VM + W_M
ZC_QD = ZC_OM + W_M
ZC_GA = ZC_QD + QK_D

LANES = 128
SUBLANES = 8
VMEM_LIMIT = 56 * 1024 * 1024

SEQ_BLOCK = 16
PAGES_PER_STEP = 8
FAR_GROUPS = (8, 4, 2, 1)
HEAD_GROUPS = (8, 4, 2, 1)
ONES_ROWS = 16


def _params(*sem):
    return pltpu.CompilerParams(dimension_semantics=sem, vmem_limit_bytes=VMEM_LIMIT)


def _pick(n, prefs):
    for p in prefs:
        if n % p == 0:
            return p
    return n


def _sigmoid(x):
    return 1.0 / (1.0 + jnp.exp(-x))


def _log_sigmoid(x):
    return jnp.minimum(x, 0.0) - jnp.log(1.0 + jnp.exp(-jnp.abs(x)))


def _dot(a, b):
    return jnp.dot(a, b, preferred_element_type=F32)


def _dot_nt(a, b):
    return lax.dot_general(a, b, (((1,), (1,)), ((), ())), preferred_element_type=F32)


def _dot_tn(a, b):
    return lax.dot_general(a, b, (((0,), (0,)), ((), ())), preferred_element_type=F32)


def _dot_exact(a, b):
    return jnp.dot(a, b, preferred_element_type=F32, precision=HIGHEST)


def _two_way_rows(S, BT, prefs):
    tm = _pick(math.gcd(S, BT), prefs)
    n_prompt = S // tm
    prow = lambda i: jnp.minimum(i, n_prompt - 1)
    srow = lambda i: jnp.maximum(i - n_prompt, 0)
    return tm, n_prompt, BT // tm, prow, srow


def _in_proj_kernel(xp_ref, xs_ref, nw_ref, wa_ref, wb_ref, wgc_ref, wgr_ref,
                    z_ref, kp_ref, ks_ref, vp_ref, vs_ref, gc_ref, gr_ref, xn_ref, *, n_prompt, n_a, n_z1, nkt):
    i = pl.program_id(0)
    j = pl.program_id(1)
    is_p = i < n_prompt
    is_s = jnp.logical_not(is_p)

    def norm(x_ref):
        x = x_ref[...]
        ms = jnp.mean(x * x, axis=-1, keepdims=True)
        xn = (x * lax.rsqrt(ms + EPS) * nw_ref[...]).astype(BF16)
        xn_ref[...] = xn
        gc_ref[...] = _dot_nt(xn, wgc_ref[...])
        gr_ref[...] = _dot_nt(wgr_ref[...], xn)

    @pl.when((j == 0) & is_p)
    def _():
        norm(xp_ref)

    @pl.when((j == 0) & is_s)
    def _():
        norm(xs_ref)

    in_k = (j >= n_z1) & (j < n_z1 + nkt)
    in_v = (j >= n_z1 + nkt) & (j < n_z1 + 2 * nkt)
    in_zb = (j >= n_a) & jnp.logical_not(in_k | in_v)
    for cond, w_ref, o_ref in ((j < n_a, wa_ref, z_ref), (in_zb, wb_ref, z_ref),
                               (in_k & is_p, wb_ref, kp_ref), (in_k & is_s, wb_ref, ks_ref),
                               (in_v & is_p, wb_ref, vp_ref), (in_v & is_s, wb_ref, vs_ref)):
        @pl.when(cond)
        def _(w_ref=w_ref, o_ref=o_ref):
            o_ref[...] = _dot_nt(xn_ref[...], w_ref[...])


def _in_proj(xp, xs, norm_w, w_all, w_rest, w_gate_col, w_gate_row):
    S, D = xp.shape
    BT = xs.shape[0]
    tm, n_prompt, n_sample, prow, srow = _two_way_rows(S, BT, (1024, 512, 256, 128))
    tn = _pick(math.gcd(D, QK_D), (512, 256, 128))
    n_a = ZC_QD // tn
    n_z1 = ZC_GA // tn
    nkt = QK_D // tn
    n_tiles = n_a + w_rest.shape[0] // tn
    nzt = n_tiles - 2 * nkt
    zcol = lambda j: jnp.where(j < n_z1, j, jnp.where(j < n_z1 + 2 * nkt, n_z1 - 1, j - 2 * nkt))
    kcol = lambda j: jnp.clip(j - n_z1, 0, nkt - 1)
    vcol = lambda j: jnp.clip(j - n_z1 - nkt, 0, nkt - 1)
    p_spec = lambda col: pl.BlockSpec((tm, tn), lambda i, j: (prow(i), jnp.where(i < n_prompt, col(j), nkt - 1)))
    s_spec = lambda col: pl.BlockSpec((tm, tn), lambda i, j: (srow(i), jnp.where(i < n_prompt, 0, col(j))))
    return pl.pallas_call(
        functools.partial(_in_proj_kernel, n_prompt=n_prompt, n_a=n_a, n_z1=n_z1, nkt=nkt),
        grid=(n_prompt + n_sample, n_tiles),
        in_specs=[
            pl.BlockSpec((tm, D), lambda i, j: (prow(i), 0), pipeline_mode=pl.Buffered(1)),
            pl.BlockSpec((tm, D), lambda i, j: (srow(i), 0), pipeline_mode=pl.Buffered(1)),
            pl.BlockSpec((1, D), lambda i, j: (0, 0)),
            pl.BlockSpec((tn, D), lambda i, j: (jnp.minimum(j, n_a - 1), 0)),
            pl.BlockSpec((tn, D), lambda i, j: (jnp.maximum(j - n_a, 0), 0)),
            pl.BlockSpec((2 * LANES, D), lambda i, j: (0, 0)),
            pl.BlockSpec((2 * SUBLANES, D), lambda i, j: (0, 0)),
        ],
        out_specs=[
            pl.BlockSpec((tm, tn), lambda i, j: (i, zcol(j))),
            p_spec(kcol), s_spec(kcol), p_spec(vcol), s_spec(vcol),
            pl.BlockSpec((tm, 2 * LANES), lambda i, j: (i, 0)),
            pl.BlockSpec((2 * SUBLANES, tm), lambda i, j: (0, i)),
        ],
        out_shape=[
            jax.ShapeDtypeStruct((S + BT, nzt * tn), F32),
            jax.ShapeDtypeStruct((S, QK_D), F32),
            jax.ShapeDtypeStruct((BT, QK_D), F32),
            jax.ShapeDtypeStruct((S, W_D), F32),
            jax.ShapeDtypeStruct((BT, W_D), F32),
            jax.ShapeDtypeStruct((S + BT, 2 * LANES), F32),
            jax.ShapeDtypeStruct((2 * SUBLANES, S + BT), F32),
        ],
        scratch_shapes=[pltpu.VMEM((tm, D), BF16)],
        compiler_params=_params("arbitrary", "arbitrary"),
        name="in_proj",
    )(xp, xs, norm_w, w_all, w_rest, w_gate_col, w_gate_row)


def _conv_silu(x, hist, hist_shift, cw_ref, cb_ref, row_in_seq):
    acc = cb_ref[...] + cw_ref[CONV_W - 1:CONV_W, :] * x
    for j in range(1, CONV_W):
        xr = pltpu.roll(x, j, axis=0)
        hr = pltpu.roll(hist, (j + hist_shift) % hist.shape[0], axis=0)
        if hist.shape[0] != x.shape[0]:
            first = jnp.where(row_in_seq[0:SUBLANES] < j, hr, xr[0:SUBLANES])
            xs = jnp.concatenate([first, xr[SUBLANES:]], axis=0)
        else:
            xs = jnp.where(row_in_seq < j, hr, xr)
        acc = acc + cw_ref[CONV_W - 1 - j:CONV_W - j, :] * xs
    return acc * _sigmoid(acc)


def _mlstm_intra(qb, kb, vb, mask, bt_c, bt_r, ig_r, inter_c):
    dlog = jnp.where(mask, bt_c - bt_r + ig_r, -jnp.inf)
    m_t = jnp.maximum(inter_c, jnp.max(dlog, axis=1, keepdims=True))
    dw = jnp.exp(dlog - m_t)
    iw = jnp.exp(inter_c - m_t)
    s = _dot_nt(qb, kb) * dw
    sv = _dot(s.astype(BF16), vb)
    return sv, jnp.sum(s, axis=1, keepdims=True), m_t, iw


def _head_out(num, den, m_t, o, hw_ref):
    den = jnp.maximum(jnp.abs(den), jnp.exp(-m_t))
    h = num / den
    hn = h * lax.rsqrt(jnp.mean(h * h, axis=-1, keepdims=True) + EPS) * hw_ref[...]
    return (hn * _sigmoid(o)).astype(BF16)


def _mlstm_prompt_kernel(zq_ref, zv_ref, zo_ref, gc_ref, gr_ref, cw_ref, cb_ref, bc_ref, br_ref, hw_ref,
                         h_ref, c_out, n_out, m_out, conv_out,
                         c_s, n_s, m_s, hist_s):
    c = pl.program_id(0)
    L = zq_ref.shape[0]

    @pl.when(c == 0)
    def _():
        c_s[...] = jnp.zeros_like(c_s)
        n_s[...] = jnp.zeros_like(n_s)
        m_s[...] = jnp.zeros_like(m_s)
        hist_s[...] = jnp.zeros_like(hist_s)

    x = zq_ref[...]
    row = lax.broadcasted_iota(jnp.int32, (L, 1), 0)
    qk = _conv_silu(x, hist_s[...], 0, cw_ref, cb_ref, row)
    hist_s[...] = x[L - SUBLANES:L, :]
    conv_out[...] = x[L - SUBLANES:L, :]

    gcol = gc_ref[...] + bc_ref[...]
    grow = gr_ref[...] + br_ref[...]
    ig_c = gcol[:, 0:LANES]
    lf_c = _log_sigmoid(gcol[:, LANES:2 * LANES])
    ig_r = grow[0:SUBLANES, :]
    lf_r = _log_sigmoid(grow[SUBLANES:2 * SUBLANES, :])
    ri = lax.broadcasted_iota(jnp.int32, (L, L), 0)
    ci = lax.broadcasted_iota(jnp.int32, (L, L), 1)
    mask = ci <= ri
    bt_c = _dot_exact(mask.astype(F32), lf_c)
    bt_r = _dot_exact(lf_r, (ri <= ci).astype(F32))
    m_prev = m_s[...]
    inter = bt_c + m_prev
    b_last = bt_c[L - 1:L, :]
    wlog = b_last - bt_c + ig_c
    m_new = jnp.maximum(b_last + m_prev, jnp.max(wlog, axis=0, keepdims=True))
    ws = jnp.exp(wlog - m_new)
    decay = jnp.exp(b_last + m_prev - m_new)
    m_s[...] = m_new
    m_out[...] = m_new

    for h in range(H_M):
        q = qk[:, h * DK_M:(h + 1) * DK_M]
        k = qk[:, QK_M + h * DK_M:QK_M + (h + 1) * DK_M] * (DK_M ** -0.5)
        v = zv_ref[:, h * DV_M:(h + 1) * DV_M]
        qb, kb, vb = q.astype(BF16), k.astype(BF16), v.astype(BF16)
        sv, ssum, m_t, iw = _mlstm_intra(qb, kb, vb, mask, bt_c[:, h:h + 1], bt_r[h:h + 1, :],
                                         ig_r[h:h + 1, :], inter[:, h:h + 1])
        C = c_s[h]
        n_row = n_s[h:h + 1, :]
        num = sv + iw * _dot_nt(qb, C.astype(BF16))
        den = ssum + iw * jnp.sum(q * n_row, axis=1, keepdims=True)
        h_ref[:, h * DV_M:(h + 1) * DV_M] = _head_out(num, den, m_t, zo_ref[:, h * DV_M:(h + 1) * DV_M], hw_ref)
        ws_h = ws[:, h:h + 1]
        dc = decay[:, h:h + 1]
        c_new = dc * C + _dot_tn((v * ws_h).astype(BF16), kb)
        n_new = dc * n_row + jnp.sum(ws_h * k, axis=0, keepdims=True)
        c_s[h] = c_new
        n_s[h:h + 1, :] = n_new
        c_out[h] = c_new
        n_out[h:h + 1, :] = n_new


def _mlstm_prompt(z, gc, gr, conv_w, conv_b, bias_c, bias_r, hnorm_w, S):
    L = _pick(S, (256, 128))
    nz = lambda col, width: col // width
    return pl.pallas_call(
        _mlstm_prompt_kernel,
        grid=(S // L,),
        in_specs=[
            pl.BlockSpec((L, 2 * QK_M), lambda c: (c, nz(ZC_QK, 2 * QK_M))),
            pl.BlockSpec((L, W_M), lambda c: (c, nz(ZC_VM, W_M))),
            pl.BlockSpec((L, W_M), lambda c: (c, nz(ZC_OM, W_M))),
            pl.BlockSpec((L, 2 * LANES), lambda c: (c, 0)),
            pl.BlockSpec((2 * SUBLANES, L), lambda c: (0, c)),
            pl.BlockSpec((CONV_W, 2 * QK_M), lambda c: (0, 0)),
            pl.BlockSpec((1, 2 * QK_M), lambda c: (0, 0)),
            pl.BlockSpec((1, 2 * LANES), lambda c: (0, 0)),
            pl.BlockSpec((2 * SUBLANES, 1), lambda c: (0, 0)),
            pl.BlockSpec((1, DV_M), lambda c: (0, 0)),
        ],
        out_specs=[
            pl.BlockSpec((L, W_M), lambda c: (c, 0)),
            pl.BlockSpec((H_M, DV_M, DK_M), lambda c: (0, 0, 0)),
            pl.BlockSpec((H_M, DK_M), lambda c: (0, 0)),
            pl.BlockSpec((1, LANES), lambda c: (0, 0)),
            pl.BlockSpec((SUBLANES, 2 * QK_M), lambda c: (0, 0)),
        ],
        out_shape=[
            jax.ShapeDtypeStruct((S, W_M), BF16),
            jax.ShapeDtypeStruct((H_M, DV_M, DK_M), F32),
            jax.ShapeDtypeStruct((H_M, DK_M), F32),
            jax.ShapeDtypeStruct((1, LANES), F32),
            jax.ShapeDtypeStruct((SUBLANES, 2 * QK_M), F32),
        ],
        scratch_shapes=[
            pltpu.VMEM((H_M, DV_M, DK_M), F32),
            pltpu.VMEM((H_M, DK_M), F32),
            pltpu.VMEM((1, LANES), F32),
            pltpu.VMEM((SUBLANES, 2 * QK_M), F32),
        ],
        compiler_params=_params("arbitrary"),
        name="mlstm_prompt",
    )(z, z, z, gc, gr, conv_w, conv_b, bias_c, bias_r, hnorm_w)


def _mlstm_sample_kernel(zq_ref, zv_ref, zo_ref, gc_ref, gr_ref, hist_ref, c0_ref, n0_ref, m0_ref,
                         cw_ref, cb_ref, bc_ref, br_ref, hw_ref,
                         h_ref, c_out, n_out, m_out, *, T):
    L = zq_ref.shape[0]
    NB = L // T
    x = zq_ref[...]
    ri = lax.broadcasted_iota(jnp.int32, (L, L), 0)
    ci = lax.broadcasted_iota(jnp.int32, (L, L), 1)
    same = (ri // T) == (ci // T)
    mask = same & (ci <= ri)
    row_t = lax.broadcasted_iota(jnp.int32, (L, 1), 0) % T
    qk = _conv_silu(x, hist_ref[...], L - T, cw_ref, cb_ref, row_t)

    gcol = gc_ref[...] + bc_ref[...]
    grow = gr_ref[...] + br_ref[...]
    ig_c = gcol[:, 0:LANES]
    lf_c = _log_sigmoid(gcol[:, LANES:2 * LANES])
    ig_r = grow[0:SUBLANES, :]
    lf_r = _log_sigmoid(grow[SUBLANES:2 * SUBLANES, :])
    bt_c = _dot_exact(mask.astype(F32), lf_c)
    bt_r = _dot_exact(lf_r, (same & (ri <= ci)).astype(F32))
    m_prev = m0_ref[...]
    last = same & (ci % T == T - 1)
    b_last = _dot_exact(last.astype(F32), bt_c)
    inter = bt_c + m_prev
    wlog = b_last - bt_c + ig_c
    wmax = jnp.max(wlog.reshape(NB, T, LANES), axis=1, keepdims=True)
    wmax = jnp.broadcast_to(wmax, (NB, T, LANES)).reshape(L, LANES)
    m_new = jnp.maximum(b_last + m_prev, wmax)
    ws = jnp.exp(wlog - m_new)
    decay = jnp.exp(b_last + m_prev - m_new)
    m_out[...] = m_new

    lane_seq = lax.broadcasted_iota(jnp.int32, (L, NB * DV_M), 1) // DV_M
    row_seq = lax.broadcasted_iota(jnp.int32, (L, NB * DV_M), 0) // T
    blockdiag = lane_seq == row_seq

    for h in range(H_M):
        q = qk[:, h * DK_M:(h + 1) * DK_M]
        k = qk[:, QK_M + h * DK_M:QK_M + (h + 1) * DK_M] * (DK_M ** -0.5)
        v = zv_ref[:, h * DV_M:(h + 1) * DV_M]
        qb, kb, vb = q.astype(BF16), k.astype(BF16), v.astype(BF16)
        sv, ssum, m_t, iw = _mlstm_intra(qb, kb, vb, mask, bt_c[:, h:h + 1], bt_r[h:h + 1, :],
                                         ig_r[h:h + 1, :], inter[:, h:h + 1])
        C = c0_ref[:, h]
        c_flat = C.reshape(NB * DV_M, DK_M)
        qc_all = _dot_nt(qb, c_flat.astype(BF16))
        qc = jnp.concatenate([qc_all[b * T:(b + 1) * T, b * DV_M:(b + 1) * DV_M] for b in range(NB)], axis=0)
        n_rows = jnp.broadcast_to(n0_ref[:, h:h + 1, :], (NB, T, DK_M)).reshape(L, DK_M)
        num = sv + iw * qc
        den = ssum + iw * jnp.sum(q * n_rows, axis=1, keepdims=True)
        h_ref[:, h * DV_M:(h + 1) * DV_M] = _head_out(num, den, m_t, zo_ref[:, h * DV_M:(h + 1) * DV_M], hw_ref)
        ws_h = ws[:, h:h + 1]
        vw = v * ws_h
        vw_exp = jnp.where(blockdiag, jnp.concatenate([vw] * NB, axis=1), 0.0).astype(BF16)
        upd = _dot_tn(vw_exp, kb).reshape(NB, DV_M, DK_M)
        dc = decay[:, h:h + 1].reshape(NB, T, 1)[:, 0:1, :]
        c_out[:, h] = dc * C + upd
        kw = (ws_h * k).reshape(NB, T, DK_M)
        n_out[:, h:h + 1, :] = dc * n0_ref[:, h:h + 1, :] + jnp.sum(kw, axis=1, keepdims=True)


def _mlstm_sample(z, gc, gr, hist, c0, n0, m0p, conv_w, conv_b, bias_c, bias_r, hnorm_w, S, B, T):
    NB = SEQ_BLOCK
    L = NB * T
    assert L == LANES and B % NB == 0 and S % L == 0
    r0 = S // L
    nz = lambda col, width: col // width
    return pl.pallas_call(
        functools.partial(_mlstm_sample_kernel, T=T),
        grid=(B // NB,),
        in_specs=[
            pl.BlockSpec((L, 2 * QK_M), lambda i: (r0 + i, nz(ZC_QK, 2 * QK_M))),
            pl.BlockSpec((L, W_M), lambda i: (r0 + i, nz(ZC_VM, W_M))),
            pl.BlockSpec((L, W_M), lambda i: (r0 + i, nz(ZC_OM, W_M))),
            pl.BlockSpec((L, 2 * LANES), lambda i: (r0 + i, 0)),
            pl.BlockSpec((2 * SUBLANES, L), lambda i: (0, r0 + i)),
            pl.BlockSpec((L, 2 * QK_M), lambda i: (i, 0)),
            pl.BlockSpec((NB, H_M, DV_M, DK_M), lambda i: (i, 0, 0, 0)),
            pl.BlockSpec((NB, H_M, DK_M), lambda i: (i, 0, 0)),
            pl.BlockSpec((L, LANES), lambda i: (i, 0)),
            pl.BlockSpec((CONV_W, 2 * QK_M), lambda i: (0, 0)),
            pl.BlockSpec((1, 2 * QK_M), lambda i: (0, 0)),
            pl.BlockSpec((1, 2 * LANES), lambda i: (0, 0)),
            pl.BlockSpec((2 * SUBLANES, 1), lambda i: (0, 0)),
            pl.BlockSpec((1, DV_M), lambda i: (0, 0)),
        ],
        out_specs=[
            pl.BlockSpec((L, W_M), lambda i: (i, 0)),
            pl.BlockSpec((NB, H_M, DV_M, DK_M), lambda i: (i, 0, 0, 0)),
            pl.BlockSpec((NB, H_M, DK_M), lambda i: (i, 0, 0)),
            pl.BlockSpec((L, LANES), lambda i: (i, 0)),
        ],
        out_shape=[
            jax.ShapeDtypeStruct((B * T, W_M), BF16),
            jax.ShapeDtypeStruct((B, H_M, DV_M, DK_M), F32),
            jax.ShapeDtypeStruct((B, H_M, DK_M), F32),
            jax.ShapeDtypeStruct((B * T, LANES), F32),
        ],
        compiler_params=_params("parallel"),
        name="mlstm_sample",
    )(z, z, z, gc, gr, hist, c0, n0, m0p, conv_w, conv_b, bias_c, bias_r, hnorm_w)


def _lambda(lq1, lk1, lq2, lk2):
    a = jnp.sum(lq1[...] * lk1[...], axis=-1, keepdims=True)
    b = jnp.sum(lq2[...] * lk2[...], axis=-1, keepdims=True)
    return jnp.exp(a) - jnp.exp(b) + LAM_INIT


def _subln(att, w_ref):
    y = att * lax.rsqrt(jnp.mean(att * att, axis=-1, keepdims=True) + EPS) * w_ref[...]
    return (y * (1.0 - LAM_INIT)).astype(BF16)


def _attn_prompt_body(qi, q_ref, k_ref, v_ref, b0_ref, b1_ref, lq1, lk1, lq2, lk2, sw_ref,
                      o_ref, kb_s, vt_s, m_s, acc_s):
    T = q_ref.shape[0]
    n_tiles = kb_s.shape[0]

    @pl.when(qi == 0)
    def _():
        for t in range(n_tiles):
            kb_s[t] = k_ref[t * T:(t + 1) * T, :].astype(BF16)
            vt_s[t, 0:DV_D, :] = v_ref[t * T:(t + 1) * T, :].T.astype(BF16)
            vt_s[t, DV_D:, :] = jnp.ones((ONES_ROWS, T), BF16)

    q = q_ref[...] * (DK_D ** -0.5 * LOG2E)
    lane = lax.broadcasted_iota(jnp.int32, q.shape, 1)
    qpad = (jnp.where(lane < DK_D, q, 0.0).astype(BF16), jnp.where(lane >= DK_D, q, 0.0).astype(BF16))

    def group(tiles, state):
        scores = []
        for kj, bias in tiles:
            kt = kb_s[kj]
            for c in range(2):
                s = _dot_nt(kt, qpad[c])
                scores.append(s if bias is None else s + bias)
        parts = ([], [])
        for t, (kj, _) in enumerate(tiles):
            vt = vt_s[kj]
            for c in range(2):
                s = scores[2 * t + c]
                m = jnp.max(s, axis=0, keepdims=True)
                parts[c].append((m, _dot(vt, jnp.exp2(s - m).astype(BF16))))
        out = []
        for c in range(2):
            m_old, acc_old = state[c]
            m_new = m_old
            for m, _ in parts[c]:
                m_new = jnp.maximum(m_new, m)
            acc_new = jnp.exp2(m_old - m_new) * acc_old
            for m, pv in parts[c]:
                acc_new = acc_new + jnp.exp2(m - m_new) * pv
            out.append((m_new, acc_new))
        return tuple(out)

    def load():
        return tuple((m_s[c], acc_s[c]) for c in range(2))

    def store(state):
        for c in range(2):
            m_s[c], acc_s[c] = state[c]

    n_all = qi + 1
    n_head = functools.reduce(lambda acc, G: jnp.where(n_all >= G, jnp.maximum(acc, G), acc), HEAD_GROUPS, 0)
    n_far = n_all - n_head
    state = tuple((jnp.full((1, T), -jnp.inf, F32), jnp.zeros((DV_D + ONES_ROWS, T), F32)) for _ in range(2))
    done = 0
    for G in FAR_GROUPS:
        n_grp = (n_far - done) // G
        state = lax.fori_loop(0, n_grp, lambda g, st, G=G, done=done: group(
            [(done + g * G + t, None) for t in range(G)], st), state)
        done = done + n_grp * G
    store(state)

    for G in HEAD_GROUPS:
        @pl.when(n_head == G)
        def _(G=G):
            bias = [None] * (G - 2) + [b1_ref[0], b0_ref[0]]
            store(group([(qi - (G - 1) + t, bias[-G:][t]) for t in range(G)], load()))

    (_, a0), (_, a1) = load()
    lam = _lambda(lq1, lk1, lq2, lk2)
    att_t = a0[0:DV_D] / a0[DV_D:DV_D + 1] - lam * (a1[0:DV_D] / a1[DV_D:DV_D + 1])
    o_ref[...] = _subln(att_t.T, sw_ref)


def _attn_prompt_kernel(*refs):
    _attn_prompt_body(pl.program_id(1), *refs)


def _attn_prompt_specs(S, T, extra=()):
    hw = 2 * DK_D
    in_specs = [
        pl.BlockSpec((T, hw), lambda h, i, *_: (i, ZC_QD // hw + h)),
        pl.BlockSpec((S, hw), lambda h, i, *_: (0, h)),
        pl.BlockSpec((S, DV_D), lambda h, i, *_: (0, h)),
        pl.BlockSpec((1, T, T), lambda h, i, *_: (h, 0, 0)),
        pl.BlockSpec((1, T, T), lambda h, i, *_: (h, 0, 0)),
    ]
    out_spec = pl.BlockSpec((T, DV_D), lambda h, i, *_: (i, h))
    scratch = [
        pltpu.VMEM((S // T, T, hw), BF16),
        pltpu.VMEM((S // T, DV_D + ONES_ROWS, T), BF16),
        pltpu.VMEM((2, 1, T), F32),
        pltpu.VMEM((2, DV_D + ONES_ROWS, T), F32),
    ]
    return in_specs, out_spec, scratch


def _small_specs(shapes):
    return [pl.BlockSpec(shape, lambda *_, n=len(shape): (0,) * n) for shape in shapes]


_LAMBDA_AND_SUBLN = [(1, DK_D)] * 4 + [(1, DV_D)]


def _attn_prompt(z, kp, vp, bias0, bias1, lq1, lk1, lq2, lk2, subln_w, S):
    T = bias0.shape[-1]
    in_specs, out_spec, scratch = _attn_prompt_specs(S, T)
    return pl.pallas_call(
        _attn_prompt_kernel,
        grid=(H_D, S // T),
        in_specs=in_specs + _small_specs(_LAMBDA_AND_SUBLN),
        out_specs=out_spec,
        out_shape=jax.ShapeDtypeStruct((S, W_D), BF16),
        scratch_shapes=scratch,
        compiler_params=_params("arbitrary", "arbitrary"),
        name="attn_prompt",
    )(z, kp, vp, bias0, bias1, lq1, lk1, lq2, lk2, subln_w)


def _attn_sample_body(p, n_steps, q_ref, *refs, pps):
    kc = refs[0:pps]
    vc = refs[pps:2 * pps]
    (kn_ref, vn_ref, bfar_ref, blast_ref, bnew_ref, lq1, lk1, lq2, lk2, sw_ref,
     o_ref, qbd_s, m_s, acc_s) = refs[2 * pps:]
    last_step = p == n_steps - 1
    HT = q_ref.shape[1]

    @pl.when(p == 0)
    def _():
        q = q_ref[0] * (DK_D ** -0.5 * LOG2E)
        lane = lax.broadcasted_iota(jnp.int32, q.shape, 1)
        qbd_s[0:HT, :] = jnp.where(lane < DK_D, q, 0.0).astype(BF16)
        qbd_s[HT:2 * HT, :] = jnp.where(lane >= DK_D, q, 0.0).astype(BF16)
        m_s[...] = jnp.full_like(m_s, -jnp.inf)
        acc_s[...] = jnp.zeros_like(acc_s)

    def update(tiles):
        qbd = qbd_s[...]
        scores = [_dot_nt(qbd, k.astype(BF16)) + b for k, _, b in tiles]
        parts = []
        for s, (_, v, _) in zip(scores, tiles):
            m = jnp.max(s, axis=1, keepdims=True)
            pr = jnp.exp2(s - m).astype(BF16)
            v_ext = jnp.concatenate([v.astype(BF16), jnp.ones(v.shape, BF16)], axis=1)
            parts.append((m, _dot(pr, v_ext)))
        m_old = m_s[...]
        m_new = m_old
        for m, _ in parts:
            m_new = jnp.maximum(m_new, m)
        acc = jnp.exp2(m_old - m_new) * acc_s[...]
        for m, pv in parts:
            acc = acc + jnp.exp2(m - m_new) * pv
        acc_s[...] = acc
        m_s[...] = m_new

    rows = kc[0].shape[0] * kc[0].shape[1]
    bfar = bfar_ref[...]
    b_end = jnp.where(last_step, blast_ref[...], bfar)
    update([(kc[i][...].reshape(rows, 2 * DK_D), vc[i][...].reshape(rows, DV_D), bfar if i < pps - 1 else b_end)
            for i in range(pps)])

    @pl.when(last_step)
    def _():
        update([(kn_ref[0], vn_ref[0], bnew_ref[...])])
        lam = _lambda(lq1, lk1, lq2, lk2)
        r = acc_s[:, 0:DV_D] / acc_s[:, DV_D:DV_D + 1]
        o_ref[0] = _subln(r[0:HT] - lam * r[HT:2 * HT], sw_ref)


def _attn_sample_kernel(pt_ref, *refs, pps):
    _attn_sample_body(pl.program_id(1), pl.num_programs(1), *refs, pps=pps)


def _attn_sample_specs(qs, bfar, blast, bnew, PG, pps, seq_step):
    B, HT, _ = qs.shape

    def per_seq(width):
        return pl.BlockSpec((1, HT, width), lambda *g: (seq_step(*g[:-1])[0], 0, 0))

    def page_spec(i, width):
        def index(*g):
            b, p = seq_step(*g[:-1])
            return (0, g[-1][b, p * pps + i], 0, 0, 0)
        return pl.BlockSpec((None, None, PG, H_D, width), index)

    in_specs = ([per_seq(2 * DK_D)]
                + [page_spec(i, 2 * DK_D) for i in range(pps)]
                + [page_spec(i, DV_D) for i in range(pps)]
                + [per_seq(2 * DK_D), per_seq(DV_D)]
                + _small_specs([bfar.shape, blast.shape, bnew.shape]))
    scratch = [
        pltpu.VMEM((2 * HT, 2 * DK_D), BF16),
        pltpu.VMEM((2 * HT, 1), F32),
        pltpu.VMEM((2 * HT, 2 * DV_D), F32),
    ]
    return in_specs, per_seq(DV_D), scratch


def _attn_sample(page_table, qs, cache_k, cache_v, kn, vn, bfar, blast, bnew, lq1, lk1, lq2, lk2, subln_w, pps):
    B, HT, _ = qs.shape
    n_pages = page_table.shape[1]
    in_specs, out_spec, scratch = _attn_sample_specs(qs, bfar, blast, bnew, cache_k.shape[2], pps,
                                                     lambda b, p: (b, p))
    grid_spec = pltpu.PrefetchScalarGridSpec(
        num_scalar_prefetch=1,
        grid=(B, n_pages // pps),
        in_specs=in_specs + _small_specs(_LAMBDA_AND_SUBLN),
        out_specs=out_spec,
        scratch_shapes=scratch,
    )
    return pl.pallas_call(
        functools.partial(_attn_sample_kernel, pps=pps),
        grid_spec=grid_spec,
        out_shape=jax.ShapeDtypeStruct((B, HT, DV_D), BF16),
        compiler_params=_params("arbitrary", "arbitrary"),
        name="attn_sample",
    )(page_table, qs, *([cache_k] * pps), *([cache_v] * pps), kn, vn, bfar, blast, bnew,
      lq1, lk1, lq2, lk2, subln_w)


def _attn_fused_kernel(pt_ref, *refs, pps, n_sp, n_sample_in):
    n_prompt_in = 5
    p_in = refs[0:n_prompt_in]
    s_in = refs[n_prompt_in:n_prompt_in + n_sample_in]
    shared = refs[n_prompt_in + n_sample_in:n_prompt_in + n_sample_in + 5]
    o_p, o_s = refs[n_prompt_in + n_sample_in + 5:n_prompt_in + n_sample_in + 7]
    scratch = refs[n_prompt_in + n_sample_in + 7:]
    qi = pl.program_id(1)
    sid = pl.program_id(0) * pl.num_programs(1) + qi
    _attn_prompt_body(qi, *p_in, *shared, o_p, *scratch[0:4])
    _attn_sample_body(sid % n_sp, n_sp, *s_in, *shared, o_s, *scratch[4:], pps=pps)


def _attn_fused(page_table, z, kp, vp, bias0, bias1, qs, cache_k, cache_v, kn, vn, bfar, blast, bnew,
                lq1, lk1, lq2, lk2, subln_w, S, pps):
    T = bias0.shape[-1]
    B, HT, _ = qs.shape
    nq = S // T
    n_sp = page_table.shape[1] // pps
    assert H_D * nq == B * n_sp
    p_specs, p_out, p_scratch = _attn_prompt_specs(S, T)
    seq_step = lambda h, i: ((h * nq + i) // n_sp, (h * nq + i) % n_sp)
    s_specs, s_out, s_scratch = _attn_sample_specs(qs, bfar, blast, bnew, cache_k.shape[2], pps, seq_step)
    grid_spec = pltpu.PrefetchScalarGridSpec(
        num_scalar_prefetch=1,
        grid=(H_D, nq),
        in_specs=p_specs + s_specs + _small_specs(_LAMBDA_AND_SUBLN),
        out_specs=[p_out, s_out],
        scratch_shapes=p_scratch + s_scratch,
    )
    return pl.pallas_call(
        functools.partial(_attn_fused_kernel, pps=pps, n_sp=n_sp, n_sample_in=len(s_specs)),
        grid_spec=grid_spec,
        out_shape=[jax.ShapeDtypeStruct((S, W_D), BF16), jax.ShapeDtypeStruct((B, HT, DV_D), BF16)],
        compiler_params=_params("arbitrary", "arbitrary"),
        name="attn_fused",
    )(page_table, z, kp, vp, bias0, bias1, qs, *([cache_k] * pps), *([cache_v] * pps), kn, vn, bfar, blast, bnew,
      lq1, lk1, lq2, lk2, subln_w)


def _merge_kernel(hmp_ref, hms_ref, atp_ref, ats_ref, wa_ref, wb_ref, ga_ref, gb_ref, u_ref, *, n_prompt):
    def body(hm_ref, at_ref):
        ya = _dot(hm_ref[...], wa_ref[...])
        yb = _dot(at_ref[...], wb_ref[...])
        u_ref[...] = (_sigmoid(ga_ref[...]) * ya + _sigmoid(gb_ref[...]) * yb).astype(BF16)

    @pl.when(pl.program_id(0) < n_prompt)
    def _():
        body(hmp_ref, atp_ref)

    @pl.when(pl.program_id(0) >= n_prompt)
    def _():
        body(hms_ref, ats_ref)


def _merge(hm_p, hm_s, att_p, att_s, w_a, w_b, z, D):
    S, BT = hm_p.shape[0], hm_s.shape[0]
    tm, n_prompt, n_sample, prow, srow = _two_way_rows(S, BT, (1024, 512, 256, 128))
    tn = _pick(D, (512, 256, 128))
    ga0, gb0 = ZC_GA // tn, (ZC_GA + D) // tn
    return pl.pallas_call(
        functools.partial(_merge_kernel, n_prompt=n_prompt),
        grid=(n_prompt + n_sample, D // tn),
        in_specs=[
            pl.BlockSpec((tm, W_M), lambda i, j: (prow(i), 0)),
            pl.BlockSpec((tm, W_M), lambda i, j: (srow(i), 0)),
            pl.BlockSpec((tm, W_D), lambda i, j: (prow(i), 0)),
            pl.BlockSpec((tm, W_D), lambda i, j: (srow(i), 0)),
            pl.BlockSpec((W_M, tn), lambda i, j: (0, j)),
            pl.BlockSpec((W_D, tn), lambda i, j: (0, j)),
            pl.BlockSpec((tm, tn), lambda i, j: (i, ga0 + j)),
            pl.BlockSpec((tm, tn), lambda i, j: (i, gb0 + j)),
        ],
        out_specs=pl.BlockSpec((tm, tn), lambda i, j: (i, j)),
        out_shape=jax.ShapeDtypeStruct((S + BT, D), BF16),
        compiler_params=_params("parallel", "parallel"),
        name="merge",
    )(hm_p, hm_s, att_p, att_s, w_a, w_b, z, z)


def _out_proj_kernel(u_ref, w_ref, xp_ref, xs_ref, o_ref, *, n_prompt):
    y = _dot(u_ref[...], w_ref[...])

    @pl.when(pl.program_id(0) < n_prompt)
    def _():
        o_ref[...] = xp_ref[...] + y

    @pl.when(pl.program_id(0) >= n_prompt)
    def _():
        o_ref[...] = xs_ref[...] + y


def _out_proj(u, w_out, xp, xs):
    S, D = xp.shape
    BT = xs.shape[0]
    tm, n_prompt, n_sample, prow, srow = _two_way_rows(S, BT, (1024, 512, 256, 128))
    tn = _pick(D, (512, 256, 128))
    return pl.pallas_call(
        functools.partial(_out_proj_kernel, n_prompt=n_prompt),
        grid=(n_prompt + n_sample, D // tn),
        in_specs=[
            pl.BlockSpec((tm, D), lambda i, j: (i, 0)),
            pl.BlockSpec((D, tn), lambda i, j: (0, j)),
            pl.BlockSpec((tm, tn), lambda i, j: (prow(i), jnp.where(i < n_prompt, j, D // tn - 1))),
            pl.BlockSpec((tm, tn), lambda i, j: (srow(i), jnp.where(i < n_prompt, 0, j))),
        ],
        out_specs=pl.BlockSpec((tm, tn), lambda i, j: (i, j)),
        out_shape=jax.ShapeDtypeStruct((S + BT, D), F32),
        compiler_params=_params("parallel", "parallel"),
        name="out_proj",
    )(u, w_out, xp, xs)


def _ffn_kernel(x_ref, nw_ref, w1_ref, w2_ref, fw_ref, yp_ref, ys_ref, xn_s, acc_s, *, n_prompt):
    f = pl.program_id(1)

    @pl.when(f == 0)
    def _():
        x = x_ref[...]
        ms = jnp.mean(x * x, axis=-1, keepdims=True)
        xn_s[...] = (x * lax.rsqrt(ms + EPS) * nw_ref[...]).astype(BF16)
        acc_s[...] = jnp.zeros_like(acc_s)

    hid = jnp.maximum(_dot(xn_s[...], w1_ref[...]), 0.0)
    acc_s[...] += _dot((hid * hid).astype(BF16), w2_ref[...])

    def final(y_ref):
        x2 = x_ref[...] + acc_s[...]
        ms = jnp.mean(x2 * x2, axis=-1, keepdims=True)
        y_ref[...] = x2 * lax.rsqrt(ms + EPS) * fw_ref[...]

    last = f == pl.num_programs(1) - 1

    @pl.when(last & (pl.program_id(0) < n_prompt))
    def _():
        final(yp_ref)

    @pl.when(last & (pl.program_id(0) >= n_prompt))
    def _():
        final(ys_ref)


def _ffn(x, norm_w, w1, w2, final_w, S):
    R, D = x.shape
    DF = w1.shape[1]
    tm, n_prompt, n_sample, prow, srow = _two_way_rows(S, R - S, (512, 256, 128))
    tf = _pick(DF, (1024, 512, 256, 128))
    return pl.pallas_call(
        functools.partial(_ffn_kernel, n_prompt=n_prompt),
        grid=(n_prompt + n_sample, DF // tf),
        in_specs=[
            pl.BlockSpec((tm, D), lambda i, f: (i, 0)),
            pl.BlockSpec((1, D), lambda i, f: (0, 0)),
            pl.BlockSpec((D, tf), lambda i, f: (0, f)),
            pl.BlockSpec((tf, D), lambda i, f: (f, 0)),
            pl.BlockSpec((1, D), lambda i, f: (0, 0)),
        ],
        out_specs=[
            pl.BlockSpec((tm, D), lambda i, f: (prow(i), 0)),
            pl.BlockSpec((tm, D), lambda i, f: (srow(i), 0)),
        ],
        out_shape=[jax.ShapeDtypeStruct((S, D), F32), jax.ShapeDtypeStruct((R - S, D), F32)],
        scratch_shapes=[pltpu.VMEM((tm, D), BF16), pltpu.VMEM((tm, D), F32)],
        compiler_params=_params("arbitrary", "arbitrary"),
        name="ffn",
    )(x, norm_w, w1, w2, final_w)


def _bias_by_distance(rel_bias, n):
    d = jnp.arange(n, dtype=jnp.int32)
    max_exact = N_BUCKETS // 2
    nf = jnp.maximum(d, 1).astype(F32)
    large = max_exact + (jnp.log(nf / max_exact) / math.log(MAX_DIST / max_exact)
                         * (N_BUCKETS - max_exact)).astype(jnp.int32)
    large = jnp.minimum(large, N_BUCKETS - 1)
    bucket = jnp.where(d < max_exact, d, large)
    onehot = (bucket[:, None] == jnp.arange(N_BUCKETS)[None, :]).astype(F32)
    return jnp.dot(onehot, rel_bias.astype(F32), precision=HIGHEST).T


def _toeplitz(w, rows, cols):
    n = w.shape[-1]
    assert cols <= n - 1
    lead = w.shape[:-1]
    flat = jnp.tile(w, (1,) * len(lead) + (rows,))[..., :rows * (n - 1)]
    return flat.reshape(lead + (rows, n - 1))[..., :cols]


def _prompt_bias_tiles(rel_bias, T):
    assert T + 1 >= MAX_DIST
    bd = _bias_by_distance(rel_bias, 2 * T)
    val = (bd - bd[:, 2 * T - 1:]) * LOG2E
    neg = jnp.full((H_D, T), NEG, F32)
    t0 = _toeplitz(jnp.concatenate([val[:, :T], neg], axis=1), T, T)
    t1 = _toeplitz(jnp.concatenate([val[:, T:], val[:, :T]], axis=1), T, T)
    return t0, t1


def _sample_bias_tables(rel_bias, T, PG):
    assert PG + 1 >= MAX_DIST
    bd = _bias_by_distance(rel_bias, 2 * PG + T) * LOG2E
    HT = T * H_D
    eye = jnp.asarray(np.eye(H_D, dtype=bool))

    def expand(tab):
        K = tab.shape[-1]
        full = jnp.where(eye[None, :, None, :], jnp.transpose(tab, (1, 0, 2))[:, :, :, None], NEG)
        full = full.reshape(HT, K * H_D)
        return jnp.concatenate([full, full], axis=0).astype(F32)

    far = expand(jnp.broadcast_to(bd[:, 2 * PG + T - 1][:, None, None], (H_D, T, PG)))
    w_last = jnp.concatenate([bd[:, PG:0:-1], bd[:, :1], bd[:, PG + T - 1:PG:-1]], axis=1)
    last = expand(_toeplitz(w_last, T, PG))
    w_new = jnp.concatenate([bd[:, :1], jnp.full((H_D, T), NEG, F32), bd[:, T - 1:0:-1]], axis=1)
    new = expand(_toeplitz(w_new, T, T))
    return far, last, new


def kernel(x_prompt, x_sample, cache_k, cache_v, page_table, state_C, state_n, state_m, state_conv,
           norm1_w, w_in, b_i, b_f, conv_w, conv_b, hnorm_w, lambda_q1, lambda_k1, lambda_q2, lambda_k2,
           subln_w, rel_bias, w_a, w_b, w_out, norm2_w, w_ff1, w_ff2, final_norm_w):
    assert w_in.shape[0] == 1 and x_prompt.shape[0] == 1
    _, S, D = x_prompt.shape
    B, T, _ = x_sample.shape
    PG = cache_k.shape[2]
    xp = x_prompt[0]
    xs = x_sample.reshape(B * T, D)

    o_i = 2 * QK_M + 2 * W_M
    o_qd = o_i + 2 * H_M
    w_t = jnp.transpose(w_in[0])
    w_all = w_t.astype(BF16)
    w_rest = w_all[o_qd:]
    w_gate_row = w_all[o_i:o_qd]
    gate_pad = jnp.zeros((LANES - H_M, D), BF16)
    w_gate_col = jnp.concatenate([w_gate_row[:H_M], gate_pad, w_gate_row[H_M:], gate_pad], axis=0)
    bias_c = jnp.zeros((1, 2 * LANES), F32).at[0, 0:H_M].set(b_i[0]).at[0, LANES:LANES + H_M].set(b_f[0])
    bias_r = jnp.concatenate([b_i[0], b_f[0]])[:, None]

    z, kp, ks, vp, vs, gc, gr = _in_proj(xp, xs, norm1_w, w_all, w_rest, w_gate_col, w_gate_row)

    hm_p, c_p, n_p, m_p, conv_p = _mlstm_prompt(z, gc, gr, conv_w[0], conv_b, bias_c, bias_r, hnorm_w, S)
    hist = jnp.pad(state_conv[0], ((0, 0), (T - (CONV_W - 1), 0), (0, 0))).reshape(B * T, 2 * QK_M)
    m0p = jnp.repeat(jnp.pad(state_m[0], ((0, 0), (0, LANES - H_M))), T, axis=0)
    hm_s, c_s, n_s, m_s = _mlstm_sample(z, gc, gr, hist, state_C[0], state_n[0], m0p, conv_w[0], conv_b,
                                        bias_c, bias_r, hnorm_w, S, B, T)

    TQ = _pick(S, (256, 128))
    t0, t1 = _prompt_bias_tiles(rel_bias, TQ)
    lq1, lk1, lq2, lk2 = lambda_q1, lambda_k1, lambda_q2, lambda_k2
    bfar, blast, bnew = _sample_bias_tables(rel_bias, T, PG)
    qs = z[S:, ZC_QD:ZC_QD + QK_D].reshape(B, T * H_D, 2 * DK_D)
    kn = ks.reshape(B, T * H_D, 2 * DK_D)
    vn = vs.reshape(B, T * H_D, DV_D)
    n_pages = page_table.shape[1]
    pps = _pick(n_pages, (PAGES_PER_STEP, 4, 2, 1))
    if H_D * (S // TQ) == B * (n_pages // pps):
        att_p, att_s = _attn_fused(page_table, z, kp, vp, t0, t1, qs, cache_k, cache_v, kn, vn, bfar, blast, bnew,
                                   lq1, lk1, lq2, lk2, subln_w, S, pps)
    else:
        att_p = _attn_prompt(z, kp, vp, t0, t1, lq1, lk1, lq2, lk2, subln_w, S)
        att_s = _attn_sample(page_table, qs, cache_k, cache_v, kn, vn, bfar, blast, bnew,
                             lq1, lk1, lq2, lk2, subln_w, pps)

    u = _merge(hm_p, hm_s, att_p, att_s.reshape(B * T, W_D), w_a[0].astype(BF16), w_b[0].astype(BF16), z, D)
    x1 = _out_proj(u, w_out[0].astype(BF16), xp, xs)
    y_p, y_s = _ffn(x1, norm2_w, w_ff1[0].astype(BF16), w_ff2[0].astype(BF16), final_norm_w[None, :], S)

    conv_prompt = conv_p[SUBLANES - (CONV_W - 1):].reshape(1, 1, CONV_W - 1, 2 * QK_M)
    conv_sample = z[S:, :2 * QK_M].reshape(B, T, 2 * QK_M)[:, T - (CONV_W - 1):][None]
    return (y_p.reshape(1, S, D), y_s.reshape(B, T, D),
            kp.reshape(1, 1, S, H_D, 2 * DK_D), vp.reshape(1, 1, S, H_D, DV_D),
            c_p[None, None], n_p[None, None], m_p[:, :H_M][None], conv_prompt,
            ks.reshape(1, B, T, H_D, 2 * DK_D), vs.reshape(1, B, T, H_D, DV_D),
            c_s[None], n_s[None], m_s[::T, :H_M][None], conv_sample)
```

```python
import functools
import math

import numpy as np
import jax
import jax.numpy as jnp
from jax import lax
from jax.experimental import pallas as pl
from jax.experimental.pallas import tpu as pltpu

F32 = jnp.float32
BF16 = jnp.bfloat16
HIGHEST = lax.Precision.HIGHEST

H_M = 8
DK_M = 128
DV_M = 128
QK_M = H_M * DK_M
W_M = H_M * DV_M
CONV_W = 4
H_D = 8
DK_D = 64
DV_D = 128
QK_D = H_D * 2 * DK_D
W_D = H_D * DV_D
N_BUCKETS = 32
MAX_DIST = 128
EPS = 1e-6
LAM_INIT = 0.8 - 0.6 * math.exp(-0.3 * 0)
NEG = -1e30
LOG2E = math.log2(math.e)

ZC_QK = 0
ZC_VM = 2 * QK_M
ZC_OM = ZC_VM + W_M
ZC_QD = ZC_OM + W_M
ZC_GA = ZC_QD + QK_D

LANES = 128
SUBLANES = 8
VMEM_LIMIT = 56 * 1024 * 1024

SEQ_BLOCK = 16
PAGES_PER_STEP = 8
FAR_GROUPS = (8, 4, 2, 1)
HEAD_GROUPS = (8, 4, 2, 1)
ONES_ROWS = 16


def _params(*sem):
    return pltpu.CompilerParams(dimension_semantics=sem, vmem_limit_bytes=VMEM_LIMIT)


def _pick(n, prefs):
    for p in prefs:
        if n % p == 0:
            return p
    return n


def _sigmoid(x):
    return 1.0 / (1.0 + jnp.exp(-x))


def _log_sigmoid(x):
    return jnp.minimum(x, 0.0) - jnp.log(1.0 + jnp.exp(-jnp.abs(x)))


def _dot(a, b):
    return jnp.dot(a, b, preferred_element_type=F32)


def _dot_nt(a, b):
    return lax.dot_general(a, b, (((1,), (1,)), ((), ())), preferred_element_type=F32)


def _dot_tn(a, b):
    return lax.dot_general(a, b, (((0,), (0,)), ((), ())), preferred_element_type=F32)


def _dot_exact(a, b):
    return jnp.dot(a, b, preferred_element_type=F32, precision=HIGHEST)


def _two_way_rows(S, BT, prefs):
    tm = _pick(math.gcd(S, BT), prefs)
    n_prompt = S // tm
    prow = lambda i: jnp.minimum(i, n_prompt - 1)
    srow = lambda i: jnp.maximum(i - n_prompt, 0)
    return tm, n_prompt, BT // tm, prow, srow


def _in_proj_kernel(xp_ref, xs_ref, nw_ref, wa_ref, wb_ref, wgc_ref, wgr_ref,
                    z_ref, kp_ref, ks_ref, vp_ref, vs_ref, gc_ref, gr_ref, xn_ref, *, n_prompt, n_a, n_z1, nkt):
    i = pl.program_id(0)
    j = pl.program_id(1)
    is_p = i < n_prompt
    is_s = jnp.logical_not(is_p)

    def norm(x_ref):
        x = x_ref[...]
        ms = jnp.mean(x * x, axis=-1, keepdims=True)
        xn = (x * lax.rsqrt(ms + EPS) * nw_ref[...]).astype(BF16)
        xn_ref[...] = xn
        gc_ref[...] = _dot_nt(xn, wgc_ref[...])
        gr_ref[...] = _dot_nt(wgr_ref[...], xn)

    @pl.when((j == 0) & is_p)
    def _():
        norm(xp_ref)

    @pl.when((j == 0) & is_s)
    def _():
        norm(xs_ref)

    in_k = (j >= n_z1) & (j < n_z1 + nkt)
    in_v = (j >= n_z1 + nkt) & (j < n_z1 + 2 * nkt)
    in_zb = (j >= n_a) & jnp.logical_not(in_k | in_v)
    for cond, w_ref, o_ref in ((j < n_a, wa_ref, z_ref), (in_zb, wb_ref, z_ref),
                               (in_k & is_p, wb_ref, kp_ref), (in_k & is_s, wb_ref, ks_ref),
                               (in_v & is_p, wb_ref, vp_ref), (in_v & is_s, wb_ref, vs_ref)):
        @pl.when(cond)
        def _(w_ref=w_ref, o_ref=o_ref):
            o_ref[...] = _dot_nt(xn_ref[...], w_ref[...])


def _in_proj(xp, xs, norm_w, w_all, w_rest, w_gate_col, w_gate_row):
    S, D = xp.shape
    BT = xs.shape[0]
    tm, n_prompt, n_sample, prow, srow = _two_way_rows(S, BT, (1024, 512, 256, 128))
    tn = _pick(math.gcd(D, QK_D), (512, 256, 128))
    n_a = ZC_QD // tn
    n_z1 = ZC_GA // tn
    nkt = QK_D // tn
    n_tiles = n_a + w_rest.shape[0] // tn
    nzt = n_tiles - 2 * nkt
    zcol = lambda j: jnp.where(j < n_z1, j, jnp.where(j < n_z1 + 2 * nkt, n_z1 - 1, j - 2 * nkt))
    kcol = lambda j: jnp.clip(j - n_z1, 0, nkt - 1)
    vcol = lambda j: jnp.clip(j - n_z1 - nkt, 0, nkt - 1)
    p_spec = lambda col: pl.BlockSpec((tm, tn), lambda i, j: (prow(i), jnp.where(i < n_prompt, col(j), nkt - 1)))
    s_spec = lambda col: pl.BlockSpec((tm, tn), lambda i, j: (srow(i), jnp.where(i < n_prompt, 0, col(j))))
    return pl.pallas_call(
        functools.partial(_in_proj_kernel, n_prompt=n_prompt, n_a=n_a, n_z1=n_z1, nkt=nkt),
        grid=(n_prompt + n_sample, n_tiles),
        in_specs=[
            pl.BlockSpec((tm, D), lambda i, j: (prow(i), 0), pipeline_mode=pl.Buffered(1)),
            pl.BlockSpec((tm, D), lambda i, j: (srow(i), 0), pipeline_mode=pl.Buffered(1)),
            pl.BlockSpec((1, D), lambda i, j: (0, 0)),
            pl.BlockSpec((tn, D), lambda i, j: (jnp.minimum(j, n_a - 1), 0)),
            pl.BlockSpec((tn, D), lambda i, j: (jnp.maximum(j - n_a, 0), 0)),
            pl.BlockSpec((2 * LANES, D), lambda i, j: (0, 0)),
            pl.BlockSpec((2 * SUBLANES, D), lambda i, j: (0, 0)),
        ],
        out_specs=[
            pl.BlockSpec((tm, tn), lambda i, j: (i, zcol(j))),
            p_spec(kcol), s_spec(kcol), p_spec(vcol), s_spec(vcol),
            pl.BlockSpec((tm, 2 * LANES), lambda i, j: (i, 0)),
            pl.BlockSpec((2 * SUBLANES, tm), lambda i, j: (0, i)),
        ],
        out_shape=[
            jax.ShapeDtypeStruct((S + BT, nzt * tn), F32),
            jax.ShapeDtypeStruct((S, QK_D), F32),
            jax.ShapeDtypeStruct((BT, QK_D), F32),
            jax.ShapeDtypeStruct((S, W_D), F32),
            jax.ShapeDtypeStruct((BT, W_D), F32),
            jax.ShapeDtypeStruct((S + BT, 2 * LANES), F32),
            jax.ShapeDtypeStruct((2 * SUBLANES, S + BT), F32),
        ],
        scratch_shapes=[pltpu.VMEM((tm, D), BF16)],
        compiler_params=_params("arbitrary", "arbitrary"),
        name="in_proj",
    )(xp, xs, norm_w, w_all, w_rest, w_gate_col, w_gate_row)


def _conv_silu(x, hist, hist_shift, cw_ref, cb_ref, row_in_seq):
    acc = cb_ref[...] + cw_ref[CONV_W - 1:CONV_W, :] * x
    for j in range(1, CONV_W):
        xr = pltpu.roll(x, j, axis=0)
        hr = pltpu.roll(hist, (j + hist_shift) % hist.shape[0], axis=0)
        if hist.shape[0] != x.shape[0]:
            first = jnp.where(row_in_seq[0:SUBLANES] < j, hr, xr[0:SUBLANES])
            xs = jnp.concatenate([first, xr[SUBLANES:]], axis=0)
        else:
            xs = jnp.where(row_in_seq < j, hr, xr)
        acc = acc + cw_ref[CONV_W - 1 - j:CONV_W - j, :] * xs
    return acc * _sigmoid(acc)


def _mlstm_intra(qb, kb, vb, mask, bt_c, bt_r, ig_r, inter_c):
    dlog = jnp.where(mask, bt_c - bt_r + ig_r, -jnp.inf)
    m_t = jnp.maximum(inter_c, jnp.max(dlog, axis=1, keepdims=True))
    dw = jnp.exp(dlog - m_t)
    iw = jnp.exp(inter_c - m_t)
    s = _dot_nt(qb, kb) * dw
    sv = _dot(s.astype(BF16), vb)
    return sv, jnp.sum(s, axis=1, keepdims=True), m_t, iw


def _head_out(num, den, m_t, o, hw_ref):
    den = jnp.maximum(jnp.abs(den), jnp.exp(-m_t))
    h = num / den
    hn = h * lax.rsqrt(jnp.mean(h * h, axis=-1, keepdims=True) + EPS) * hw_ref[...]
    return (hn * _sigmoid(o)).astype(BF16)


def _mlstm_prompt_kernel(zq_ref, zv_ref, zo_ref, gc_ref, gr_ref, cw_ref, cb_ref, bc_ref, br_ref, hw_ref,
                         h_ref, c_out, n_out, m_out, conv_out,
                         c_s, n_s, m_s, hist_s):
    c = pl.program_id(0)
    L = zq_ref.shape[0]

    @pl.when(c == 0)
    def _():
        c_s[...] = jnp.zeros_like(c_s)
        n_s[...] = jnp.zeros_like(n_s)
        m_s[...] = jnp.zeros_like(m_s)
        hist_s[...] = jnp.zeros_like(hist_s)

    x = zq_ref[...]
    row = lax.broadcasted_iota(jnp.int32, (L, 1), 0)
    qk = _conv_silu(x, hist_s[...], 0, cw_ref, cb_ref, row)
    hist_s[...] = x[L - SUBLANES:L, :]
    conv_out[...] = x[L - SUBLANES:L, :]

    gcol = gc_ref[...] + bc_ref[...]
    grow = gr_ref[...] + br_ref[...]
    ig_c = gcol[:, 0:LANES]
    lf_c = _log_sigmoid(gcol[:, LANES:2 * LANES])
    ig_r = grow[0:SUBLANES, :]
    lf_r = _log_sigmoid(grow[SUBLANES:2 * SUBLANES, :])
    ri = lax.broadcasted_iota(jnp.int32, (L, L), 0)
    ci = lax.broadcasted_iota(jnp.int32, (L, L), 1)
    mask = ci <= ri
    bt_c = _dot_exact(mask.astype(F32), lf_c)
    bt_r = _dot_exact(lf_r, (ri <= ci).astype(F32))
    m_prev = m_s[...]
    inter = bt_c + m_prev
    b_last = bt_c[L - 1:L, :]
    wlog = b_last - bt_c + ig_c
    m_new = jnp.maximum(b_last + m_prev, jnp.max(wlog, axis=0, keepdims=True))
    ws = jnp.exp(wlog - m_new)
    decay = jnp.exp(b_last + m_prev - m_new)
    m_s[...] = m_new
    m_out[...] = m_new

    for h in range(H_M):
        q = qk[:, h * DK_M:(h + 1) * DK_M]
        k = qk[:, QK_M + h * DK_M:QK_M + (h + 1) * DK_M] * (DK_M ** -0.5)
        v = zv_ref[:, h * DV_M:(h + 1) * DV_M]
        qb, kb, vb = q.astype(BF16), k.astype(BF16), v.astype(BF16)
        sv, ssum, m_t, iw = _mlstm_intra(qb, kb, vb, mask, bt_c[:, h:h + 1], bt_r[h:h + 1, :],
                                         ig_r[h:h + 1, :], inter[:, h:h + 1])
        C = c_s[h]
        n_row = n_s[h:h + 1, :]
        num = sv + iw * _dot_nt(qb, C.astype(BF16))
        den = ssum + iw * jnp.sum(q * n_row, axis=1, keepdims=True)
        h_ref[:, h * DV_M:(h + 1) * DV_M] = _head_out(num, den, m_t, zo_ref[:, h * DV_M:(h + 1) * DV_M], hw_ref)
        ws_h = ws[:, h:h + 1]
        dc = decay[:, h:h + 1]
        c_new = dc * C + _dot_tn((v * ws_h).astype(BF16), kb)
        n_new = dc * n_row + jnp.sum(ws_h * k, axis=0, keepdims=True)
        c_s[h] = c_new
        n_s[h:h + 1, :] = n_new
        c_out[h] = c_new
        n_out[h:h + 1, :] = n_new


def _mlstm_prompt(z, gc, gr, conv_w, conv_b, bias_c, bias_r, hnorm_w, S):
    L = _pick(S, (256, 128))
    nz = lambda col, width: col // width
    return pl.pallas_call(
        _mlstm_prompt_kernel,
        grid=(S // L,),
        in_specs=[
            pl.BlockSpec((L, 2 * QK_M), lambda c: (c, nz(ZC_QK, 2 * QK_M))),
            pl.BlockSpec((L, W_M), lambda c: (c, nz(ZC_VM, W_M))),
            pl.BlockSpec((L, W_M), lambda c: (c, nz(ZC_OM, W_M))),
            pl.BlockSpec((L, 2 * LANES), lambda c: (c, 0)),
            pl.BlockSpec((2 * SUBLANES, L), lambda c: (0, c)),
            pl.BlockSpec((CONV_W, 2 * QK_M), lambda c: (0, 0)),
            pl.BlockSpec((1, 2 * QK_M), lambda c: (0, 0)),
            pl.BlockSpec((1, 2 * LANES), lambda c: (0, 0)),
            pl.BlockSpec((2 * SUBLANES, 1), lambda c: (0, 0)),
            pl.BlockSpec((1, DV_M), lambda c: (0, 0)),
        ],
        out_specs=[
            pl.BlockSpec((L, W_M), lambda c: (c, 0)),
            pl.BlockSpec((H_M, DV_M, DK_M), lambda c: (0, 0, 0)),
            pl.BlockSpec((H_M, DK_M), lambda c: (0, 0)),
            pl.BlockSpec((1, LANES), lambda c: (0, 0)),
            pl.BlockSpec((SUBLANES, 2 * QK_M), lambda c: (0, 0)),
        ],
        out_shape=[
            jax.ShapeDtypeStruct((S, W_M), BF16),
            jax.ShapeDtypeStruct((H_M, DV_M, DK_M), F32),
            jax.ShapeDtypeStruct((H_M, DK_M), F32),
            jax.ShapeDtypeStruct((1, LANES), F32),
            jax.ShapeDtypeStruct((SUBLANES, 2 * QK_M), F32),
        ],
        scratch_shapes=[
            pltpu.VMEM((H_M, DV_M, DK_M), F32),
            pltpu.VMEM((H_M, DK_M), F32),
            pltpu.VMEM((1, LANES), F32),
            pltpu.VMEM((SUBLANES, 2 * QK_M), F32),
        ],
        compiler_params=_params("arbitrary"),
        name="mlstm_prompt",
    )(z, z, z, gc, gr, conv_w, conv_b, bias_c, bias_r, hnorm_w)


def _mlstm_sample_kernel(zq_ref, zv_ref, zo_ref, gc_ref, gr_ref, hist_ref, c0_ref, n0_ref, m0_ref,
                         cw_ref, cb_ref, bc_ref, br_ref, hw_ref,
                         h_ref, c_out, n_out, m_out, *, T):
    L = zq_ref.shape[0]
    NB = L // T
    x = zq_ref[...]
    ri = lax.broadcasted_iota(jnp.int32, (L, L), 0)
    ci = lax.broadcasted_iota(jnp.int32, (L, L), 1)
    same = (ri // T) == (ci // T)
    mask = same & (ci <= ri)
    row_t = lax.broadcasted_iota(jnp.int32, (L, 1), 0) % T
    qk = _conv_silu(x, hist_ref[...], L - T, cw_ref, cb_ref, row_t)

    gcol = gc_ref[...] + bc_ref[...]
    grow = gr_ref[...] + br_ref[...]
    ig_c = gcol[:, 0:LANES]
    lf_c = _log_sigmoid(gcol[:, LANES:2 * LANES])
    ig_r = grow[0:SUBLANES, :]
    lf_r = _log_sigmoid(grow[SUBLANES:2 * SUBLANES, :])
    bt_c = _dot_exact(mask.astype(F32), lf_c)
    bt_r = _dot_exact(lf_r, (same & (ri <= ci)).astype(F32))
    m_prev = m0_ref[...]
    last = same & (ci % T == T - 1)
    b_last = _dot_exact(last.astype(F32), bt_c)
    inter = bt_c + m_prev
    wlog = b_last - bt_c + ig_c
    wmax = jnp.max(wlog.reshape(NB, T, LANES), axis=1, keepdims=True)
    wmax = jnp.broadcast_to(wmax, (NB, T, LANES)).reshape(L, LANES)
    m_new = jnp.maximum(b_last + m_prev, wmax)
    ws = jnp.exp(wlog - m_new)
    decay = jnp.exp(b_last + m_prev - m_new)
    m_out[...] = m_new

    lane_seq = lax.broadcasted_iota(jnp.int32, (L, NB * DV_M), 1) // DV_M
    row_seq = lax.broadcasted_iota(jnp.int32, (L, NB * DV_M), 0) // T
    blockdiag = lane_seq == row_seq

    for h in range(H_M):
        q = qk[:, h * DK_M:(h + 1) * DK_M]
        k = qk[:, QK_M + h * DK_M:QK_M + (h + 1) * DK_M] * (DK_M ** -0.5)
        v = zv_ref[:, h * DV_M:(h + 1) * DV_M]
        qb, kb, vb = q.astype(BF16), k.astype(BF16), v.astype(BF16)
        sv, ssum, m_t, iw = _mlstm_intra(qb, kb, vb, mask, bt_c[:, h:h + 1], bt_r[h:h + 1, :],
                                         ig_r[h:h + 1, :], inter[:, h:h + 1])
        C = c0_ref[:, h]
        c_flat = C.reshape(NB * DV_M, DK_M)
        qc_all = _dot_nt(qb, c_flat.astype(BF16))
        qc = jnp.concatenate([qc_all[b * T:(b + 1) * T, b * DV_M:(b + 1) * DV_M] for b in range(NB)], axis=0)
        n_rows = jnp.broadcast_to(n0_ref[:, h:h + 1, :], (NB, T, DK_M)).reshape(L, DK_M)
        num = sv + iw * qc
        den = ssum + iw * jnp.sum(q * n_rows, axis=1, keepdims=True)
        h_ref[:, h * DV_M:(h + 1) * DV_M] = _head_out(num, den, m_t, zo_ref[:, h * DV_M:(h + 1) * DV_M], hw_ref)
        ws_h = ws[:, h:h + 1]
        vw = v * ws_h
        vw_exp = jnp.where(blockdiag, jnp.concatenate([vw] * NB, axis=1), 0.0).astype(BF16)
        upd = _dot_tn(vw_exp, kb).reshape(NB, DV_M, DK_M)
        dc = decay[:, h:h + 1].reshape(NB, T, 1)[:, 0:1, :]
        c_out[:, h] = dc * C + upd
        kw = (ws_h * k).reshape(NB, T, DK_M)
        n_out[:, h:h + 1, :] = dc * n0_ref[:, h:h + 1, :] + jnp.sum(kw, axis=1, keepdims=True)


def _mlstm_sample(z, gc, gr, hist, c0, n0, m0p, conv_w, conv_b, bias_c, bias_r, hnorm_w, S, B, T):
    NB = SEQ_BLOCK
    L = NB * T
    assert L == LANES and B % NB == 0 and S % L == 0
    r0 = S // L
    nz = lambda col, width: col // width
    return pl.pallas_call(
        functools.partial(_mlstm_sample_kernel, T=T),
        grid=(B // NB,),
        in_specs=[
            pl.BlockSpec((L, 2 * QK_M), lambda i: (r0 + i, nz(ZC_QK, 2 * QK_M))),
            pl.BlockSpec((L, W_M), lambda i: (r0 + i, nz(ZC_VM, W_M))),
            pl.BlockSpec((L, W_M), lambda i: (r0 + i, nz(ZC_OM, W_M))),
            pl.BlockSpec((L, 2 * LANES), lambda i: (r0 + i, 0)),
            pl.BlockSpec((2 * SUBLANES, L), lambda i: (0, r0 + i)),
            pl.BlockSpec((L, 2 * QK_M), lambda i: (i, 0)),
            pl.BlockSpec((NB, H_M, DV_M, DK_M), lambda i: (i, 0, 0, 0)),
            pl.BlockSpec((NB, H_M, DK_M), lambda i: (i, 0, 0)),
            pl.BlockSpec((L, LANES), lambda i: (i, 0)),
            pl.BlockSpec((CONV_W, 2 * QK_M), lambda i: (0, 0)),
            pl.BlockSpec((1, 2 * QK_M), lambda i: (0, 0)),
            pl.BlockSpec((1, 2 * LANES), lambda i: (0, 0)),
            pl.BlockSpec((2 * SUBLANES, 1), lambda i: (0, 0)),
            pl.BlockSpec((1, DV_M), lambda i: (0, 0)),
        ],
        out_specs=[
            pl.BlockSpec((L, W_M), lambda i: (i, 0)),
            pl.BlockSpec((NB, H_M, DV_M, DK_M), lambda i: (i, 0, 0, 0)),
            pl.BlockSpec((NB, H_M, DK_M), lambda i: (i, 0, 0)),
            pl.BlockSpec((L, LANES), lambda i: (i, 0)),
        ],
        out_shape=[
            jax.ShapeDtypeStruct((B * T, W_M), BF16),
            jax.ShapeDtypeStruct((B, H_M, DV_M, DK_M), F32),
            jax.ShapeDtypeStruct((B, H_M, DK_M), F32),
            jax.ShapeDtypeStruct((B * T, LANES), F32),
        ],
        compiler_params=_params("parallel"),
        name="mlstm_sample",
    )(z, z, z, gc, gr, hist, c0, n0, m0p, conv_w, conv_b, bias_c, bias_r, hnorm_w)


def _lambda(lq1, lk1, lq2, lk2):
    a = jnp.sum(lq1[...] * lk1[...], axis=-1, keepdims=True)
    b = jnp.sum(lq2[...] * lk2[...], axis=-1, keepdims=True)
    return jnp.exp(a) - jnp.exp(b) + LAM_INIT


def _subln(att, w_ref):
    y = att * lax.rsqrt(jnp.mean(att * att, axis=-1, keepdims=True) + EPS) * w_ref[...]
    return (y * (1.0 - LAM_INIT)).astype(BF16)


def _attn_prompt_body(qi, q_ref, k_ref, v_ref, b0_ref, b1_ref, lq1, lk1, lq2, lk2, sw_ref,
                      o_ref, kb_s, vt_s, m_s, acc_s):
    T = q_ref.shape[0]
    n_tiles = kb_s.shape[0]

    @pl.when(qi == 0)
    def _():
        for t in range(n_tiles):
            kb_s[t] = k_ref[t * T:(t + 1) * T, :].astype(BF16)
            vt_s[t, 0:DV_D, :] = v_ref[t * T:(t + 1) * T, :].T.astype(BF16)
            vt_s[t, DV_D:, :] = jnp.ones((ONES_ROWS, T), BF16)

    q = q_ref[...] * (DK_D ** -0.5 * LOG2E)
    lane = lax.broadcasted_iota(jnp.int32, q.shape, 1)
    qpad = (jnp.where(lane < DK_D, q, 0.0).astype(BF16), jnp.where(lane >= DK_D, q, 0.0).astype(BF16))

    def group(tiles, state):
        scores = []
        for kj, bias in tiles:
            kt = kb_s[kj]
            for c in range(2):
                s = _dot_nt(kt, qpad[c])
                scores.append(s if bias is None else s + bias)
        parts = ([], [])
        for t, (kj, _) in enumerate(tiles):
            vt = vt_s[kj]
            for c in range(2):
                s = scores[2 * t + c]
                m = jnp.max(s, axis=0, keepdims=True)
                parts[c].append((m, _dot(vt, jnp.exp2(s - m).astype(BF16))))
        out = []
        for c in range(2):
            m_old, acc_old = state[c]
            m_new = m_old
            for m, _ in parts[c]:
                m_new = jnp.maximum(m_new, m)
            acc_new = jnp.exp2(m_old - m_new) * acc_old
            for m, pv in parts[c]:
                acc_new = acc_new + jnp.exp2(m - m_new) * pv
            out.append((m_new, acc_new))
        return tuple(out)

    def load():
        return tuple((m_s[c], acc_s[c]) for c in range(2))

    def store(state):
        for c in range(2):
            m_s[c], acc_s[c] = state[c]

    n_all = qi + 1
    n_head = functools.reduce(lambda acc, G: jnp.where(n_all >= G, jnp.maximum(acc, G), acc), HEAD_GROUPS, 0)
    n_far = n_all - n_head
    state = tuple((jnp.full((1, T), -jnp.inf, F32), jnp.zeros((DV_D + ONES_ROWS, T), F32)) for _ in range(2))
    done = 0
    for G in FAR_GROUPS:
        n_grp = (n_far - done) // G
        state = lax.fori_loop(0, n_grp, lambda g, st, G=G, done=done: group(
            [(done + g * G + t, None) for t in range(G)], st), state)
        done = done + n_grp * G
    store(state)

    for G in HEAD_GROUPS:
        @pl.when(n_head == G)
        def _(G=G):
            bias = [None] * (G - 2) + [b1_ref[0], b0_ref[0]]
            store(group([(qi - (G - 1) + t, bias[-G:][t]) for t in range(G)], load()))

    (_, a0), (_, a1) = load()
    lam = _lambda(lq1, lk1, lq2, lk2)
    att_t = a0[0:DV_D] / a0[DV_D:DV_D + 1] - lam * (a1[0:DV_D] / a1[DV_D:DV_D + 1])
    o_ref[...] = _subln(att_t.T, sw_ref)


def _attn_prompt_kernel(*refs):
    _attn_prompt_body(pl.program_id(1), *refs)


def _attn_prompt_specs(S, T, extra=()):
    hw = 2 * DK_D
    in_specs = [
        pl.BlockSpec((T, hw), lambda h, i, *_: (i, ZC_QD // hw + h)),
        pl.BlockSpec((S, hw), lambda h, i, *_: (0, h)),
        pl.BlockSpec((S, DV_D), lambda h, i, *_: (0, h)),
        pl.BlockSpec((1, T, T), lambda h, i, *_: (h, 0, 0)),
        pl.BlockSpec((1, T, T), lambda h, i, *_: (h, 0, 0)),
    ]
    out_spec = pl.BlockSpec((T, DV_D), lambda h, i, *_: (i, h))
    scratch = [
        pltpu.VMEM((S // T, T, hw), BF16),
        pltpu.VMEM((S // T, DV_D + ONES_ROWS, T), BF16),
        pltpu.VMEM((2, 1, T), F32),
        pltpu.VMEM((2, DV_D + ONES_ROWS, T), F32),
    ]
    return in_specs, out_spec, scratch


def _small_specs(shapes):
    return [pl.BlockSpec(shape, lambda *_, n=len(shape): (0,) * n) for shape in shapes]


_LAMBDA_AND_SUBLN = [(1, DK_D)] * 4 + [(1, DV_D)]


def _attn_prompt(z, kp, vp, bias0, bias1, lq1, lk1, lq2, lk2, subln_w, S):
    T = bias0.shape[-1]
    in_specs, out_spec, scratch = _attn_prompt_specs(S, T)
    return pl.pallas_call(
        _attn_prompt_kernel,
        grid=(H_D, S // T),
        in_specs=in_specs + _small_specs(_LAMBDA_AND_SUBLN),
        out_specs=out_spec,
        out_shape=jax.ShapeDtypeStruct((S, W_D), BF16),
        scratch_shapes=scratch,
        compiler_params=_params("arbitrary", "arbitrary"),
        name="attn_prompt",
    )(z, kp, vp, bias0, bias1, lq1, lk1, lq2, lk2, subln_w)


def _head_rows(ref, h):
    n_keys, n_heads, width = ref.shape
    return ref.reshape(n_keys * n_heads, width)[pl.ds(h, n_keys, stride=n_heads), :]


def _attn_sample_body(p, n_steps, q_ref, *refs, pps):
    kc = refs[0:pps]
    vc = refs[pps:2 * pps]
    (kn_ref, vn_ref, blast_ref, bnew_ref, lq1, lk1, lq2, lk2, sw_ref,
     o_ref, qbd_s, m_s, acc_s) = refs[2 * pps:]
    last_step = p == n_steps - 1
    T = q_ref.shape[2]
    PG = kc[0].shape[0]

    @pl.when(p == 0)
    def _():
        lane = lax.broadcasted_iota(jnp.int32, (T, 2 * DK_D), 1)
        for h in range(H_D):
            q = q_ref[0, h] * (DK_D ** -0.5 * LOG2E)
            qbd_s[h, 0:T, :] = jnp.where(lane < DK_D, q, 0.0).astype(BF16)
            qbd_s[h, T:2 * T, :] = jnp.where(lane >= DK_D, q, 0.0).astype(BF16)
        m_s[...] = jnp.full_like(m_s, -jnp.inf)
        acc_s[...] = jnp.zeros_like(acc_s)

    def update(keys, values, biases):
        scores = []
        for h in range(H_D):
            s = _dot_nt(qbd_s[h], keys[h])
            scores.append(s if biases[h] is None else s + biases[h])
        for h in range(H_D):
            s = scores[h]
            m_old = m_s[h]
            m_new = jnp.maximum(m_old, jnp.max(s, axis=1, keepdims=True))
            pr = jnp.exp2(s - m_new).astype(BF16)
            v = values[h]
            v_ext = jnp.concatenate([v.astype(BF16), jnp.ones(v.shape, BF16)], axis=1)
            acc_s[h] = jnp.exp2(m_old - m_new) * acc_s[h] + _dot(pr, v_ext)
            m_s[h] = m_new

    pad = jnp.zeros((2 * T, (pps - 1) * PG), F32)
    keys, values, biases = [], [], []
    for h in range(H_D):
        keys.append(jnp.concatenate([_head_rows(kc[i], h) for i in range(pps)], axis=0).astype(BF16))
        values.append(jnp.concatenate([_head_rows(vc[i], h) for i in range(pps)], axis=0))
        b_end = jnp.where(last_step, blast_ref[h], 0.0)
        biases.append(jnp.concatenate([pad, b_end], axis=1) if pps > 1 else b_end)
    update(keys, values, biases)

    @pl.when(last_step)
    def _():
        zeros = jnp.zeros((PG - T, 2 * DK_D), F32)
        update([jnp.concatenate([kn_ref[0, h], zeros], axis=0).astype(BF16) for h in range(H_D)],
               [jnp.concatenate([vn_ref[0, h], zeros], axis=0) for h in range(H_D)],
               [bnew_ref[h] for h in range(H_D)])
        lam = _lambda(lq1, lk1, lq2, lk2)
        for h in range(H_D):
            r = acc_s[h, :, 0:DV_D] / acc_s[h, :, DV_D:DV_D + 1]
            o_ref[0, h] = _subln(r[0:T] - lam * r[T:2 * T], sw_ref)


def _attn_sample_kernel(pt_ref, *refs, pps):
    _attn_sample_body(pl.program_id(1), pl.num_programs(1), *refs, pps=pps)


def _attn_sample_specs(qs, blast, bnew, PG, pps, seq_step):
    B, _, T, _ = qs.shape

    def per_seq(width):
        return pl.BlockSpec((1, H_D, T, width), lambda *g: (seq_step(*g[:-1])[0], 0, 0, 0))

    def page_spec(i, width):
        def index(*g):
            b, p = seq_step(*g[:-1])
            return (0, g[-1][b, p * pps + i], 0, 0, 0)
        return pl.BlockSpec((None, None, PG, H_D, width), index)

    in_specs = ([per_seq(2 * DK_D)]
                + [page_spec(i, 2 * DK_D) for i in range(pps)]
                + [page_spec(i, DV_D) for i in range(pps)]
                + [per_seq(2 * DK_D), per_seq(DV_D)]
                + _small_specs([blast.shape, bnew.shape]))
    scratch = [
        pltpu.VMEM((H_D, 2 * T, 2 * DK_D), BF16),
        pltpu.VMEM((H_D, 2 * T, 1), F32),
        pltpu.VMEM((H_D, 2 * T, 2 * DV_D), F32),
    ]
    return in_specs, per_seq(DV_D), scratch


def _attn_sample(page_table, qs, cache_k, cache_v, kn, vn, blast, bnew, lq1, lk1, lq2, lk2, subln_w, pps):
    B, _, T, _ = qs.shape
    n_pages = page_table.shape[1]
    in_specs, out_spec, scratch = _attn_sample_specs(qs, blast, bnew, cache_k.shape[2], pps, lambda b, p: (b, p))
    grid_spec = pltpu.PrefetchScalarGridSpec(
        num_scalar_prefetch=1,
        grid=(B, n_pages // pps),
        in_specs=in_specs + _small_specs(_LAMBDA_AND_SUBLN),
        out_specs=out_spec,
        scratch_shapes=scratch,
    )
    return pl.pallas_call(
        functools.partial(_attn_sample_kernel, pps=pps),
        grid_spec=grid_spec,
        out_shape=jax.ShapeDtypeStruct((B, H_D, T, DV_D), BF16),
        compiler_params=_params("arbitrary", "arbitrary"),
        name="attn_sample",
    )(page_table, qs, *([cache_k] * pps), *([cache_v] * pps), kn, vn, blast, bnew,
      lq1, lk1, lq2, lk2, subln_w)


def _attn_fused_kernel(pt_ref, *refs, pps, n_sp, n_sample_in):
    n_prompt_in = 5
    p_in = refs[0:n_prompt_in]
    s_in = refs[n_prompt_in:n_prompt_in + n_sample_in]
    shared = refs[n_prompt_in + n_sample_in:n_prompt_in + n_sample_in + 5]
    o_p, o_s = refs[n_prompt_in + n_sample_in + 5:n_prompt_in + n_sample_in + 7]
    scratch = refs[n_prompt_in + n_sample_in + 7:]
    qi = pl.program_id(1)
    sid = pl.program_id(0) * pl.num_programs(1) + qi
    _attn_prompt_body(qi, *p_in, *shared, o_p, *scratch[0:4])
    _attn_sample_body(sid % n_sp, n_sp, *s_in, *shared, o_s, *scratch[4:], pps=pps)


def _attn_fused(page_table, z, kp, vp, bias0, bias1, qs, cache_k, cache_v, kn, vn, blast, bnew,
                lq1, lk1, lq2, lk2, subln_w, S, pps):
    T = bias0.shape[-1]
    B, _, TS, _ = qs.shape
    nq = S // T
    n_sp = page_table.shape[1] // pps
    assert H_D * nq == B * n_sp
    p_specs, p_out, p_scratch = _attn_prompt_specs(S, T)
    seq_step = lambda h, i: ((h * nq + i) // n_sp, (h * nq + i) % n_sp)
    s_specs, s_out, s_scratch = _attn_sample_specs(qs, blast, bnew, cache_k.shape[2], pps, seq_step)
    grid_spec = pltpu.PrefetchScalarGridSpec(
        num_scalar_prefetch=1,
        grid=(H_D, nq),
        in_specs=p_specs + s_specs + _small_specs(_LAMBDA_AND_SUBLN),
        out_specs=[p_out, s_out],
        scratch_shapes=p_scratch + s_scratch,
    )
    return pl.pallas_call(
        functools.partial(_attn_fused_kernel, pps=pps, n_sp=n_sp, n_sample_in=len(s_specs)),
        grid_spec=grid_spec,
        out_shape=[jax.ShapeDtypeStruct((S, W_D), BF16), jax.ShapeDtypeStruct((B, H_D, TS, DV_D), BF16)],
        compiler_params=_params("arbitrary", "arbitrary"),
        name="attn_fused",
    )(page_table, z, kp, vp, bias0, bias1, qs, *([cache_k] * pps), *([cache_v] * pps), kn, vn, blast, bnew,
      lq1, lk1, lq2, lk2, subln_w)


def _merge_kernel(hmp_ref, hms_ref, atp_ref, ats_ref, wa_ref, wb_ref, ga_ref, gb_ref, u_ref, *, n_prompt):
    def body(hm_ref, at_ref):
        ya = _dot(hm_ref[...], wa_ref[...])
        yb = _dot(at_ref[...], wb_ref[...])
        u_ref[...] = (_sigmoid(ga_ref[...]) * ya + _sigmoid(gb_ref[...]) * yb).astype(BF16)

    @pl.when(pl.program_id(0) < n_prompt)
    def _():
        body(hmp_ref, atp_ref)

    @pl.when(pl.program_id(0) >= n_prompt)
    def _():
        body(hms_ref, ats_ref)


def _merge(hm_p, hm_s, att_p, att_s, w_a, w_b, z, D):
    S, BT = hm_p.shape[0], hm_s.shape[0]
    tm, n_prompt, n_sample, prow, srow = _two_way_rows(S, BT, (1024, 512, 256, 128))
    tn = _pick(D, (512, 256, 128))
    ga0, gb0 = ZC_GA // tn, (ZC_GA + D) // tn
    return pl.pallas_call(
        functools.partial(_merge_kernel, n_prompt=n_prompt),
        grid=(n_prompt + n_sample, D // tn),
        in_specs=[
            pl.BlockSpec((tm, W_M), lambda i, j: (prow(i), 0)),
            pl.BlockSpec((tm, W_M), lambda i, j: (srow(i), 0)),
            pl.BlockSpec((tm, W_D), lambda i, j: (prow(i), 0)),
            pl.BlockSpec((tm, W_D), lambda i, j: (srow(i), 0)),
            pl.BlockSpec((W_M, tn), lambda i, j: (0, j)),
            pl.BlockSpec((W_D, tn), lambda i, j: (0, j)),
            pl.BlockSpec((tm, tn), lambda i, j: (i, ga0 + j)),
            pl.BlockSpec((tm, tn), lambda i, j: (i, gb0 + j)),
        ],
        out_specs=pl.BlockSpec((tm, tn), lambda i, j: (i, j)),
        out_shape=jax.ShapeDtypeStruct((S + BT, D), BF16),
        compiler_params=_params("parallel", "parallel"),
        name="merge",
    )(hm_p, hm_s, att_p, att_s, w_a, w_b, z, z)


def _out_proj_kernel(u_ref, w_ref, xp_ref, xs_ref, o_ref, *, n_prompt):
    y = _dot(u_ref[...], w_ref[...])

    @pl.when(pl.program_id(0) < n_prompt)
    def _():
        o_ref[...] = xp_ref[...] + y

    @pl.when(pl.program_id(0) >= n_prompt)
    def _():
        o_ref[...] = xs_ref[...] + y


def _out_proj(u, w_out, xp, xs):
    S, D = xp.shape
    BT = xs.shape[0]
    tm, n_prompt, n_sample, prow, srow = _two_way_rows(S, BT, (1024, 512, 256, 128))
    tn = _pick(D, (512, 256, 128))
    return pl.pallas_call(
        functools.partial(_out_proj_kernel, n_prompt=n_prompt),
        grid=(n_prompt + n_sample, D // tn),
        in_specs=[
            pl.BlockSpec((tm, D), lambda i, j: (i, 0)),
            pl.BlockSpec((D, tn), lambda i, j: (0, j)),
            pl.BlockSpec((tm, tn), lambda i, j: (prow(i), jnp.where(i < n_prompt, j, D // tn - 1))),
            pl.BlockSpec((tm, tn), lambda i, j: (srow(i), jnp.where(i < n_prompt, 0, j))),
        ],
        out_specs=pl.BlockSpec((tm, tn), lambda i, j: (i, j)),
        out_shape=jax.ShapeDtypeStruct((S + BT, D), F32),
        compiler_params=_params("parallel", "parallel"),
        name="out_proj",
    )(u, w_out, xp, xs)


def _ffn_kernel(x_ref, nw_ref, w1_ref, w2_ref, fw_ref, yp_ref, ys_ref, xn_s, acc_s, *, n_prompt):
    f = pl.program_id(1)

    @pl.when(f == 0)
    def _():
        x = x_ref[...]
        ms = jnp.mean(x * x, axis=-1, keepdims=True)
        xn_s[...] = (x * lax.rsqrt(ms + EPS) * nw_ref[...]).astype(BF16)
        acc_s[...] = jnp.zeros_like(acc_s)

    hid = jnp.maximum(_dot(xn_s[...], w1_ref[...]), 0.0)
    acc_s[...] += _dot((hid * hid).astype(BF16), w2_ref[...])

    def final(y_ref):
        x2 = x_ref[...] + acc_s[...]
        ms = jnp.mean(x2 * x2, axis=-1, keepdims=True)
        y_ref[...] = x2 * lax.rsqrt(ms + EPS) * fw_ref[...]

    last = f == pl.num_programs(1) - 1

    @pl.when(last & (pl.program_id(0) < n_prompt))
    def _():
        final(yp_ref)

    @pl.when(last & (pl.program_id(0) >= n_prompt))
    def _():
        final(ys_ref)


def _ffn(x, norm_w, w1, w2, final_w, S):
    R, D = x.shape
    DF = w1.shape[1]
    tm, n_prompt, n_sample, prow, srow = _two_way_rows(S, R - S, (512, 256, 128))
    tf = _pick(DF, (1024, 512, 256, 128))
    return pl.pallas_call(
        functools.partial(_ffn_kernel, n_prompt=n_prompt),
        grid=(n_prompt + n_sample, DF // tf),
        in_specs=[
            pl.BlockSpec((tm, D), lambda i, f: (i, 0)),
            pl.BlockSpec((1, D), lambda i, f: (0, 0)),
            pl.BlockSpec((D, tf), lambda i, f: (0, f)),
            pl.BlockSpec((tf, D), lambda i, f: (f, 0)),
            pl.BlockSpec((1, D), lambda i, f: (0, 0)),
        ],
        out_specs=[
            pl.BlockSpec((tm, D), lambda i, f: (prow(i), 0)),
            pl.BlockSpec((tm, D), lambda i, f: (srow(i), 0)),
        ],
        out_shape=[jax.ShapeDtypeStruct((S, D), F32), jax.ShapeDtypeStruct((R - S, D), F32)],
        scratch_shapes=[pltpu.VMEM((tm, D), BF16), pltpu.VMEM((tm, D), F32)],
        compiler_params=_params("arbitrary", "arbitrary"),
        name="ffn",
    )(x, norm_w, w1, w2, final_w)


def _bias_by_distance(rel_bias, n):
    d = jnp.arange(n, dtype=jnp.int32)
    max_exact = N_BUCKETS // 2
    nf = jnp.maximum(d, 1).astype(F32)
    large = max_exact + jnp.floor(jnp.log(nf / max_exact) / math.log(MAX_DIST / max_exact)
                                  * (N_BUCKETS - max_exact))
    large = jnp.minimum(large, N_BUCKETS - 1.0)
    bucket = jnp.where(d < max_exact, d.astype(F32), large)
    onehot = (bucket[:, None] == jnp.arange(N_BUCKETS, dtype=F32)[None, :]).astype(F32)
    return jnp.dot(onehot, rel_bias.astype(F32), precision=HIGHEST).T


def _toeplitz(w, rows, cols):
    n = w.shape[-1]
    assert cols <= n - 1
    lead = w.shape[:-1]
    flat = jnp.tile(w, (1,) * len(lead) + (rows,))[..., :rows * (n - 1)]
    return flat.reshape(lead + (rows, n - 1))[..., :cols]


def _prompt_bias_tiles(rel_bias, T):
    assert T + 1 >= MAX_DIST
    bd = _bias_by_distance(rel_bias, 2 * T)
    val = (bd - bd[:, 2 * T - 1:]) * LOG2E
    neg = jnp.full((H_D, T), NEG, F32)
    t0 = _toeplitz(jnp.concatenate([val[:, :T], neg], axis=1), T, T)
    t1 = _toeplitz(jnp.concatenate([val[:, T:], val[:, :T]], axis=1), T, T)
    return t0, t1


def _sample_bias_tables(rel_bias, T, PG):
    assert PG + 1 >= MAX_DIST
    bd = _bias_by_distance(rel_bias, 2 * PG + T)
    bd = (bd - bd[:, 2 * PG + T - 1:]) * LOG2E
    w_last = jnp.concatenate([bd[:, PG:0:-1], bd[:, :1], bd[:, PG + T - 1:PG:-1]], axis=1)
    last = _toeplitz(w_last, T, PG)
    w_new = jnp.concatenate([bd[:, :1], jnp.full((H_D, T), NEG, F32), bd[:, T - 1:0:-1]], axis=1)
    new = jnp.concatenate([_toeplitz(w_new, T, T), jnp.full((H_D, T, PG - T), NEG, F32)], axis=2)
    both_maps = lambda t: jnp.concatenate([t, t], axis=1).astype(F32)
    return both_maps(last), both_maps(new)


def kernel(x_prompt, x_sample, cache_k, cache_v, page_table, state_C, state_n, state_m, state_conv,
           norm1_w, w_in, b_i, b_f, conv_w, conv_b, hnorm_w, lambda_q1, lambda_k1, lambda_q2, lambda_k2,
           subln_w, rel_bias, w_a, w_b, w_out, norm2_w, w_ff1, w_ff2, final_norm_w):
    assert w_in.shape[0] == 1 and x_prompt.shape[0] == 1
    _, S, D = x_prompt.shape
    B, T, _ = x_sample.shape
    PG = cache_k.shape[2]
    xp = x_prompt[0]
    xs = x_sample.reshape(B * T, D)

    o_i = 2 * QK_M + 2 * W_M
    o_qd = o_i + 2 * H_M
    w_t = jnp.transpose(w_in[0])
    w_all = w_t.astype(BF16)
    w_rest = w_all[o_qd:]
    w_gate_row = w_all[o_i:o_qd]
    gate_pad = jnp.zeros((LANES - H_M, D), BF16)
    w_gate_col = jnp.concatenate([w_gate_row[:H_M], gate_pad, w_gate_row[H_M:], gate_pad], axis=0)
    bias_c = jnp.zeros((1, 2 * LANES), F32).at[0, 0:H_M].set(b_i[0]).at[0, LANES:LANES + H_M].set(b_f[0])
    bias_r = jnp.concatenate([b_i[0], b_f[0]])[:, None]

    z, kp, ks, vp, vs, gc, gr = _in_proj(xp, xs, norm1_w, w_all, w_rest, w_gate_col, w_gate_row)

    hm_p, c_p, n_p, m_p, conv_p = _mlstm_prompt(z, gc, gr, conv_w[0], conv_b, bias_c, bias_r, hnorm_w, S)
    hist = jnp.pad(state_conv[0], ((0, 0), (T - (CONV_W - 1), 0), (0, 0))).reshape(B * T, 2 * QK_M)
    m0p = jnp.repeat(jnp.pad(state_m[0], ((0, 0), (0, LANES - H_M))), T, axis=0)
    hm_s, c_s, n_s, m_s = _mlstm_sample(z, gc, gr, hist, state_C[0], state_n[0], m0p, conv_w[0], conv_b,
                                        bias_c, bias_r, hnorm_w, S, B, T)

    TQ = _pick(S, (256, 128))
    t0, t1 = _prompt_bias_tiles(rel_bias, TQ)
    lq1, lk1, lq2, lk2 = lambda_q1, lambda_k1, lambda_q2, lambda_k2
    blast, bnew = _sample_bias_tables(rel_bias, T, PG)
    per_head = lambda a, width: jnp.transpose(a.reshape(B, T, H_D, width), (0, 2, 1, 3))
    qs = per_head(z[S:, ZC_QD:ZC_QD + QK_D], 2 * DK_D)
    kn = per_head(ks, 2 * DK_D)
    vn = per_head(vs, DV_D)
    n_pages = page_table.shape[1]
    pps = _pick(n_pages, (PAGES_PER_STEP, 4, 2, 1))
    if H_D * (S // TQ) == B * (n_pages // pps):
        att_p, att_s = _attn_fused(page_table, z, kp, vp, t0, t1, qs, cache_k, cache_v, kn, vn, blast, bnew,
                                   lq1, lk1, lq2, lk2, subln_w, S, pps)
    else:
        att_p = _attn_prompt(z, kp, vp, t0, t1, lq1, lk1, lq2, lk2, subln_w, S)
        att_s = _attn_sample(page_table, qs, cache_k, cache_v, kn, vn, blast, bnew,
                             lq1, lk1, lq2, lk2, subln_w, pps)
    att_s = jnp.transpose(att_s, (0, 2, 1, 3))

    u = _merge(hm_p, hm_s, att_p, att_s.reshape(B * T, W_D), w_a[0].astype(BF16), w_b[0].astype(BF16), z, D)
    x1 = _out_proj(u, w_out[0].astype(BF16), xp, xs)
    y_p, y_s = _ffn(x1, norm2_w, w_ff1[0].astype(BF16), w_ff2[0].astype(BF16), final_norm_w[None, :], S)

    conv_prompt = conv_p[SUBLANES - (CONV_W - 1):].reshape(1, 1, CONV_W - 1, 2 * QK_M)
    conv_sample = z[S:, :2 * QK_M].reshape(B, T, 2 * QK_M)[:, T - (CONV_W - 1):][None]
    return (y_p.reshape(1, S, D), y_s.reshape(B, T, D),
            kp.reshape(1, 1, S, H_D, 2 * DK_D), vp.reshape(1, 1, S, H_D, DV_D),
            c_p[None, None], n_p[None, None], m_p[:, :H_M][None], conv_prompt,
            ks.reshape(1, B, T, H_D, 2 * DK_D), vs.reshape(1, B, T, H_D, DV_D),
            c_s[None], n_s[None], m_s[::T, :H_M][None], conv_sample)
```

```python
import functools
import math

import numpy as np
import jax
import jax.numpy as jnp
from jax import lax
from jax.experimental import pallas as pl
from jax.experimental.pallas import tpu as pltpu

F32 = jnp.float32
BF16 = jnp.bfloat16
HIGHEST = lax.Precision.HIGHEST

H_M = 8
DK_M = 128
DV_M = 128
QK_M = H_M * DK_M
W_M = H_M * DV_M
CONV_W = 4
H_D = 8
DK_D = 64
DV_D = 128
QK_D = H_D * 2 * DK_D
W_D = H_D * DV_D
N_BUCKETS = 32
MAX_DIST = 128
EPS = 1e-6
LAM_INIT = 0.8 - 0.6 * math.exp(-0.3 * 0)
NEG = -1e30
LOG2E = math.log2(math.e)

ZC_QK = 0
ZC_VM = 2 * QK_M
ZC_OM = ZC_VM + W_M
ZC_QD = ZC_OM + W_M
ZC_GA = ZC_QD + QK_D

LANES = 128
SUBLANES = 8
VMEM_LIMIT = 56 * 1024 * 1024

SEQ_BLOCK = 16
PAGES_PER_STEP = 8
FAR_GROUPS = (16, 8, 4, 2, 1)
HEAD_GROUPS = (16, 8, 4, 2, 1)
ONES_ROWS = 16


def _params(*sem):
    return pltpu.CompilerParams(dimension_semantics=sem, vmem_limit_bytes=VMEM_LIMIT)


def _pick(n, prefs):
    for p in prefs:
        if n % p == 0:
            return p
    return n


def _sigmoid(x):
    return 1.0 / (1.0 + jnp.exp(-x))


def _log_sigmoid(x):
    return jnp.minimum(x, 0.0) - jnp.log(1.0 + jnp.exp(-jnp.abs(x)))


def _dot(a, b):
    return jnp.dot(a, b, preferred_element_type=F32)


def _dot_nt(a, b):
    return lax.dot_general(a, b, (((1,), (1,)), ((), ())), preferred_element_type=F32)


def _dot_tn(a, b):
    return lax.dot_general(a, b, (((0,), (0,)), ((), ())), preferred_element_type=F32)


def _dot_exact(a, b):
    return jnp.dot(a, b, preferred_element_type=F32, precision=HIGHEST)


def _two_way_rows(S, BT, prefs):
    tm = _pick(math.gcd(S, BT), prefs)
    n_prompt = S // tm
    prow = lambda i: jnp.minimum(i, n_prompt - 1)
    srow = lambda i: jnp.maximum(i - n_prompt, 0)
    return tm, n_prompt, BT // tm, prow, srow


def _in_proj_kernel(xp_ref, xs_ref, nw_ref, wa_ref, wb_ref, wgc_ref, wgr_ref,
                    z_ref, kp_ref, ks_ref, vp_ref, vs_ref, gc_ref, gr_ref, xn_ref, *, n_prompt, n_a, n_z1, nkt):
    i = pl.program_id(0)
    j = pl.program_id(1)
    is_p = i < n_prompt
    is_s = jnp.logical_not(is_p)

    def norm(x_ref):
        x = x_ref[...]
        ms = jnp.mean(x * x, axis=-1, keepdims=True)
        xn = (x * lax.rsqrt(ms + EPS) * nw_ref[...]).astype(BF16)
        xn_ref[...] = xn
        gc_ref[...] = _dot_nt(xn, wgc_ref[...])
        gr_ref[...] = _dot_nt(wgr_ref[...], xn)

    @pl.when((j == 0) & is_p)
    def _():
        norm(xp_ref)

    @pl.when((j == 0) & is_s)
    def _():
        norm(xs_ref)

    in_k = (j >= n_z1) & (j < n_z1 + nkt)
    in_v = (j >= n_z1 + nkt) & (j < n_z1 + 2 * nkt)
    in_zb = (j >= n_a) & jnp.logical_not(in_k | in_v)
    for cond, w_ref, o_ref in ((j < n_a, wa_ref, z_ref), (in_zb, wb_ref, z_ref),
                               (in_k & is_p, wb_ref, kp_ref), (in_k & is_s, wb_ref, ks_ref),
                               (in_v & is_p, wb_ref, vp_ref), (in_v & is_s, wb_ref, vs_ref)):
        @pl.when(cond)
        def _(w_ref=w_ref, o_ref=o_ref):
            o_ref[...] = _dot_nt(xn_ref[...], w_ref[...])


def _in_proj(xp, xs, norm_w, w_all, w_rest, w_gate_col, w_gate_row):
    S, D = xp.shape
    BT = xs.shape[0]
    tm, n_prompt, n_sample, prow, srow = _two_way_rows(S, BT, (1024, 512, 256, 128))
    tn = _pick(math.gcd(D, QK_D), (512, 256, 128))
    n_a = ZC_QD // tn
    n_z1 = ZC_GA // tn
    nkt = QK_D // tn
    n_tiles = n_a + w_rest.shape[0] // tn
    nzt = n_tiles - 2 * nkt
    zcol = lambda j: jnp.where(j < n_z1, j, jnp.where(j < n_z1 + 2 * nkt, n_z1 - 1, j - 2 * nkt))
    kcol = lambda j: jnp.clip(j - n_z1, 0, nkt - 1)
    vcol = lambda j: jnp.clip(j - n_z1 - nkt, 0, nkt - 1)
    p_spec = lambda col: pl.BlockSpec((tm, tn), lambda i, j: (prow(i), jnp.where(i < n_prompt, col(j), nkt - 1)))
    s_spec = lambda col: pl.BlockSpec((tm, tn), lambda i, j: (srow(i), jnp.where(i < n_prompt, 0, col(j))))
    return pl.pallas_call(
        functools.partial(_in_proj_kernel, n_prompt=n_prompt, n_a=n_a, n_z1=n_z1, nkt=nkt),
        grid=(n_prompt + n_sample, n_tiles),
        in_specs=[
            pl.BlockSpec((tm, D), lambda i, j: (prow(i), 0), pipeline_mode=pl.Buffered(1)),
            pl.BlockSpec((tm, D), lambda i, j: (srow(i), 0), pipeline_mode=pl.Buffered(1)),
            pl.BlockSpec((1, D), lambda i, j: (0, 0)),
            pl.BlockSpec((tn, D), lambda i, j: (jnp.minimum(j, n_a - 1), 0)),
            pl.BlockSpec((tn, D), lambda i, j: (jnp.maximum(j - n_a, 0), 0)),
            pl.BlockSpec((2 * LANES, D), lambda i, j: (0, 0)),
            pl.BlockSpec((2 * SUBLANES, D), lambda i, j: (0, 0)),
        ],
        out_specs=[
            pl.BlockSpec((tm, tn), lambda i, j: (i, zcol(j))),
            p_spec(kcol), s_spec(kcol), p_spec(vcol), s_spec(vcol),
            pl.BlockSpec((tm, 2 * LANES), lambda i, j: (i, 0)),
            pl.BlockSpec((2 * SUBLANES, tm), lambda i, j: (0, i)),
        ],
        out_shape=[
            jax.ShapeDtypeStruct((S + BT, nzt * tn), F32),
            jax.ShapeDtypeStruct((S, QK_D), F32),
            jax.ShapeDtypeStruct((BT, QK_D), F32),
            jax.ShapeDtypeStruct((S, W_D), F32),
            jax.ShapeDtypeStruct((BT, W_D), F32),
            jax.ShapeDtypeStruct((S + BT, 2 * LANES), F32),
            jax.ShapeDtypeStruct((2 * SUBLANES, S + BT), F32),
        ],
        scratch_shapes=[pltpu.VMEM((tm, D), BF16)],
        compiler_params=_params("arbitrary", "arbitrary"),
        name="in_proj",
    )(xp, xs, norm_w, w_all, w_rest, w_gate_col, w_gate_row)


def _conv_silu(x, hist, hist_shift, cw_ref, cb_ref, row_in_seq):
    acc = cb_ref[...] + cw_ref[CONV_W - 1:CONV_W, :] * x
    for j in range(1, CONV_W):
        xr = pltpu.roll(x, j, axis=0)
        hr = pltpu.roll(hist, (j + hist_shift) % hist.shape[0], axis=0)
        if hist.shape[0] != x.shape[0]:
            first = jnp.where(row_in_seq[0:SUBLANES] < j, hr, xr[0:SUBLANES])
            xs = jnp.concatenate([first, xr[SUBLANES:]], axis=0)
        else:
            xs = jnp.where(row_in_seq < j, hr, xr)
        acc = acc + cw_ref[CONV_W - 1 - j:CONV_W - j, :] * xs
    return acc * _sigmoid(acc)


def _mlstm_intra(qb, kb, vb, mask, bt_c, bt_r, ig_r, inter_c):
    dlog = jnp.where(mask, bt_c - bt_r + ig_r, -jnp.inf)
    m_t = jnp.maximum(inter_c, jnp.max(dlog, axis=1, keepdims=True))
    dw = jnp.exp(dlog - m_t)
    iw = jnp.exp(inter_c - m_t)
    s = _dot_nt(qb, kb) * dw
    sv = _dot(s.astype(BF16), vb)
    return sv, jnp.sum(s, axis=1, keepdims=True), m_t, iw


def _head_out(num, den, m_t, o, hw_ref):
    den = jnp.maximum(jnp.abs(den), jnp.exp(-m_t))
    h = num / den
    hn = h * lax.rsqrt(jnp.mean(h * h, axis=-1, keepdims=True) + EPS) * hw_ref[...]
    return (hn * _sigmoid(o)).astype(BF16)


def _mlstm_prompt_kernel(zq_ref, zv_ref, zo_ref, gc_ref, gr_ref, cw_ref, cb_ref, bc_ref, br_ref, hw_ref,
                         h_ref, c_out, n_out, m_out, conv_out,
                         c_s, n_s, m_s, hist_s):
    c = pl.program_id(0)
    L = zq_ref.shape[0]

    @pl.when(c == 0)
    def _():
        c_s[...] = jnp.zeros_like(c_s)
        n_s[...] = jnp.zeros_like(n_s)
        m_s[...] = jnp.zeros_like(m_s)
        hist_s[...] = jnp.zeros_like(hist_s)

    x = zq_ref[...]
    row = lax.broadcasted_iota(jnp.int32, (L, 1), 0)
    qk = _conv_silu(x, hist_s[...], 0, cw_ref, cb_ref, row)
    hist_s[...] = x[L - SUBLANES:L, :]
    conv_out[...] = x[L - SUBLANES:L, :]

    gcol = gc_ref[...] + bc_ref[...]
    grow = gr_ref[...] + br_ref[...]
    ig_c = gcol[:, 0:LANES]
    lf_c = _log_sigmoid(gcol[:, LANES:2 * LANES])
    ig_r = grow[0:SUBLANES, :]
    lf_r = _log_sigmoid(grow[SUBLANES:2 * SUBLANES, :])
    ri = lax.broadcasted_iota(jnp.int32, (L, L), 0)
    ci = lax.broadcasted_iota(jnp.int32, (L, L), 1)
    mask = ci <= ri
    bt_c = _dot_exact(mask.astype(F32), lf_c)
    bt_r = _dot_exact(lf_r, (ri <= ci).astype(F32))
    m_prev = m_s[...]
    inter = bt_c + m_prev
    b_last = bt_c[L - 1:L, :]
    wlog = b_last - bt_c + ig_c
    m_new = jnp.maximum(b_last + m_prev, jnp.max(wlog, axis=0, keepdims=True))
    ws = jnp.exp(wlog - m_new)
    decay = jnp.exp(b_last + m_prev - m_new)
    m_s[...] = m_new
    m_out[...] = m_new

    for h in range(H_M):
        q = qk[:, h * DK_M:(h + 1) * DK_M]
        k = qk[:, QK_M + h * DK_M:QK_M + (h + 1) * DK_M] * (DK_M ** -0.5)
        v = zv_ref[:, h * DV_M:(h + 1) * DV_M]
        qb, kb, vb = q.astype(BF16), k.astype(BF16), v.astype(BF16)
        sv, ssum, m_t, iw = _mlstm_intra(qb, kb, vb, mask, bt_c[:, h:h + 1], bt_r[h:h + 1, :],
                                         ig_r[h:h + 1, :], inter[:, h:h + 1])
        C = c_s[h]
        n_row = n_s[h:h + 1, :]
        num = sv + iw * _dot_nt(qb, C.astype(BF16))
        den = ssum + iw * jnp.sum(q * n_row, axis=1, keepdims=True)
        h_ref[:, h * DV_M:(h + 1) * DV_M] = _head_out(num, den, m_t, zo_ref[:, h * DV_M:(h + 1) * DV_M], hw_ref)
        ws_h = ws[:, h:h + 1]
        dc = decay[:, h:h + 1]
        c_new = dc * C + _dot_tn((v * ws_h).astype(BF16), kb)
        n_new = dc * n_row + jnp.sum(ws_h * k, axis=0, keepdims=True)
        c_s[h] = c_new
        n_s[h:h + 1, :] = n_new
        c_out[h] = c_new
        n_out[h:h + 1, :] = n_new


def _mlstm_prompt(z, gc, gr, conv_w, conv_b, bias_c, bias_r, hnorm_w, S):
    L = _pick(S, (256, 128))
    nz = lambda col, width: col // width
    return pl.pallas_call(
        _mlstm_prompt_kernel,
        grid=(S // L,),
        in_specs=[
            pl.BlockSpec((L, 2 * QK_M), lambda c: (c, nz(ZC_QK, 2 * QK_M))),
            pl.BlockSpec((L, W_M), lambda c: (c, nz(ZC_VM, W_M))),
            pl.BlockSpec((L, W_M), lambda c: (c, nz(ZC_OM, W_M))),
            pl.BlockSpec((L, 2 * LANES), lambda c: (c, 0)),
            pl.BlockSpec((2 * SUBLANES, L), lambda c: (0, c)),
            pl.BlockSpec((CONV_W, 2 * QK_M), lambda c: (0, 0)),
            pl.BlockSpec((1, 2 * QK_M), lambda c: (0, 0)),
            pl.BlockSpec((1, 2 * LANES), lambda c: (0, 0)),
            pl.BlockSpec((2 * SUBLANES, 1), lambda c: (0, 0)),
            pl.BlockSpec((1, DV_M), lambda c: (0, 0)),
        ],
        out_specs=[
            pl.BlockSpec((L, W_M), lambda c: (c, 0)),
            pl.BlockSpec((H_M, DV_M, DK_M), lambda c: (0, 0, 0)),
            pl.BlockSpec((H_M, DK_M), lambda c: (0, 0)),
            pl.BlockSpec((1, LANES), lambda c: (0, 0)),
            pl.BlockSpec((SUBLANES, 2 * QK_M), lambda c: (0, 0)),
        ],
        out_shape=[
            jax.ShapeDtypeStruct((S, W_M), BF16),
            jax.ShapeDtypeStruct((H_M, DV_M, DK_M), F32),
            jax.ShapeDtypeStruct((H_M, DK_M), F32),
            jax.ShapeDtypeStruct((1, LANES), F32),
            jax.ShapeDtypeStruct((SUBLANES, 2 * QK_M), F32),
        ],
        scratch_shapes=[
            pltpu.VMEM((H_M, DV_M, DK_M), F32),
            pltpu.VMEM((H_M, DK_M), F32),
            pltpu.VMEM((1, LANES), F32),
            pltpu.VMEM((SUBLANES, 2 * QK_M), F32),
        ],
        compiler_params=_params("arbitrary"),
        name="mlstm_prompt",
    )(z, z, z, gc, gr, conv_w, conv_b, bias_c, bias_r, hnorm_w)


def _mlstm_sample_kernel(zq_ref, zv_ref, zo_ref, gc_ref, gr_ref, hist_ref, c0_ref, n0_ref, m0_ref,
                         cw_ref, cb_ref, bc_ref, br_ref, hw_ref,
                         h_ref, c_out, n_out, m_out, *, T):
    L = zq_ref.shape[0]
    NB = L // T
    x = zq_ref[...]
    ri = lax.broadcasted_iota(jnp.int32, (L, L), 0)
    ci = lax.broadcasted_iota(jnp.int32, (L, L), 1)
    same = (ri // T) == (ci // T)
    mask = same & (ci <= ri)
    row_t = lax.broadcasted_iota(jnp.int32, (L, 1), 0) % T
    qk = _conv_silu(x, hist_ref[...], L - T, cw_ref, cb_ref, row_t)

    gcol = gc_ref[...] + bc_ref[...]
    grow = gr_ref[...] + br_ref[...]
    ig_c = gcol[:, 0:LANES]
    lf_c = _log_sigmoid(gcol[:, LANES:2 * LANES])
    ig_r = grow[0:SUBLANES, :]
    lf_r = _log_sigmoid(grow[SUBLANES:2 * SUBLANES, :])
    bt_c = _dot_exact(mask.astype(F32), lf_c)
    bt_r = _dot_exact(lf_r, (same & (ri <= ci)).astype(F32))
    m_prev = m0_ref[...]
    last = same & (ci % T == T - 1)
    b_last = _dot_exact(last.astype(F32), bt_c)
    inter = bt_c + m_prev
    wlog = b_last - bt_c + ig_c
    wmax = jnp.max(wlog.reshape(NB, T, LANES), axis=1, keepdims=True)
    wmax = jnp.broadcast_to(wmax, (NB, T, LANES)).reshape(L, LANES)
    m_new = jnp.maximum(b_last + m_prev, wmax)
    ws = jnp.exp(wlog - m_new)
    decay = jnp.exp(b_last + m_prev - m_new)
    m_out[...] = m_new

    lane_seq = lax.broadcasted_iota(jnp.int32, (L, NB * DV_M), 1) // DV_M
    row_seq = lax.broadcasted_iota(jnp.int32, (L, NB * DV_M), 0) // T
    blockdiag = lane_seq == row_seq

    for h in range(H_M):
        q = qk[:, h * DK_M:(h + 1) * DK_M]
        k = qk[:, QK_M + h * DK_M:QK_M + (h + 1) * DK_M] * (DK_M ** -0.5)
        v = zv_ref[:, h * DV_M:(h + 1) * DV_M]
        qb, kb, vb = q.astype(BF16), k.astype(BF16), v.astype(BF16)
        sv, ssum, m_t, iw = _mlstm_intra(qb, kb, vb, mask, bt_c[:, h:h + 1], bt_r[h:h + 1, :],
                                         ig_r[h:h + 1, :], inter[:, h:h + 1])
        C = c0_ref[:, h]
        c_flat = C.reshape(NB * DV_M, DK_M)
        qc_all = _dot_nt(qb, c_flat.astype(BF16))
        qc = jnp.concatenate([qc_all[b * T:(b + 1) * T, b * DV_M:(b + 1) * DV_M] for b in range(NB)], axis=0)
        n_rows = jnp.broadcast_to(n0_ref[:, h:h + 1, :], (NB, T, DK_M)).reshape(L, DK_M)
        num = sv + iw * qc
        den = ssum + iw * jnp.sum(q * n_rows, axis=1, keepdims=True)
        h_ref[:, h * DV_M:(h + 1) * DV_M] = _head_out(num, den, m_t, zo_ref[:, h * DV_M:(h + 1) * DV_M], hw_ref)
        ws_h = ws[:, h:h + 1]
        vw = v * ws_h
        vw_exp = jnp.where(blockdiag, jnp.concatenate([vw] * NB, axis=1), 0.0).astype(BF16)
        upd = _dot_tn(vw_exp, kb).reshape(NB, DV_M, DK_M)
        dc = decay[:, h:h + 1].reshape(NB, T, 1)[:, 0:1, :]
        c_out[:, h] = dc * C + upd
        kw = (ws_h * k).reshape(NB, T, DK_M)
        n_out[:, h:h + 1, :] = dc * n0_ref[:, h:h + 1, :] + jnp.sum(kw, axis=1, keepdims=True)


def _mlstm_sample(z, gc, gr, hist, c0, n0, m0p, conv_w, conv_b, bias_c, bias_r, hnorm_w, S, B, T):
    NB = SEQ_BLOCK
    L = NB * T
    assert L == LANES and B % NB == 0 and S % L == 0
    r0 = S // L
    nz = lambda col, width: col // width
    return pl.pallas_call(
        functools.partial(_mlstm_sample_kernel, T=T),
        grid=(B // NB,),
        in_specs=[
            pl.BlockSpec((L, 2 * QK_M), lambda i: (r0 + i, nz(ZC_QK, 2 * QK_M))),
            pl.BlockSpec((L, W_M), lambda i: (r0 + i, nz(ZC_VM, W_M))),
            pl.BlockSpec((L, W_M), lambda i: (r0 + i, nz(ZC_OM, W_M))),
            pl.BlockSpec((L, 2 * LANES), lambda i: (r0 + i, 0)),
            pl.BlockSpec((2 * SUBLANES, L), lambda i: (0, r0 + i)),
            pl.BlockSpec((L, 2 * QK_M), lambda i: (i, 0)),
            pl.BlockSpec((NB, H_M, DV_M, DK_M), lambda i: (i, 0, 0, 0)),
            pl.BlockSpec((NB, H_M, DK_M), lambda i: (i, 0, 0)),
            pl.BlockSpec((L, LANES), lambda i: (i, 0)),
            pl.BlockSpec((CONV_W, 2 * QK_M), lambda i: (0, 0)),
            pl.BlockSpec((1, 2 * QK_M), lambda i: (0, 0)),
            pl.BlockSpec((1, 2 * LANES), lambda i: (0, 0)),
            pl.BlockSpec((2 * SUBLANES, 1), lambda i: (0, 0)),
            pl.BlockSpec((1, DV_M), lambda i: (0, 0)),
        ],
        out_specs=[
            pl.BlockSpec((L, W_M), lambda i: (i, 0)),
            pl.BlockSpec((NB, H_M, DV_M, DK_M), lambda i: (i, 0, 0, 0)),
            pl.BlockSpec((NB, H_M, DK_M), lambda i: (i, 0, 0)),
            pl.BlockSpec((L, LANES), lambda i: (i, 0)),
        ],
        out_shape=[
            jax.ShapeDtypeStruct((B * T, W_M), BF16),
            jax.ShapeDtypeStruct((B, H_M, DV_M, DK_M), F32),
            jax.ShapeDtypeStruct((B, H_M, DK_M), F32),
            jax.ShapeDtypeStruct((B * T, LANES), F32),
        ],
        compiler_params=_params("parallel"),
        name="mlstm_sample",
    )(z, z, z, gc, gr, hist, c0, n0, m0p, conv_w, conv_b, bias_c, bias_r, hnorm_w)


def _lambda(lq1, lk1, lq2, lk2):
    a = jnp.sum(lq1[...] * lk1[...], axis=-1, keepdims=True)
    b = jnp.sum(lq2[...] * lk2[...], axis=-1, keepdims=True)
    return jnp.exp(a) - jnp.exp(b) + LAM_INIT


def _subln(att, w_ref):
    y = att * lax.rsqrt(jnp.mean(att * att, axis=-1, keepdims=True) + EPS) * w_ref[...]
    return (y * (1.0 - LAM_INIT)).astype(BF16)


def _attn_prompt_body(qi, q_ref, k_ref, v_ref, b0_ref, b1_ref, lq1, lk1, lq2, lk2, sw_ref,
                      o_ref, kb_s, vt_s, m_s, acc_s):
    T = q_ref.shape[0]
    n_tiles = kb_s.shape[0]

    @pl.when(qi == 0)
    def _():
        for t in range(n_tiles):
            kb_s[t] = k_ref[t * T:(t + 1) * T, :].astype(BF16)
            vt_s[t, 0:DV_D, :] = v_ref[t * T:(t + 1) * T, :].T.astype(BF16)
            vt_s[t, DV_D:, :] = jnp.ones((ONES_ROWS, T), BF16)

    q = q_ref[...] * (DK_D ** -0.5 * LOG2E)
    lane = lax.broadcasted_iota(jnp.int32, q.shape, 1)
    qpad = (jnp.where(lane < DK_D, q, 0.0).astype(BF16), jnp.where(lane >= DK_D, q, 0.0).astype(BF16))

    def group(tiles, state):
        scores = []
        for kj, bias in tiles:
            kt = kb_s[kj]
            for c in range(2):
                s = _dot_nt(kt, qpad[c])
                scores.append(s if bias is None else s + bias)
        parts = ([], [])
        for t, (kj, _) in enumerate(tiles):
            vt = vt_s[kj]
            for c in range(2):
                s = scores[2 * t + c]
                m = jnp.max(s, axis=0, keepdims=True)
                parts[c].append((m, _dot(vt, jnp.exp2(s - m).astype(BF16))))
        out = []
        for c in range(2):
            m_old, acc_old = state[c]
            m_new = m_old
            for m, _ in parts[c]:
                m_new = jnp.maximum(m_new, m)
            acc_new = jnp.exp2(m_old - m_new) * acc_old
            for m, pv in parts[c]:
                acc_new = acc_new + jnp.exp2(m - m_new) * pv
            out.append((m_new, acc_new))
        return tuple(out)

    def load():
        return tuple((m_s[c], acc_s[c]) for c in range(2))

    def store(state):
        for c in range(2):
            m_s[c], acc_s[c] = state[c]

    n_all = qi + 1
    n_head = functools.reduce(lambda acc, G: jnp.where(n_all >= G, jnp.maximum(acc, G), acc), HEAD_GROUPS, 0)
    n_far = n_all - n_head
    state = tuple((jnp.full((1, T), -jnp.inf, F32), jnp.zeros((DV_D + ONES_ROWS, T), F32)) for _ in range(2))
    done = 0
    for G in FAR_GROUPS:
        n_grp = (n_far - done) // G
        state = lax.fori_loop(0, n_grp, lambda g, st, G=G, done=done: group(
            [(done + g * G + t, None) for t in range(G)], st), state)
        done = done + n_grp * G
    store(state)

    for G in HEAD_GROUPS:
        @pl.when(n_head == G)
        def _(G=G):
            bias = [None] * (G - 2) + [b1_ref[0], b0_ref[0]]
            store(group([(qi - (G - 1) + t, bias[-G:][t]) for t in range(G)], load()))

    (_, a0), (_, a1) = load()
    lam = _lambda(lq1, lk1, lq2, lk2)
    att_t = a0[0:DV_D] / a0[DV_D:DV_D + 1] - lam * (a1[0:DV_D] / a1[DV_D:DV_D + 1])
    o_ref[...] = _subln(att_t.T, sw_ref)


def _attn_prompt_kernel(*refs):
    _attn_prompt_body(pl.program_id(1), *refs)


def _attn_prompt_specs(S, T, extra=()):
    hw = 2 * DK_D
    in_specs = [
        pl.BlockSpec((T, hw), lambda h, i, *_: (i, ZC_QD // hw + h)),
        pl.BlockSpec((S, hw), lambda h, i, *_: (0, h)),
        pl.BlockSpec((S, DV_D), lambda h, i, *_: (0, h)),
        pl.BlockSpec((1, T, T), lambda h, i, *_: (h, 0, 0)),
        pl.BlockSpec((1, T, T), lambda h, i, *_: (h, 0, 0)),
    ]
    out_spec = pl.BlockSpec((T, DV_D), lambda h, i, *_: (i, h))
    scratch = [
        pltpu.VMEM((S // T, T, hw), BF16),
        pltpu.VMEM((S // T, DV_D + ONES_ROWS, T), BF16),
        pltpu.VMEM((2, 1, T), F32),
        pltpu.VMEM((2, DV_D + ONES_ROWS, T), F32),
    ]
    return in_specs, out_spec, scratch


def _small_specs(shapes):
    return [pl.BlockSpec(shape, lambda *_, n=len(shape): (0,) * n) for shape in shapes]


_LAMBDA_AND_SUBLN = [(1, DK_D)] * 4 + [(1, DV_D)]


def _attn_prompt(z, kp, vp, bias0, bias1, lq1, lk1, lq2, lk2, subln_w, S):
    T = bias0.shape[-1]
    in_specs, out_spec, scratch = _attn_prompt_specs(S, T)
    return pl.pallas_call(
        _attn_prompt_kernel,
        grid=(H_D, S // T),
        in_specs=in_specs + _small_specs(_LAMBDA_AND_SUBLN),
        out_specs=out_spec,
        out_shape=jax.ShapeDtypeStruct((S, W_D), BF16),
        scratch_shapes=scratch,
        compiler_params=_params("arbitrary", "arbitrary"),
        name="attn_prompt",
    )(z, kp, vp, bias0, bias1, lq1, lk1, lq2, lk2, subln_w)


def _head_rows(ref, h):
    n_keys, n_heads, width = ref.shape
    return ref.reshape(n_keys * n_heads, width)[pl.ds(h, n_keys, stride=n_heads), :]


def _attn_sample_body(p, n_steps, q_ref, *refs, pps):
    kc = refs[0:pps]
    vc = refs[pps:2 * pps]
    (kn_ref, vn_ref, blast_ref, bnew_ref, lq1, lk1, lq2, lk2, sw_ref,
     o_ref, qbd_s, m_s, acc_s) = refs[2 * pps:]
    last_step = p == n_steps - 1
    T = q_ref.shape[2]
    PG = kc[0].shape[0]

    @pl.when(p == 0)
    def _():
        lane = lax.broadcasted_iota(jnp.int32, (T, 2 * DK_D), 1)
        for h in range(H_D):
            q = q_ref[0, h] * (DK_D ** -0.5 * LOG2E)
            qbd_s[h, 0:T, :] = jnp.where(lane < DK_D, q, 0.0).astype(BF16)
            qbd_s[h, T:2 * T, :] = jnp.where(lane >= DK_D, q, 0.0).astype(BF16)
        m_s[...] = jnp.full_like(m_s, -jnp.inf)
        acc_s[...] = jnp.zeros_like(acc_s)

    def update(keys, values, biases):
        scores = []
        for h in range(H_D):
            s = _dot_nt(qbd_s[h], keys[h])
            scores.append(s if biases[h] is None else s + biases[h])
        for h in range(H_D):
            s = scores[h]
            m_old = m_s[h]
            m_new = jnp.maximum(m_old, jnp.max(s, axis=1, keepdims=True))
            pr = jnp.exp2(s - m_new).astype(BF16)
            v = values[h]
            v_ext = jnp.concatenate([v.astype(BF16), jnp.ones(v.shape, BF16)], axis=1)
            acc_s[h] = jnp.exp2(m_old - m_new) * acc_s[h] + _dot(pr, v_ext)
            m_s[h] = m_new

    pad = jnp.zeros((2 * T, (pps - 1) * PG), F32)
    keys, values, biases = [], [], []
    for h in range(H_D):
        keys.append(jnp.concatenate([_head_rows(kc[i], h) for i in range(pps)], axis=0).astype(BF16))
        values.append(jnp.concatenate([_head_rows(vc[i], h) for i in range(pps)], axis=0))
        b_end = jnp.where(last_step, blast_ref[h], 0.0)
        biases.append(jnp.concatenate([pad, b_end], axis=1) if pps > 1 else b_end)
    update(keys, values, biases)

    @pl.when(last_step)
    def _():
        zeros = jnp.zeros((PG - T, 2 * DK_D), F32)
        update([jnp.concatenate([kn_ref[0, h], zeros], axis=0).astype(BF16) for h in range(H_D)],
               [jnp.concatenate([vn_ref[0, h], zeros], axis=0) for h in range(H_D)],
               [bnew_ref[h] for h in range(H_D)])
        lam = _lambda(lq1, lk1, lq2, lk2)
        for h in range(H_D):
            r = acc_s[h, :, 0:DV_D] / acc_s[h, :, DV_D:DV_D + 1]
            o_ref[0, h] = _subln(r[0:T] - lam * r[T:2 * T], sw_ref)


def _attn_sample_kernel(pt_ref, *refs, pps):
    _attn_sample_body(pl.program_id(1), pl.num_programs(1), *refs, pps=pps)


def _attn_sample_specs(qs, blast, bnew, PG, pps, seq_step):
    B, _, T, _ = qs.shape

    def per_seq(width):
        return pl.BlockSpec((1, H_D, T, width), lambda *g: (seq_step(*g[:-1])[0], 0, 0, 0))

    def page_spec(i, width):
        def index(*g):
            b, p = seq_step(*g[:-1])
            return (0, g[-1][b, p * pps + i], 0, 0, 0)
        return pl.BlockSpec((None, None, PG, H_D, width), index)

    in_specs = ([per_seq(2 * DK_D)]
                + [page_spec(i, 2 * DK_D) for i in range(pps)]
                + [page_spec(i, DV_D) for i in range(pps)]
                + [per_seq(2 * DK_D), per_seq(DV_D)]
                + _small_specs([blast.shape, bnew.shape]))
    scratch = [
        pltpu.VMEM((H_D, 2 * T, 2 * DK_D), BF16),
        pltpu.VMEM((H_D, 2 * T, 1), F32),
        pltpu.VMEM((H_D, 2 * T, 2 * DV_D), F32),
    ]
    return in_specs, per_seq(DV_D), scratch


def _attn_sample(page_table, qs, cache_k, cache_v, kn, vn, blast, bnew, lq1, lk1, lq2, lk2, subln_w, pps):
    B, _, T, _ = qs.shape
    n_pages = page_table.shape[1]
    in_specs, out_spec, scratch = _attn_sample_specs(qs, blast, bnew, cache_k.shape[2], pps, lambda b, p: (b, p))
    grid_spec = pltpu.PrefetchScalarGridSpec(
        num_scalar_prefetch=1,
        grid=(B, n_pages // pps),
        in_specs=in_specs + _small_specs(_LAMBDA_AND_SUBLN),
        out_specs=out_spec,
        scratch_shapes=scratch,
    )
    return pl.pallas_call(
        functools.partial(_attn_sample_kernel, pps=pps),
        grid_spec=grid_spec,
        out_shape=jax.ShapeDtypeStruct((B, H_D, T, DV_D), BF16),
        compiler_params=_params("arbitrary", "arbitrary"),
        name="attn_sample",
    )(page_table, qs, *([cache_k] * pps), *([cache_v] * pps), kn, vn, blast, bnew,
      lq1, lk1, lq2, lk2, subln_w)


def _attn_fused_kernel(pt_ref, *refs, pps, n_sp, n_sample_in):
    n_prompt_in = 5
    p_in = refs[0:n_prompt_in]
    s_in = refs[n_prompt_in:n_prompt_in + n_sample_in]
    shared = refs[n_prompt_in + n_sample_in:n_prompt_in + n_sample_in + 5]
    o_p, o_s = refs[n_prompt_in + n_sample_in + 5:n_prompt_in + n_sample_in + 7]
    scratch = refs[n_prompt_in + n_sample_in + 7:]
    qi = pl.program_id(1)
    sid = pl.program_id(0) * pl.num_programs(1) + qi
    _attn_prompt_body(qi, *p_in, *shared, o_p, *scratch[0:4])
    _attn_sample_body(sid % n_sp, n_sp, *s_in, *shared, o_s, *scratch[4:], pps=pps)


def _attn_fused(page_table, z, kp, vp, bias0, bias1, qs, cache_k, cache_v, kn, vn, blast, bnew,
                lq1, lk1, lq2, lk2, subln_w, S, pps):
    T = bias0.shape[-1]
    B, _, TS, _ = qs.shape
    nq = S // T
    n_sp = page_table.shape[1] // pps
    assert H_D * nq == B * n_sp
    p_specs, p_out, p_scratch = _attn_prompt_specs(S, T)
    seq_step = lambda h, i: ((h * nq + i) // n_sp, (h * nq + i) % n_sp)
    s_specs, s_out, s_scratch = _attn_sample_specs(qs, blast, bnew, cache_k.shape[2], pps, seq_step)
    grid_spec = pltpu.PrefetchScalarGridSpec(
        num_scalar_prefetch=1,
        grid=(H_D, nq),
        in_specs=p_specs + s_specs + _small_specs(_LAMBDA_AND_SUBLN),
        out_specs=[p_out, s_out],
        scratch_shapes=p_scratch + s_scratch,
    )
    return pl.pallas_call(
        functools.partial(_attn_fused_kernel, pps=pps, n_sp=n_sp, n_sample_in=len(s_specs)),
        grid_spec=grid_spec,
        out_shape=[jax.ShapeDtypeStruct((S, W_D), BF16), jax.ShapeDtypeStruct((B, H_D, TS, DV_D), BF16)],
        compiler_params=_params("arbitrary", "arbitrary"),
        name="attn_fused",
    )(page_table, z, kp, vp, bias0, bias1, qs, *([cache_k] * pps), *([cache_v] * pps), kn, vn, blast, bnew,
      lq1, lk1, lq2, lk2, subln_w)


def _merge_kernel(hmp_ref, hms_ref, atp_ref, ats_ref, wa_ref, wb_ref, ga_ref, gb_ref, u_ref, *, n_prompt):
    def body(hm_ref, at_ref):
        ya = _dot(hm_ref[...], wa_ref[...])
        yb = _dot(at_ref[...], wb_ref[...])
        u_ref[...] = (_sigmoid(ga_ref[...]) * ya + _sigmoid(gb_ref[...]) * yb).astype(BF16)

    @pl.when(pl.program_id(0) < n_prompt)
    def _():
        body(hmp_ref, atp_ref)

    @pl.when(pl.program_id(0) >= n_prompt)
    def _():
        body(hms_ref, ats_ref)


def _merge(hm_p, hm_s, att_p, att_s, w_a, w_b, z, D):
    S, BT = hm_p.shape[0], hm_s.shape[0]
    tm, n_prompt, n_sample, prow, srow = _two_way_rows(S, BT, (1024, 512, 256, 128))
    tn = _pick(D, (512, 256, 128))
    ga0, gb0 = ZC_GA // tn, (ZC_GA + D) // tn
    return pl.pallas_call(
        functools.partial(_merge_kernel, n_prompt=n_prompt),
        grid=(n_prompt + n_sample, D // tn),
        in_specs=[
            pl.BlockSpec((tm, W_M), lambda i, j: (prow(i), 0)),
            pl.BlockSpec((tm, W_M), lambda i, j: (srow(i), 0)),
            pl.BlockSpec((tm, W_D), lambda i, j: (prow(i), 0)),
            pl.BlockSpec((tm, W_D), lambda i, j: (srow(i), 0)),
            pl.BlockSpec((W_M, tn), lambda i, j: (0, j)),
            pl.BlockSpec((W_D, tn), lambda i, j: (0, j)),
            pl.BlockSpec((tm, tn), lambda i, j: (i, ga0 + j)),
            pl.BlockSpec((tm, tn), lambda i, j: (i, gb0 + j)),
        ],
        out_specs=pl.BlockSpec((tm, tn), lambda i, j: (i, j)),
        out_shape=jax.ShapeDtypeStruct((S + BT, D), BF16),
        compiler_params=_params("parallel", "parallel"),
        name="merge",
    )(hm_p, hm_s, att_p, att_s, w_a, w_b, z, z)


def _out_proj_kernel(u_ref, w_ref, xp_ref, xs_ref, o_ref, *, n_prompt):
    y = _dot(u_ref[...], w_ref[...])

    @pl.when(pl.program_id(0) < n_prompt)
    def _():
        o_ref[...] = xp_ref[...] + y

    @pl.when(pl.program_id(0) >= n_prompt)
    def _():
        o_ref[...] = xs_ref[...] + y


def _out_proj(u, w_out, xp, xs):
    S, D = xp.shape
    BT = xs.shape[0]
    tm, n_prompt, n_sample, prow, srow = _two_way_rows(S, BT, (1024, 512, 256, 128))
    tn = _pick(D, (512, 256, 128))
    return pl.pallas_call(
        functools.partial(_out_proj_kernel, n_prompt=n_prompt),
        grid=(n_prompt + n_sample, D // tn),
        in_specs=[
            pl.BlockSpec((tm, D), lambda i, j: (i, 0)),
            pl.BlockSpec((D, tn), lambda i, j: (0, j)),
            pl.BlockSpec((tm, tn), lambda i, j: (prow(i), jnp.where(i < n_prompt, j, D // tn - 1))),
            pl.BlockSpec((tm, tn), lambda i, j: (srow(i), jnp.where(i < n_prompt, 0, j))),
        ],
        out_specs=pl.BlockSpec((tm, tn), lambda i, j: (i, j)),
        out_shape=jax.ShapeDtypeStruct((S + BT, D), F32),
        compiler_params=_params("parallel", "parallel"),
        name="out_proj",
    )(u, w_out, xp, xs)


def _ffn_kernel(x_ref, nw_ref, w1_ref, w2_ref, fw_ref, yp_ref, ys_ref, xn_s, acc_s, *, n_prompt):
    f = pl.program_id(1)

    @pl.when(f == 0)
    def _():
        x = x_ref[...]
        ms = jnp.mean(x * x, axis=-1, keepdims=True)
        xn_s[...] = (x * lax.rsqrt(ms + EPS) * nw_ref[...]).astype(BF16)
        acc_s[...] = jnp.zeros_like(acc_s)

    hid = jnp.maximum(_dot(xn_s[...], w1_ref[...]), 0.0)
    acc_s[...] += _dot((hid * hid).astype(BF16), w2_ref[...])

    def final(y_ref):
        x2 = x_ref[...] + acc_s[...]
        ms = jnp.mean(x2 * x2, axis=-1, keepdims=True)
        y_ref[...] = x2 * lax.rsqrt(ms + EPS) * fw_ref[...]

    last = f == pl.num_programs(1) - 1

    @pl.when(last & (pl.program_id(0) < n_prompt))
    def _():
        final(yp_ref)

    @pl.when(last & (pl.program_id(0) >= n_prompt))
    def _():
        final(ys_ref)


def _ffn(x, norm_w, w1, w2, final_w, S):
    R, D = x.shape
    DF = w1.shape[1]
    tm, n_prompt, n_sample, prow, srow = _two_way_rows(S, R - S, (512, 256, 128))
    tf = _pick(DF, (1024, 512, 256, 128))
    return pl.pallas_call(
        functools.partial(_ffn_kernel, n_prompt=n_prompt),
        grid=(n_prompt + n_sample, DF // tf),
        in_specs=[
            pl.BlockSpec((tm, D), lambda i, f: (i, 0)),
            pl.BlockSpec((1, D), lambda i, f: (0, 0)),
            pl.BlockSpec((D, tf), lambda i, f: (0, f)),
            pl.BlockSpec((tf, D), lambda i, f: (f, 0)),
            pl.BlockSpec((1, D), lambda i, f: (0, 0)),
        ],
        out_specs=[
            pl.BlockSpec((tm, D), lambda i, f: (prow(i), 0)),
            pl.BlockSpec((tm, D), lambda i, f: (srow(i), 0)),
        ],
        out_shape=[jax.ShapeDtypeStruct((S, D), F32), jax.ShapeDtypeStruct((R - S, D), F32)],
        scratch_shapes=[pltpu.VMEM((tm, D), BF16), pltpu.VMEM((tm, D), F32)],
        compiler_params=_params("arbitrary", "arbitrary"),
        name="ffn",
    )(x, norm_w, w1, w2, final_w)


def _bias_by_distance(rel_bias, n):
    d = jnp.arange(n, dtype=jnp.int32)
    max_exact = N_BUCKETS // 2
    nf = jnp.maximum(d, 1).astype(F32)
    large = max_exact + jnp.floor(jnp.log(nf / max_exact) / math.log(MAX_DIST / max_exact)
                                  * (N_BUCKETS - max_exact))
    large = jnp.minimum(large, N_BUCKETS - 1.0)
    bucket = jnp.where(d < max_exact, d.astype(F32), large)
    onehot = (bucket[:, None] == jnp.arange(N_BUCKETS, dtype=F32)[None, :]).astype(F32)
    return jnp.dot(onehot, rel_bias.astype(F32), precision=HIGHEST).T


def _toeplitz(w, rows, cols):
    n = w.shape[-1]
    assert cols <= n - 1
    lead = w.shape[:-1]
    flat = jnp.tile(w, (1,) * len(lead) + (rows,))[..., :rows * (n - 1)]
    return flat.reshape(lead + (rows, n - 1))[..., :cols]


def _prompt_bias_tiles(rel_bias, T):
    assert T + 1 >= MAX_DIST
    bd = _bias_by_distance(rel_bias, 2 * T)
    val = (bd - bd[:, 2 * T - 1:]) * LOG2E
    neg = jnp.full((H_D, T), NEG, F32)
    t0 = _toeplitz(jnp.concatenate([val[:, :T], neg], axis=1), T, T)
    t1 = _toeplitz(jnp.concatenate([val[:, T:], val[:, :T]], axis=1), T, T)
    return t0, t1


def _sample_bias_tables(rel_bias, T, PG):
    assert PG + 1 >= MAX_DIST
    bd = _bias_by_distance(rel_bias, 2 * PG + T)
    bd = (bd - bd[:, 2 * PG + T - 1:]) * LOG2E
    w_last = jnp.concatenate([bd[:, PG:0:-1], bd[:, :1], bd[:, PG + T - 1:PG:-1]], axis=1)
    last = _toeplitz(w_last, T, PG)
    w_new = jnp.concatenate([bd[:, :1], jnp.full((H_D, T), NEG, F32), bd[:, T - 1:0:-1]], axis=1)
    new = jnp.concatenate([_toeplitz(w_new, T, T), jnp.full((H_D, T, PG - T), NEG, F32)], axis=2)
    both_maps = lambda t: jnp.concatenate([t, t], axis=1).astype(F32)
    return both_maps(last), both_maps(new)


def kernel(x_prompt, x_sample, cache_k, cache_v, page_table, state_C, state_n, state_m, state_conv,
           norm1_w, w_in, b_i, b_f, conv_w, conv_b, hnorm_w, lambda_q1, lambda_k1, lambda_q2, lambda_k2,
           subln_w, rel_bias, w_a, w_b, w_out, norm2_w, w_ff1, w_ff2, final_norm_w):
    assert w_in.shape[0] == 1 and x_prompt.shape[0] == 1
    _, S, D = x_prompt.shape
    B, T, _ = x_sample.shape
    PG = cache_k.shape[2]
    xp = x_prompt[0]
    xs = x_sample.reshape(B * T, D)

    o_i = 2 * QK_M + 2 * W_M
    o_qd = o_i + 2 * H_M
    w_t = jnp.transpose(w_in[0])
    w_all = w_t.astype(BF16)
    w_rest = w_all[o_qd:]
    w_gate_row = w_all[o_i:o_qd]
    gate_pad = jnp.zeros((LANES - H_M, D), BF16)
    w_gate_col = jnp.concatenate([w_gate_row[:H_M], gate_pad, w_gate_row[H_M:], gate_pad], axis=0)
    bias_c = jnp.zeros((1, 2 * LANES), F32).at[0, 0:H_M].set(b_i[0]).at[0, LANES:LANES + H_M].set(b_f[0])
    bias_r = jnp.concatenate([b_i[0], b_f[0]])[:, None]

    z, kp, ks, vp, vs, gc, gr = _in_proj(xp, xs, norm1_w, w_all, w_rest, w_gate_col, w_gate_row)

    hm_p, c_p, n_p, m_p, conv_p = _mlstm_prompt(z, gc, gr, conv_w[0], conv_b, bias_c, bias_r, hnorm_w, S)
    hist = jnp.pad(state_conv[0], ((0, 0), (T - (CONV_W - 1), 0), (0, 0))).reshape(B * T, 2 * QK_M)
    m0p = jnp.repeat(jnp.pad(state_m[0], ((0, 0), (0, LANES - H_M))), T, axis=0)
    hm_s, c_s, n_s, m_s = _mlstm_sample(z, gc, gr, hist, state_C[0], state_n[0], m0p, conv_w[0], conv_b,
                                        bias_c, bias_r, hnorm_w, S, B, T)

    TQ = _pick(S, (256, 128))
    t0, t1 = _prompt_bias_tiles(rel_bias, TQ)
    lq1, lk1, lq2, lk2 = lambda_q1, lambda_k1, lambda_q2, lambda_k2
    blast, bnew = _sample_bias_tables(rel_bias, T, PG)
    per_head = lambda a, width: jnp.transpose(a.reshape(B, T, H_D, width), (0, 2, 1, 3))
    qs = per_head(z[S:, ZC_QD:ZC_QD + QK_D], 2 * DK_D)
    kn = per_head(ks, 2 * DK_D)
    vn = per_head(vs, DV_D)
    n_pages = page_table.shape[1]
    pps = _pick(n_pages, (PAGES_PER_STEP, 4, 2, 1))
    if H_D * (S // TQ) == B * (n_pages // pps):
        att_p, att_s = _attn_fused(page_table, z, kp, vp, t0, t1, qs, cache_k, cache_v, kn, vn, blast, bnew,
                                   lq1, lk1, lq2, lk2, subln_w, S, pps)
    else:
        att_p = _attn_prompt(z, kp, vp, t0, t1, lq1, lk1, lq2, lk2, subln_w, S)
        att_s = _attn_sample(page_table, qs, cache_k, cache_v, kn, vn, blast, bnew,
                             lq1, lk1, lq2, lk2, subln_w, pps)
    att_s = jnp.transpose(att_s, (0, 2, 1, 3))

    u = _merge(hm_p, hm_s, att_p, att_s.reshape(B * T, W_D), w_a[0].astype(BF16), w_b[0].astype(BF16), z, D)
    x1 = _out_proj(u, w_out[0].astype(BF16), xp, xs)
    y_p, y_s = _ffn(x1, norm2_w, w_ff1[0].astype(BF16), w_ff2[0].astype(BF16), final_norm_w[None, :], S)

    conv_prompt = conv_p[SUBLANES - (CONV_W - 1):].reshape(1, 1, CONV_W - 1, 2 * QK_M)
    conv_sample = z[S:, :2 * QK_M].reshape(B, T, 2 * QK_M)[:, T - (CONV_W - 1):][None]
    return (y_p.reshape(1, S, D), y_s.reshape(B, T, D),
            kp.reshape(1, 1, S, H_D, 2 * DK_D), vp.reshape(1, 1, S, H_D, DV_D),
            c_p[None, None], n_p[None, None], m_p[:, :H_M][None], conv_prompt,
            ks.reshape(1, B, T, H_D, 2 * DK_D), vs.reshape(1, B, T, H_D, DV_D),
            c_s[None], n_s[None], m_s[::T, :H_M][None], conv_sample)
```

```python
import functools
import math

import numpy as np
import jax
import jax.numpy as jnp
from jax import lax
from jax.experimental import pallas as pl
from jax.experimental.pallas import tpu as pltpu

F32 = jnp.float32
BF16 = jnp.bfloat16
HIGHEST = lax.Precision.HIGHEST

H_M = 8
DK_M = 128
DV_M = 128
QK_M = H_M * DK_M
W_M = H_M * DV_M
CONV_W = 4
H_D = 8
DK_D = 64
DV_D = 128
QK_D = H_D * 2 * DK_D
W_D = H_D * DV_D
N_BUCKETS = 32
MAX_DIST = 128
EPS = 1e-6
LAM_INIT = 0.8 - 0.6 * math.exp(-0.3 * 0)
NEG = -1e30
LOG2E = math.log2(math.e)

ZC_QK = 0
ZC_VM = 2 * QK_M
ZC_OM = ZC_VM + W_M
ZC_QD = ZC_OM + W_M
ZC_GA = ZC_QD + QK_D

LANES = 128
SUBLANES = 8
VMEM_LIMIT = 56 * 1024 * 1024

SEQ_BLOCK = 16
PAGES_PER_STEP = 8
FAR_GROUPS = (16, 8, 4, 2, 1)
HEAD_GROUPS = (16, 8, 4, 2, 1)
ONES_ROWS = 16


def _params(*sem):
    return pltpu.CompilerParams(dimension_semantics=sem, vmem_limit_bytes=VMEM_LIMIT)


def _pick(n, prefs):
    for p in prefs:
        if n % p == 0:
            return p
    return n


def _sigmoid(x):
    return 1.0 / (1.0 + jnp.exp(-x))


def _log_sigmoid(x):
    return jnp.minimum(x, 0.0) - jnp.log(1.0 + jnp.exp(-jnp.abs(x)))


def _dot(a, b):
    return jnp.dot(a, b, preferred_element_type=F32)


def _dot_nt(a, b):
    return lax.dot_general(a, b, (((1,), (1,)), ((), ())), preferred_element_type=F32)


def _dot_tn(a, b):
    return lax.dot_general(a, b, (((0,), (0,)), ((), ())), preferred_element_type=F32)


def _dot_exact(a, b):
    return jnp.dot(a, b, preferred_element_type=F32, precision=HIGHEST)


def _two_way_rows(S, BT, prefs):
    tm = _pick(math.gcd(S, BT), prefs)
    n_prompt = S // tm
    prow = lambda i: jnp.minimum(i, n_prompt - 1)
    srow = lambda i: jnp.maximum(i - n_prompt, 0)
    return tm, n_prompt, BT // tm, prow, srow


def _in_proj_kernel(xp_ref, xs_ref, nw_ref, wa_ref, wb_ref, wgc_ref, wgr_ref,
                    z_ref, kp_ref, ks_ref, vp_ref, vs_ref, gc_ref, gr_ref, xn_ref, *, n_prompt, n_a, n_z1, nkt):
    i = pl.program_id(0)
    j = pl.program_id(1)
    is_p = i < n_prompt
    is_s = jnp.logical_not(is_p)

    def norm(x_ref):
        x = x_ref[...]
        ms = jnp.mean(x * x, axis=-1, keepdims=True)
        xn = (x * lax.rsqrt(ms + EPS) * nw_ref[...]).astype(BF16)
        xn_ref[...] = xn
        gc_ref[...] = _dot_nt(xn, wgc_ref[...])
        gr_ref[...] = _dot_nt(wgr_ref[...], xn)

    @pl.when((j == 0) & is_p)
    def _():
        norm(xp_ref)

    @pl.when((j == 0) & is_s)
    def _():
        norm(xs_ref)

    in_k = (j >= n_z1) & (j < n_z1 + nkt)
    in_v = (j >= n_z1 + nkt) & (j < n_z1 + 2 * nkt)
    in_zb = (j >= n_a) & jnp.logical_not(in_k | in_v)
    for cond, w_ref, o_ref in ((j < n_a, wa_ref, z_ref), (in_zb, wb_ref, z_ref),
                               (in_k & is_p, wb_ref, kp_ref), (in_k & is_s, wb_ref, ks_ref),
                               (in_v & is_p, wb_ref, vp_ref), (in_v & is_s, wb_ref, vs_ref)):
        @pl.when(cond)
        def _(w_ref=w_ref, o_ref=o_ref):
            o_ref[...] = _dot_nt(xn_ref[...], w_ref[...])


def _in_proj(xp, xs, norm_w, w_all, w_rest, w_gate_col, w_gate_row):
    S, D = xp.shape
    BT = xs.shape[0]
    tm, n_prompt, n_sample, prow, srow = _two_way_rows(S, BT, (1024, 512, 256, 128))
    tn = _pick(math.gcd(D, QK_D), (512, 256, 128))
    n_a = ZC_QD // tn
    n_z1 = ZC_GA // tn
    nkt = QK_D // tn
    n_tiles = n_a + w_rest.shape[0] // tn
    nzt = n_tiles - 2 * nkt
    zcol = lambda j: jnp.where(j < n_z1, j, jnp.where(j < n_z1 + 2 * nkt, n_z1 - 1, j - 2 * nkt))
    kcol = lambda j: jnp.clip(j - n_z1, 0, nkt - 1)
    vcol = lambda j: jnp.clip(j - n_z1 - nkt, 0, nkt - 1)
    p_spec = lambda col: pl.BlockSpec((tm, tn), lambda i, j: (prow(i), jnp.where(i < n_prompt, col(j), nkt - 1)))
    s_spec = lambda col: pl.BlockSpec((tm, tn), lambda i, j: (srow(i), jnp.where(i < n_prompt, 0, col(j))))
    return pl.pallas_call(
        functools.partial(_in_proj_kernel, n_prompt=n_prompt, n_a=n_a, n_z1=n_z1, nkt=nkt),
        grid=(n_prompt + n_sample, n_tiles),
        in_specs=[
            pl.BlockSpec((tm, D), lambda i, j: (prow(i), 0), pipeline_mode=pl.Buffered(1)),
            pl.BlockSpec((tm, D), lambda i, j: (srow(i), 0), pipeline_mode=pl.Buffered(1)),
            pl.BlockSpec((1, D), lambda i, j: (0, 0)),
            pl.BlockSpec((tn, D), lambda i, j: (jnp.minimum(j, n_a - 1), 0)),
            pl.BlockSpec((tn, D), lambda i, j: (jnp.maximum(j - n_a, 0), 0)),
            pl.BlockSpec((2 * LANES, D), lambda i, j: (0, 0)),
            pl.BlockSpec((2 * SUBLANES, D), lambda i, j: (0, 0)),
        ],
        out_specs=[
            pl.BlockSpec((tm, tn), lambda i, j: (i, zcol(j))),
            p_spec(kcol), s_spec(kcol), p_spec(vcol), s_spec(vcol),
            pl.BlockSpec((tm, 2 * LANES), lambda i, j: (i, 0)),
            pl.BlockSpec((2 * SUBLANES, tm), lambda i, j: (0, i)),
        ],
        out_shape=[
            jax.ShapeDtypeStruct((S + BT, nzt * tn), F32),
            jax.ShapeDtypeStruct((S, QK_D), F32),
            jax.ShapeDtypeStruct((BT, QK_D), F32),
            jax.ShapeDtypeStruct((S, W_D), F32),
            jax.ShapeDtypeStruct((BT, W_D), F32),
            jax.ShapeDtypeStruct((S + BT, 2 * LANES), F32),
            jax.ShapeDtypeStruct((2 * SUBLANES, S + BT), F32),
        ],
        scratch_shapes=[pltpu.VMEM((tm, D), BF16)],
        compiler_params=_params("arbitrary", "arbitrary"),
        name="in_proj",
    )(xp, xs, norm_w, w_all, w_rest, w_gate_col, w_gate_row)


def _conv_silu(x, hist, hist_shift, cw_ref, cb_ref, row_in_seq):
    acc = cb_ref[...] + cw_ref[CONV_W - 1:CONV_W, :] * x
    for j in range(1, CONV_W):
        xr = pltpu.roll(x, j, axis=0)
        hr = pltpu.roll(hist, (j + hist_shift) % hist.shape[0], axis=0)
        if hist.shape[0] != x.shape[0]:
            first = jnp.where(row_in_seq[0:SUBLANES] < j, hr, xr[0:SUBLANES])
            xs = jnp.concatenate([first, xr[SUBLANES:]], axis=0)
        else:
            xs = jnp.where(row_in_seq < j, hr, xr)
        acc = acc + cw_ref[CONV_W - 1 - j:CONV_W - j, :] * xs
    return acc * _sigmoid(acc)


def _mlstm_intra(qb, kb, vb, mask, bt_c, bt_r, ig_r, inter_c):
    dlog = jnp.where(mask, bt_c - bt_r + ig_r, -jnp.inf)
    m_t = jnp.maximum(inter_c, jnp.max(dlog, axis=1, keepdims=True))
    dw = jnp.exp(dlog - m_t)
    iw = jnp.exp(inter_c - m_t)
    s = _dot_nt(qb, kb) * dw
    sv = _dot(s.astype(BF16), vb)
    return sv, jnp.sum(s, axis=1, keepdims=True), m_t, iw


def _head_out(num, den, m_t, o, hw_ref):
    den = jnp.maximum(jnp.abs(den), jnp.exp(-m_t))
    h = num / den
    hn = h * lax.rsqrt(jnp.mean(h * h, axis=-1, keepdims=True) + EPS) * hw_ref[...]
    return (hn * _sigmoid(o)).astype(BF16)


def _mlstm_prompt_kernel(zq_ref, zv_ref, zo_ref, gc_ref, gr_ref, cw_ref, cb_ref, bc_ref, br_ref, hw_ref,
                         h_ref, c_out, n_out, m_out, conv_out,
                         c_s, n_s, m_s, hist_s):
    c = pl.program_id(0)
    L = zq_ref.shape[0]

    @pl.when(c == 0)
    def _():
        c_s[...] = jnp.zeros_like(c_s)
        n_s[...] = jnp.zeros_like(n_s)
        m_s[...] = jnp.zeros_like(m_s)
        hist_s[...] = jnp.zeros_like(hist_s)

    x = zq_ref[...]
    row = lax.broadcasted_iota(jnp.int32, (L, 1), 0)
    qk = _conv_silu(x, hist_s[...], 0, cw_ref, cb_ref, row)
    hist_s[...] = x[L - SUBLANES:L, :]
    conv_out[...] = x[L - SUBLANES:L, :]

    gcol = gc_ref[...] + bc_ref[...]
    grow = gr_ref[...] + br_ref[...]
    ig_c = gcol[:, 0:LANES]
    lf_c = _log_sigmoid(gcol[:, LANES:2 * LANES])
    ig_r = grow[0:SUBLANES, :]
    lf_r = _log_sigmoid(grow[SUBLANES:2 * SUBLANES, :])
    ri = lax.broadcasted_iota(jnp.int32, (L, L), 0)
    ci = lax.broadcasted_iota(jnp.int32, (L, L), 1)
    mask = ci <= ri
    bt_c = _dot_exact(mask.astype(F32), lf_c)
    bt_r = _dot_exact(lf_r, (ri <= ci).astype(F32))
    m_prev = m_s[...]
    inter = bt_c + m_prev
    b_last = bt_c[L - 1:L, :]
    wlog = b_last - bt_c + ig_c
    m_new = jnp.maximum(b_last + m_prev, jnp.max(wlog, axis=0, keepdims=True))
    ws = jnp.exp(wlog - m_new)
    decay = jnp.exp(b_last + m_prev - m_new)
    m_s[...] = m_new
    m_out[...] = m_new

    for h in range(H_M):
        q = qk[:, h * DK_M:(h + 1) * DK_M]
        k = qk[:, QK_M + h * DK_M:QK_M + (h + 1) * DK_M] * (DK_M ** -0.5)
        v = zv_ref[:, h * DV_M:(h + 1) * DV_M]
        qb, kb, vb = q.astype(BF16), k.astype(BF16), v.astype(BF16)
        sv, ssum, m_t, iw = _mlstm_intra(qb, kb, vb, mask, bt_c[:, h:h + 1], bt_r[h:h + 1, :],
                                         ig_r[h:h + 1, :], inter[:, h:h + 1])
        C = c_s[h]
        n_row = n_s[h:h + 1, :]
        num = sv + iw * _dot_nt(qb, C.astype(BF16))
        den = ssum + iw * jnp.sum(q * n_row, axis=1, keepdims=True)
        h_ref[:, h * DV_M:(h + 1) * DV_M] = _head_out(num, den, m_t, zo_ref[:, h * DV_M:(h + 1) * DV_M], hw_ref)
        ws_h = ws[:, h:h + 1]
        dc = decay[:, h:h + 1]
        c_new = dc * C + _dot_tn((v * ws_h).astype(BF16), kb)
        n_new = dc * n_row + jnp.sum(ws_h * k, axis=0, keepdims=True)
        c_s[h] = c_new
        n_s[h:h + 1, :] = n_new
        c_out[h] = c_new
        n_out[h:h + 1, :] = n_new


def _mlstm_prompt(z, gc, gr, conv_w, conv_b, bias_c, bias_r, hnorm_w, S):
    L = _pick(S, (256, 128))
    nz = lambda col, width: col // width
    return pl.pallas_call(
        _mlstm_prompt_kernel,
        grid=(S // L,),
        in_specs=[
            pl.BlockSpec((L, 2 * QK_M), lambda c: (c, nz(ZC_QK, 2 * QK_M))),
            pl.BlockSpec((L, W_M), lambda c: (c, nz(ZC_VM, W_M))),
            pl.BlockSpec((L, W_M), lambda c: (c, nz(ZC_OM, W_M))),
            pl.BlockSpec((L, 2 * LANES), lambda c: (c, 0)),
            pl.BlockSpec((2 * SUBLANES, L), lambda c: (0, c)),
            pl.BlockSpec((CONV_W, 2 * QK_M), lambda c: (0, 0)),
            pl.BlockSpec((1, 2 * QK_M), lambda c: (0, 0)),
            pl.BlockSpec((1, 2 * LANES), lambda c: (0, 0)),
            pl.BlockSpec((2 * SUBLANES, 1), lambda c: (0, 0)),
            pl.BlockSpec((1, DV_M), lambda c: (0, 0)),
        ],
        out_specs=[
            pl.BlockSpec((L, W_M), lambda c: (c, 0)),
            pl.BlockSpec((H_M, DV_M, DK_M), lambda c: (0, 0, 0)),
            pl.BlockSpec((H_M, DK_M), lambda c: (0, 0)),
            pl.BlockSpec((1, LANES), lambda c: (0, 0)),
            pl.BlockSpec((SUBLANES, 2 * QK_M), lambda c: (0, 0)),
        ],
        out_shape=[
            jax.ShapeDtypeStruct((S, W_M), BF16),
            jax.ShapeDtypeStruct((H_M, DV_M, DK_M), F32),
            jax.ShapeDtypeStruct((H_M, DK_M), F32),
            jax.ShapeDtypeStruct((1, LANES), F32),
            jax.ShapeDtypeStruct((SUBLANES, 2 * QK_M), F32),
        ],
        scratch_shapes=[
            pltpu.VMEM((H_M, DV_M, DK_M), F32),
            pltpu.VMEM((H_M, DK_M), F32),
            pltpu.VMEM((1, LANES), F32),
            pltpu.VMEM((SUBLANES, 2 * QK_M), F32),
        ],
        compiler_params=_params("arbitrary"),
        name="mlstm_prompt",
    )(z, z, z, gc, gr, conv_w, conv_b, bias_c, bias_r, hnorm_w)


def _mlstm_sample_kernel(zq_ref, zv_ref, zo_ref, gc_ref, gr_ref, hist_ref, c0_ref, n0_ref, m0_ref,
                         cw_ref, cb_ref, bc_ref, br_ref, hw_ref,
                         h_ref, c_out, n_out, m_out, *, T):
    L = zq_ref.shape[0]
    NB = L // T
    x = zq_ref[...]
    ri = lax.broadcasted_iota(jnp.int32, (L, L), 0)
    ci = lax.broadcasted_iota(jnp.int32, (L, L), 1)
    same = (ri // T) == (ci // T)
    mask = same & (ci <= ri)
    row_t = lax.broadcasted_iota(jnp.int32, (L, 1), 0) % T
    qk = _conv_silu(x, hist_ref[...], L - T, cw_ref, cb_ref, row_t)

    gcol = gc_ref[...] + bc_ref[...]
    grow = gr_ref[...] + br_ref[...]
    ig_c = gcol[:, 0:LANES]
    lf_c = _log_sigmoid(gcol[:, LANES:2 * LANES])
    ig_r = grow[0:SUBLANES, :]
    lf_r = _log_sigmoid(grow[SUBLANES:2 * SUBLANES, :])
    bt_c = _dot_exact(mask.astype(F32), lf_c)
    bt_r = _dot_exact(lf_r, (same & (ri <= ci)).astype(F32))
    m_prev = m0_ref[...]
    last = same & (ci % T == T - 1)
    b_last = _dot_exact(last.astype(F32), bt_c)
    inter = bt_c + m_prev
    wlog = b_last - bt_c + ig_c
    wmax = jnp.max(wlog.reshape(NB, T, LANES), axis=1, keepdims=True)
    wmax = jnp.broadcast_to(wmax, (NB, T, LANES)).reshape(L, LANES)
    m_new = jnp.maximum(b_last + m_prev, wmax)
    ws = jnp.exp(wlog - m_new)
    decay = jnp.exp(b_last + m_prev - m_new)
    m_out[...] = m_new

    lane_seq = lax.broadcasted_iota(jnp.int32, (L, NB * DV_M), 1) // DV_M
    row_seq = lax.broadcasted_iota(jnp.int32, (L, NB * DV_M), 0) // T
    blockdiag = lane_seq == row_seq

    for h in range(H_M):
        q = qk[:, h * DK_M:(h + 1) * DK_M]
        k = qk[:, QK_M + h * DK_M:QK_M + (h + 1) * DK_M] * (DK_M ** -0.5)
        v = zv_ref[:, h * DV_M:(h + 1) * DV_M]
        qb, kb, vb = q.astype(BF16), k.astype(BF16), v.astype(BF16)
        sv, ssum, m_t, iw = _mlstm_intra(qb, kb, vb, mask, bt_c[:, h:h + 1], bt_r[h:h + 1, :],
                                         ig_r[h:h + 1, :], inter[:, h:h + 1])
        C = c0_ref[:, h]
        c_flat = C.reshape(NB * DV_M, DK_M)
        qc_all = _dot_nt(qb, c_flat.astype(BF16))
        qc = jnp.concatenate([qc_all[b * T:(b + 1) * T, b * DV_M:(b + 1) * DV_M] for b in range(NB)], axis=0)
        n_rows = jnp.broadcast_to(n0_ref[:, h:h + 1, :], (NB, T, DK_M)).reshape(L, DK_M)
        num = sv + iw * qc
        den = ssum + iw * jnp.sum(q * n_rows, axis=1, keepdims=True)
        h_ref[:, h * DV_M:(h + 1) * DV_M] = _head_out(num, den, m_t, zo_ref[:, h * DV_M:(h + 1) * DV_M], hw_ref)
        ws_h = ws[:, h:h + 1]
        vw = v * ws_h
        vw_exp = jnp.where(blockdiag, jnp.concatenate([vw] * NB, axis=1), 0.0).astype(BF16)
        upd = _dot_tn(vw_exp, kb).reshape(NB, DV_M, DK_M)
        dc = decay[:, h:h + 1].reshape(NB, T, 1)[:, 0:1, :]
        c_out[:, h] = dc * C + upd
        kw = (ws_h * k).reshape(NB, T, DK_M)
        n_out[:, h:h + 1, :] = dc * n0_ref[:, h:h + 1, :] + jnp.sum(kw, axis=1, keepdims=True)


def _mlstm_sample(z, gc, gr, hist, c0, n0, m0p, conv_w, conv_b, bias_c, bias_r, hnorm_w, S, B, T):
    NB = SEQ_BLOCK
    L = NB * T
    assert L == LANES and B % NB == 0 and S % L == 0
    r0 = S // L
    nz = lambda col, width: col // width
    return pl.pallas_call(
        functools.partial(_mlstm_sample_kernel, T=T),
        grid=(B // NB,),
        in_specs=[
            pl.BlockSpec((L, 2 * QK_M), lambda i: (r0 + i, nz(ZC_QK, 2 * QK_M))),
            pl.BlockSpec((L, W_M), lambda i: (r0 + i, nz(ZC_VM, W_M))),
            pl.BlockSpec((L, W_M), lambda i: (r0 + i, nz(ZC_OM, W_M))),
            pl.BlockSpec((L, 2 * LANES), lambda i: (r0 + i, 0)),
            pl.BlockSpec((2 * SUBLANES, L), lambda i: (0, r0 + i)),
            pl.BlockSpec((L, 2 * QK_M), lambda i: (i, 0)),
            pl.BlockSpec((NB, H_M, DV_M, DK_M), lambda i: (i, 0, 0, 0)),
            pl.BlockSpec((NB, H_M, DK_M), lambda i: (i, 0, 0)),
            pl.BlockSpec((L, LANES), lambda i: (i, 0)),
            pl.BlockSpec((CONV_W, 2 * QK_M), lambda i: (0, 0)),
            pl.BlockSpec((1, 2 * QK_M), lambda i: (0, 0)),
            pl.BlockSpec((1, 2 * LANES), lambda i: (0, 0)),
            pl.BlockSpec((2 * SUBLANES, 1), lambda i: (0, 0)),
            pl.BlockSpec((1, DV_M), lambda i: (0, 0)),
        ],
        out_specs=[
            pl.BlockSpec((L, W_M), lambda i: (i, 0)),
            pl.BlockSpec((NB, H_M, DV_M, DK_M), lambda i: (i, 0, 0, 0)),
            pl.BlockSpec((NB, H_M, DK_M), lambda i: (i, 0, 0)),
            pl.BlockSpec((L, LANES), lambda i: (i, 0)),
        ],
        out_shape=[
            jax.ShapeDtypeStruct((B * T, W_M), BF16),
            jax.ShapeDtypeStruct((B, H_M, DV_M, DK_M), F32),
            jax.ShapeDtypeStruct((B, H_M, DK_M), F32),
            jax.ShapeDtypeStruct((B * T, LANES), F32),
        ],
        compiler_params=_params("parallel"),
        name="mlstm_sample",
    )(z, z, z, gc, gr, hist, c0, n0, m0p, conv_w, conv_b, bias_c, bias_r, hnorm_w)


def _lambda(lq1, lk1, lq2, lk2):
    a = jnp.sum(lq1[...] * lk1[...], axis=-1, keepdims=True)
    b = jnp.sum(lq2[...] * lk2[...], axis=-1, keepdims=True)
    return jnp.exp(a) - jnp.exp(b) + LAM_INIT


def _subln(att, w_ref):
    y = att * lax.rsqrt(jnp.mean(att * att, axis=-1, keepdims=True) + EPS) * w_ref[...]
    return (y * (1.0 - LAM_INIT)).astype(BF16)


def _attn_prompt_body(qi, q_ref, k_ref, v_ref, b0_ref, b1_ref, lq1, lk1, lq2, lk2, sw_ref,
                      o_ref, kb_s, vt_s, m_s, acc_s):
    T = q_ref.shape[0]
    n_tiles = kb_s.shape[0]

    @pl.when(qi == 0)
    def _():
        for t in range(n_tiles):
            kb_s[t] = k_ref[t * T:(t + 1) * T, :].astype(BF16)
            vt_s[t, 0:DV_D, :] = v_ref[t * T:(t + 1) * T, :].T.astype(BF16)
            vt_s[t, DV_D:, :] = jnp.ones((ONES_ROWS, T), BF16)

    q = q_ref[...] * (DK_D ** -0.5 * LOG2E)
    lane = lax.broadcasted_iota(jnp.int32, q.shape, 1)
    qpad = (jnp.where(lane < DK_D, q, 0.0).astype(BF16), jnp.where(lane >= DK_D, q, 0.0).astype(BF16))

    def group(tiles, state):
        scores = []
        for kj, bias in tiles:
            kt = kb_s[kj]
            for c in range(2):
                s = _dot_nt(kt, qpad[c])
                scores.append(s if bias is None else s + bias)
        parts = ([], [])
        for t, (kj, _) in enumerate(tiles):
            vt = vt_s[kj]
            for c in range(2):
                s = scores[2 * t + c]
                m = jnp.max(s, axis=0, keepdims=True)
                parts[c].append((m, _dot(vt, jnp.exp2(s - m).astype(BF16))))
        out = []
        for c in range(2):
            m_old, acc_old = state[c]
            m_new = m_old
            for m, _ in parts[c]:
                m_new = jnp.maximum(m_new, m)
            acc_new = jnp.exp2(m_old - m_new) * acc_old
            for m, pv in parts[c]:
                acc_new = acc_new + jnp.exp2(m - m_new) * pv
            out.append((m_new, acc_new))
        return tuple(out)

    def load():
        return tuple((m_s[c], acc_s[c]) for c in range(2))

    def store(state):
        for c in range(2):
            m_s[c], acc_s[c] = state[c]

    n_all = qi + 1
    n_head = functools.reduce(lambda acc, G: jnp.where(n_all >= G, jnp.maximum(acc, G), acc), HEAD_GROUPS, 0)
    n_far = n_all - n_head
    state = tuple((jnp.full((1, T), -jnp.inf, F32), jnp.zeros((DV_D + ONES_ROWS, T), F32)) for _ in range(2))
    done = 0
    for G in FAR_GROUPS:
        n_grp = (n_far - done) // G
        state = lax.fori_loop(0, n_grp, lambda g, st, G=G, done=done: group(
            [(done + g * G + t, None) for t in range(G)], st), state)
        done = done + n_grp * G
    store(state)

    for G in HEAD_GROUPS:
        @pl.when(n_head == G)
        def _(G=G):
            bias = [None] * (G - 2) + [b1_ref[0], b0_ref[0]]
            store(group([(qi - (G - 1) + t, bias[-G:][t]) for t in range(G)], load()))

    (_, a0), (_, a1) = load()
    lam = _lambda(lq1, lk1, lq2, lk2)
    att_t = a0[0:DV_D] / a0[DV_D:DV_D + 1] - lam * (a1[0:DV_D] / a1[DV_D:DV_D + 1])
    o_ref[...] = _subln(att_t.T, sw_ref)


def _attn_prompt_kernel(*refs):
    _attn_prompt_body(pl.program_id(1), *refs)


def _attn_prompt_specs(S, T, extra=()):
    hw = 2 * DK_D
    in_specs = [
        pl.BlockSpec((T, hw), lambda h, i, *_: (i, ZC_QD // hw + h)),
        pl.BlockSpec((S, hw), lambda h, i, *_: (0, h)),
        pl.BlockSpec((S, DV_D), lambda h, i, *_: (0, h)),
        pl.BlockSpec((1, T, T), lambda h, i, *_: (h, 0, 0)),
        pl.BlockSpec((1, T, T), lambda h, i, *_: (h, 0, 0)),
    ]
    out_spec = pl.BlockSpec((T, DV_D), lambda h, i, *_: (i, h))
    scratch = [
        pltpu.VMEM((S // T, T, hw), BF16),
        pltpu.VMEM((S // T, DV_D + ONES_ROWS, T), BF16),
        pltpu.VMEM((2, 1, T), F32),
        pltpu.VMEM((2, DV_D + ONES_ROWS, T), F32),
    ]
    return in_specs, out_spec, scratch


def _small_specs(shapes):
    return [pl.BlockSpec(shape, lambda *_, n=len(shape): (0,) * n) for shape in shapes]


_LAMBDA_AND_SUBLN = [(1, DK_D)] * 4 + [(1, DV_D)]


def _attn_prompt(z, kp, vp, bias0, bias1, lq1, lk1, lq2, lk2, subln_w, S):
    T = bias0.shape[-1]
    in_specs, out_spec, scratch = _attn_prompt_specs(S, T)
    return pl.pallas_call(
        _attn_prompt_kernel,
        grid=(H_D, S // T),
        in_specs=in_specs + _small_specs(_LAMBDA_AND_SUBLN),
        out_specs=out_spec,
        out_shape=jax.ShapeDtypeStruct((S, W_D), BF16),
        scratch_shapes=scratch,
        compiler_params=_params("arbitrary", "arbitrary"),
        name="attn_prompt",
    )(z, kp, vp, bias0, bias1, lq1, lk1, lq2, lk2, subln_w)


def _head_rows(ref, h):
    n_keys, n_heads, width = ref.shape
    return ref.reshape(n_keys * n_heads, width)[pl.ds(h, n_keys, stride=n_heads), :]


def _attn_sample_body(p, n_steps, q_ref, *refs, pps):
    kc = refs[0:pps]
    vc = refs[pps:2 * pps]
    (kn_ref, vn_ref, blast_ref, bnew_ref, lq1, lk1, lq2, lk2, sw_ref,
     o_ref, qbd_s, m_s, acc_s) = refs[2 * pps:]
    last_step = p == n_steps - 1
    T = q_ref.shape[2]
    PG = kc[0].shape[0]

    @pl.when(p == 0)
    def _():
        lane = lax.broadcasted_iota(jnp.int32, (T, 2 * DK_D), 1)
        for h in range(H_D):
            q = q_ref[0, h] * (DK_D ** -0.5 * LOG2E)
            qbd_s[h, 0:T, :] = jnp.where(lane < DK_D, q, 0.0).astype(BF16)
            qbd_s[h, T:2 * T, :] = jnp.where(lane >= DK_D, q, 0.0).astype(BF16)
        m_s[...] = jnp.full_like(m_s, -jnp.inf)
        acc_s[...] = jnp.zeros_like(acc_s)

    def update(keys, values, biases):
        scores = []
        for h in range(H_D):
            s = _dot_nt(qbd_s[h], keys[h])
            scores.append(s if biases[h] is None else s + biases[h])
        for h in range(H_D):
            s = scores[h]
            m_old = m_s[h]
            m_new = jnp.maximum(m_old, jnp.max(s, axis=1, keepdims=True))
            pr = jnp.exp2(s - m_new).astype(BF16)
            v = values[h]
            v_ext = jnp.concatenate([v.astype(BF16), jnp.ones(v.shape, BF16)], axis=1)
            acc_s[h] = jnp.exp2(m_old - m_new) * acc_s[h] + _dot(pr, v_ext)
            m_s[h] = m_new

    pad = jnp.zeros((2 * T, (pps - 1) * PG), F32)
    keys, values, biases = [], [], []
    for h in range(H_D):
        keys.append(jnp.concatenate([_head_rows(kc[i], h) for i in range(pps)], axis=0).astype(BF16))
        values.append(jnp.concatenate([_head_rows(vc[i], h) for i in range(pps)], axis=0))
        b_end = jnp.where(last_step, blast_ref[h], 0.0)
        biases.append(jnp.concatenate([pad, b_end], axis=1) if pps > 1 else b_end)
    update(keys, values, biases)

    @pl.when(last_step)
    def _():
        zeros = jnp.zeros((PG - T, 2 * DK_D), F32)
        update([jnp.concatenate([kn_ref[0, h], zeros], axis=0).astype(BF16) for h in range(H_D)],
               [jnp.concatenate([vn_ref[0, h], zeros], axis=0) for h in range(H_D)],
               [bnew_ref[h] for h in range(H_D)])
        lam = _lambda(lq1, lk1, lq2, lk2)
        for h in range(H_D):
            r = acc_s[h, :, 0:DV_D] / acc_s[h, :, DV_D:DV_D + 1]
            o_ref[0, h] = _subln(r[0:T] - lam * r[T:2 * T], sw_ref)


def _attn_sample_kernel(pt_ref, *refs, pps):
    _attn_sample_body(pl.program_id(1), pl.num_programs(1), *refs, pps=pps)


def _attn_sample_specs(qs, blast, bnew, PG, pps, seq_step):
    B, _, T, _ = qs.shape

    def per_seq(width):
        return pl.BlockSpec((1, H_D, T, width), lambda *g: (seq_step(*g[:-1])[0], 0, 0, 0))

    def page_spec(i, width):
        def index(*g):
            b, p = seq_step(*g[:-1])
            return (0, g[-1][b, p * pps + i], 0, 0, 0)
        return pl.BlockSpec((None, None, PG, H_D, width), index)

    in_specs = ([per_seq(2 * DK_D)]
                + [page_spec(i, 2 * DK_D) for i in range(pps)]
                + [page_spec(i, DV_D) for i in range(pps)]
                + [per_seq(2 * DK_D), per_seq(DV_D)]
                + _small_specs([blast.shape, bnew.shape]))
    scratch = [
        pltpu.VMEM((H_D, 2 * T, 2 * DK_D), BF16),
        pltpu.VMEM((H_D, 2 * T, 1), F32),
        pltpu.VMEM((H_D, 2 * T, 2 * DV_D), F32),
    ]
    return in_specs, per_seq(DV_D), scratch


def _attn_sample(page_table, qs, cache_k, cache_v, kn, vn, blast, bnew, lq1, lk1, lq2, lk2, subln_w, pps):
    B, _, T, _ = qs.shape
    n_pages = page_table.shape[1]
    in_specs, out_spec, scratch = _attn_sample_specs(qs, blast, bnew, cache_k.shape[2], pps, lambda b, p: (b, p))
    grid_spec = pltpu.PrefetchScalarGridSpec(
        num_scalar_prefetch=1,
        grid=(B, n_pages // pps),
        in_specs=in_specs + _small_specs(_LAMBDA_AND_SUBLN),
        out_specs=out_spec,
        scratch_shapes=scratch,
    )
    return pl.pallas_call(
        functools.partial(_attn_sample_kernel, pps=pps),
        grid_spec=grid_spec,
        out_shape=jax.ShapeDtypeStruct((B, H_D, T, DV_D), BF16),
        compiler_params=_params("arbitrary", "arbitrary"),
        name="attn_sample",
    )(page_table, qs, *([cache_k] * pps), *([cache_v] * pps), kn, vn, blast, bnew,
      lq1, lk1, lq2, lk2, subln_w)


def _attn_fused_kernel(pt_ref, *refs, pps, n_sp, n_sample_in):
    n_prompt_in = 5
    p_in = refs[0:n_prompt_in]
    s_in = refs[n_prompt_in:n_prompt_in + n_sample_in]
    shared = refs[n_prompt_in + n_sample_in:n_prompt_in + n_sample_in + 5]
    o_p, o_s = refs[n_prompt_in + n_sample_in + 5:n_prompt_in + n_sample_in + 7]
    scratch = refs[n_prompt_in + n_sample_in + 7:]
    qi = pl.program_id(1)
    sid = pl.program_id(0) * pl.num_programs(1) + qi
    _attn_prompt_body(qi, *p_in, *shared, o_p, *scratch[0:4])
    _attn_sample_body(sid % n_sp, n_sp, *s_in, *shared, o_s, *scratch[4:], pps=pps)


def _attn_fused(page_table, z, kp, vp, bias0, bias1, qs, cache_k, cache_v, kn, vn, blast, bnew,
                lq1, lk1, lq2, lk2, subln_w, S, pps):
    T = bias0.shape[-1]
    B, _, TS, _ = qs.shape
    nq = S // T
    n_sp = page_table.shape[1] // pps
    assert H_D * nq == B * n_sp
    p_specs, p_out, p_scratch = _attn_prompt_specs(S, T)
    seq_step = lambda h, i: ((h * nq + i) // n_sp, (h * nq + i) % n_sp)
    s_specs, s_out, s_scratch = _attn_sample_specs(qs, blast, bnew, cache_k.shape[2], pps, seq_step)
    grid_spec = pltpu.PrefetchScalarGridSpec(
        num_scalar_prefetch=1,
        grid=(H_D, nq),
        in_specs=p_specs + s_specs + _small_specs(_LAMBDA_AND_SUBLN),
        out_specs=[p_out, s_out],
        scratch_shapes=p_scratch + s_scratch,
    )
    return pl.pallas_call(
        functools.partial(_attn_fused_kernel, pps=pps, n_sp=n_sp, n_sample_in=len(s_specs)),
        grid_spec=grid_spec,
        out_shape=[jax.ShapeDtypeStruct((S, W_D), BF16), jax.ShapeDtypeStruct((B, H_D, TS, DV_D), BF16)],
        compiler_params=_params("arbitrary", "arbitrary"),
        name="attn_fused",
    )(page_table, z, kp, vp, bias0, bias1, qs, *([cache_k] * pps), *([cache_v] * pps), kn, vn, blast, bnew,
      lq1, lk1, lq2, lk2, subln_w)


def _merge_kernel(hmp_ref, hms_ref, atp_ref, ats_ref, wa_ref, wb_ref, ga_ref, gb_ref, u_ref, *, n_prompt):
    def body(hm_ref, at_ref):
        ya = _dot(hm_ref[...], wa_ref[...])
        yb = _dot(at_ref[...], wb_ref[...])
        u_ref[...] = (_sigmoid(ga_ref[...]) * ya + _sigmoid(gb_ref[...]) * yb).astype(BF16)

    @pl.when(pl.program_id(0) < n_prompt)
    def _():
        body(hmp_ref, atp_ref)

    @pl.when(pl.program_id(0) >= n_prompt)
    def _():
        body(hms_ref, ats_ref)


def _merge(hm_p, hm_s, att_p, att_s, w_a, w_b, z, D):
    S, BT = hm_p.shape[0], hm_s.shape[0]
    tm, n_prompt, n_sample, prow, srow = _two_way_rows(S, BT, (1024, 512, 256, 128))
    tn = _pick(D, (1024, 512, 256, 128))
    ga0, gb0 = ZC_GA // tn, (ZC_GA + D) // tn
    return pl.pallas_call(
        functools.partial(_merge_kernel, n_prompt=n_prompt),
        grid=(n_prompt + n_sample, D // tn),
        in_specs=[
            pl.BlockSpec((tm, W_M), lambda i, j: (prow(i), 0)),
            pl.BlockSpec((tm, W_M), lambda i, j: (srow(i), 0)),
            pl.BlockSpec((tm, W_D), lambda i, j: (prow(i), 0)),
            pl.BlockSpec((tm, W_D), lambda i, j: (srow(i), 0)),
            pl.BlockSpec((W_M, tn), lambda i, j: (0, j)),
            pl.BlockSpec((W_D, tn), lambda i, j: (0, j)),
            pl.BlockSpec((tm, tn), lambda i, j: (i, ga0 + j)),
            pl.BlockSpec((tm, tn), lambda i, j: (i, gb0 + j)),
        ],
        out_specs=pl.BlockSpec((tm, tn), lambda i, j: (i, j)),
        out_shape=jax.ShapeDtypeStruct((S + BT, D), BF16),
        compiler_params=_params("parallel", "parallel"),
        name="merge",
    )(hm_p, hm_s, att_p, att_s, w_a, w_b, z, z)


def _out_proj_kernel(u_ref, w_ref, xp_ref, xs_ref, o_ref, *, n_prompt):
    y = _dot(u_ref[...], w_ref[...])

    @pl.when(pl.program_id(0) < n_prompt)
    def _():
        o_ref[...] = xp_ref[...] + y

    @pl.when(pl.program_id(0) >= n_prompt)
    def _():
        o_ref[...] = xs_ref[...] + y


def _out_proj(u, w_out, xp, xs):
    S, D = xp.shape
    BT = xs.shape[0]
    tm, n_prompt, n_sample, prow, srow = _two_way_rows(S, BT, (1024, 512, 256, 128))
    tn = _pick(D, (1024, 512, 256, 128))
    return pl.pallas_call(
        functools.partial(_out_proj_kernel, n_prompt=n_prompt),
        grid=(n_prompt + n_sample, D // tn),
        in_specs=[
            pl.BlockSpec((tm, D), lambda i, j: (i, 0)),
            pl.BlockSpec((D, tn), lambda i, j: (0, j)),
            pl.BlockSpec((tm, tn), lambda i, j: (prow(i), jnp.where(i < n_prompt, j, D // tn - 1))),
            pl.BlockSpec((tm, tn), lambda i, j: (srow(i), jnp.where(i < n_prompt, 0, j))),
        ],
        out_specs=pl.BlockSpec((tm, tn), lambda i, j: (i, j)),
        out_shape=jax.ShapeDtypeStruct((S + BT, D), F32),
        compiler_params=_params("parallel", "parallel"),
        name="out_proj",
    )(u, w_out, xp, xs)


def _ffn_kernel(x_ref, nw_ref, w1_ref, w2_ref, fw_ref, yp_ref, ys_ref, xn_s, acc_s, *, n_prompt):
    f = pl.program_id(1)

    @pl.when(f == 0)
    def _():
        x = x_ref[...]
        ms = jnp.mean(x * x, axis=-1, keepdims=True)
        xn_s[...] = (x * lax.rsqrt(ms + EPS) * nw_ref[...]).astype(BF16)
        acc_s[...] = jnp.zeros_like(acc_s)

    hid = jnp.maximum(_dot(xn_s[...], w1_ref[...]), 0.0)
    acc_s[...] += _dot((hid * hid).astype(BF16), w2_ref[...])

    def final(y_ref):
        x2 = x_ref[...] + acc_s[...]
        ms = jnp.mean(x2 * x2, axis=-1, keepdims=True)
        y_ref[...] = x2 * lax.rsqrt(ms + EPS) * fw_ref[...]

    last = f == pl.num_programs(1) - 1

    @pl.when(last & (pl.program_id(0) < n_prompt))
    def _():
        final(yp_ref)

    @pl.when(last & (pl.program_id(0) >= n_prompt))
    def _():
        final(ys_ref)


def _ffn(x, norm_w, w1, w2, final_w, S):
    R, D = x.shape
    DF = w1.shape[1]
    tm, n_prompt, n_sample, prow, srow = _two_way_rows(S, R - S, (512, 256, 128))
    tf = _pick(DF, (1024, 512, 256, 128))
    return pl.pallas_call(
        functools.partial(_ffn_kernel, n_prompt=n_prompt),
        grid=(n_prompt + n_sample, DF // tf),
        in_specs=[
            pl.BlockSpec((tm, D), lambda i, f: (i, 0)),
            pl.BlockSpec((1, D), lambda i, f: (0, 0)),
            pl.BlockSpec((D, tf), lambda i, f: (0, f)),
            pl.BlockSpec((tf, D), lambda i, f: (f, 0)),
            pl.BlockSpec((1, D), lambda i, f: (0, 0)),
        ],
        out_specs=[
            pl.BlockSpec((tm, D), lambda i, f: (prow(i), 0)),
            pl.BlockSpec((tm, D), lambda i, f: (srow(i), 0)),
        ],
        out_shape=[jax.ShapeDtypeStruct((S, D), F32), jax.ShapeDtypeStruct((R - S, D), F32)],
        scratch_shapes=[pltpu.VMEM((tm, D), BF16), pltpu.VMEM((tm, D), F32)],
        compiler_params=_params("arbitrary", "arbitrary"),
        name="ffn",
    )(x, norm_w, w1, w2, final_w)


def _bias_by_distance(rel_bias, n):
    d = jnp.arange(n, dtype=jnp.int32)
    max_exact = N_BUCKETS // 2
    nf = jnp.maximum(d, 1).astype(F32)
    large = max_exact + jnp.floor(jnp.log(nf / max_exact) / math.log(MAX_DIST / max_exact)
                                  * (N_BUCKETS - max_exact))
    large = jnp.minimum(large, N_BUCKETS - 1.0)
    bucket = jnp.where(d < max_exact, d.astype(F32), large)
    onehot = (bucket[:, None] == jnp.arange(N_BUCKETS, dtype=F32)[None, :]).astype(F32)
    return jnp.dot(onehot, rel_bias.astype(F32), precision=HIGHEST).T


def _toeplitz(w, rows, cols):
    n = w.shape[-1]
    assert cols <= n - 1
    lead = w.shape[:-1]
    flat = jnp.tile(w, (1,) * len(lead) + (rows,))[..., :rows * (n - 1)]
    return flat.reshape(lead + (rows, n - 1))[..., :cols]


def _prompt_bias_tiles(rel_bias, T):
    assert T + 1 >= MAX_DIST
    bd = _bias_by_distance(rel_bias, 2 * T)
    val = (bd - bd[:, 2 * T - 1:]) * LOG2E
    neg = jnp.full((H_D, T), NEG, F32)
    t0 = _toeplitz(jnp.concatenate([val[:, :T], neg], axis=1), T, T)
    t1 = _toeplitz(jnp.concatenate([val[:, T:], val[:, :T]], axis=1), T, T)
    return t0, t1


def _sample_bias_tables(rel_bias, T, PG):
    assert PG + 1 >= MAX_DIST
    bd = _bias_by_distance(rel_bias, 2 * PG + T)
    bd = (bd - bd[:, 2 * PG + T - 1:]) * LOG2E
    w_last = jnp.concatenate([bd[:, PG:0:-1], bd[:, :1], bd[:, PG + T - 1:PG:-1]], axis=1)
    last = _toeplitz(w_last, T, PG)
    w_new = jnp.concatenate([bd[:, :1], jnp.full((H_D, T), NEG, F32), bd[:, T - 1:0:-1]], axis=1)
    new = jnp.concatenate([_toeplitz(w_new, T, T), jnp.full((H_D, T, PG - T), NEG, F32)], axis=2)
    both_maps = lambda t: jnp.concatenate([t, t], axis=1).astype(F32)
    return both_maps(last), both_maps(new)


def kernel(x_prompt, x_sample, cache_k, cache_v, page_table, state_C, state_n, state_m, state_conv,
           norm1_w, w_in, b_i, b_f, conv_w, conv_b, hnorm_w, lambda_q1, lambda_k1, lambda_q2, lambda_k2,
           subln_w, rel_bias, w_a, w_b, w_out, norm2_w, w_ff1, w_ff2, final_norm_w):
    assert w_in.shape[0] == 1 and x_prompt.shape[0] == 1
    _, S, D = x_prompt.shape
    B, T, _ = x_sample.shape
    PG = cache_k.shape[2]
    xp = x_prompt[0]
    xs = x_sample.reshape(B * T, D)

    o_i = 2 * QK_M + 2 * W_M
    o_qd = o_i + 2 * H_M
    w_t = jnp.transpose(w_in[0])
    w_all = w_t[:o_qd].astype(BF16)
    w_rest = w_t[o_qd:].astype(BF16)
    w_gate_row = w_all[o_i:o_qd]
    gate_pad = jnp.zeros((LANES - H_M, D), BF16)
    w_gate_col = jnp.concatenate([w_gate_row[:H_M], gate_pad, w_gate_row[H_M:], gate_pad], axis=0)
    bias_c = jnp.zeros((1, 2 * LANES), F32).at[0, 0:H_M].set(b_i[0]).at[0, LANES:LANES + H_M].set(b_f[0])
    bias_r = jnp.concatenate([b_i[0], b_f[0]])[:, None]

    z, kp, ks, vp, vs, gc, gr = _in_proj(xp, xs, norm1_w, w_all, w_rest, w_gate_col, w_gate_row)

    hm_p, c_p, n_p, m_p, conv_p = _mlstm_prompt(z, gc, gr, conv_w[0], conv_b, bias_c, bias_r, hnorm_w, S)
    hist = jnp.pad(state_conv[0], ((0, 0), (T - (CONV_W - 1), 0), (0, 0))).reshape(B * T, 2 * QK_M)
    m0p = jnp.repeat(jnp.pad(state_m[0], ((0, 0), (0, LANES - H_M))), T, axis=0)
    hm_s, c_s, n_s, m_s = _mlstm_sample(z, gc, gr, hist, state_C[0], state_n[0], m0p, conv_w[0], conv_b,
                                        bias_c, bias_r, hnorm_w, S, B, T)

    TQ = _pick(S, (256, 128))
    t0, t1 = _prompt_bias_tiles(rel_bias, TQ)
    lq1, lk1, lq2, lk2 = lambda_q1, lambda_k1, lambda_q2, lambda_k2
    blast, bnew = _sample_bias_tables(rel_bias, T, PG)
    per_head = lambda a, width: jnp.transpose(a.reshape(B, T, H_D, width), (0, 2, 1, 3))
    qs = per_head(z[S:, ZC_QD:ZC_QD + QK_D], 2 * DK_D)
    kn = per_head(ks, 2 * DK_D)
    vn = per_head(vs, DV_D)
    n_pages = page_table.shape[1]
    pps = _pick(n_pages, (PAGES_PER_STEP, 4, 2, 1))
    if H_D * (S // TQ) == B * (n_pages // pps):
        att_p, att_s = _attn_fused(page_table, z, kp, vp, t0, t1, qs, cache_k, cache_v, kn, vn, blast, bnew,
                                   lq1, lk1, lq2, lk2, subln_w, S, pps)
    else:
        att_p = _attn_prompt(z, kp, vp, t0, t1, lq1, lk1, lq2, lk2, subln_w, S)
        att_s = _attn_sample(page_table, qs, cache_k, cache_v, kn, vn, blast, bnew,
                             lq1, lk1, lq2, lk2, subln_w, pps)
    att_s = jnp.transpose(att_s, (0, 2, 1, 3))

    u = _merge(hm_p, hm_s, att_p, att_s.reshape(B * T, W_D), w_a[0].astype(BF16), w_b[0].astype(BF16), z, D)
    x1 = _out_proj(u, w_out[0].astype(BF16), xp, xs)
    y_p, y_s = _ffn(x1, norm2_w, w_ff1[0].astype(BF16), w_ff2[0].astype(BF16), final_norm_w[None, :], S)

    conv_prompt = conv_p[SUBLANES - (CONV_W - 1):].reshape(1, 1, CONV_W - 1, 2 * QK_M)
    conv_sample = z[S:, :2 * QK_M].reshape(B, T, 2 * QK_M)[:, T - (CONV_W - 1):][None]
    return (y_p.reshape(1, S, D), y_s.reshape(B, T, D),
            kp.reshape(1, 1, S, H_D, 2 * DK_D), vp.reshape(1, 1, S, H_D, DV_D),
            c_p[None, None], n_p[None, None], m_p[:, :H_M][None], conv_prompt,
            ks.reshape(1, B, T, H_D, 2 * DK_D), vs.reshape(1, B, T, H_D, DV_D),
            c_s[None], n_s[None], m_s[::T, :H_M][None], conv_sample)
```

```python
import functools
import math

import numpy as np
import jax
import jax.numpy as jnp
from jax import lax
from jax.experimental import pallas as pl
from jax.experimental.pallas import tpu as pltpu

F32 = jnp.float32
BF16 = jnp.bfloat16
HIGHEST = lax.Precision.HIGHEST

H_M = 8
DK_M = 128
DV_M = 128
QK_M = H_M * DK_M
W_M = H_M * DV_M
CONV_W = 4
H_D = 8
DK_D = 64
DV_D = 128
QK_D = H_D * 2 * DK_D
W_D = H_D * DV_D
N_BUCKETS = 32
MAX_DIST = 128
EPS = 1e-6
LAM_INIT = 0.8 - 0.6 * math.exp(-0.3 * 0)
NEG = -1e30
LOG2E = math.log2(math.e)

ZC_QK = 0
ZC_VM = 2 * QK_M
ZC_OM = ZC_VM + W_M
ZC_QD = ZC_OM + W_M
ZC_GA = ZC_QD + QK_D

LANES = 128
SUBLANES = 8
VMEM_LIMIT = 56 * 1024 * 1024

SEQ_BLOCK = 16
PAGES_PER_STEP = 8
FAR_GROUPS = (16, 8, 4, 2, 1)
HEAD_GROUPS = (16, 8, 4, 2, 1)
ONES_ROWS = 16


def _params(*sem):
    return pltpu.CompilerParams(dimension_semantics=sem, vmem_limit_bytes=VMEM_LIMIT)


def _pick(n, prefs):
    for p in prefs:
        if n % p == 0:
            return p
    return n


def _sigmoid(x):
    return 1.0 / (1.0 + jnp.exp(-x))


def _log_sigmoid(x):
    return jnp.minimum(x, 0.0) - jnp.log(1.0 + jnp.exp(-jnp.abs(x)))


def _dot(a, b):
    return jnp.dot(a, b, preferred_element_type=F32)


def _dot_nt(a, b):
    return lax.dot_general(a, b, (((1,), (1,)), ((), ())), preferred_element_type=F32)


def _dot_tn(a, b):
    return lax.dot_general(a, b, (((0,), (0,)), ((), ())), preferred_element_type=F32)


def _dot_exact(a, b):
    return jnp.dot(a, b, preferred_element_type=F32, precision=HIGHEST)


def _two_way_rows(S, BT, prefs):
    tm = _pick(math.gcd(S, BT), prefs)
    n_prompt = S // tm
    prow = lambda i: jnp.minimum(i, n_prompt - 1)
    srow = lambda i: jnp.maximum(i - n_prompt, 0)
    return tm, n_prompt, BT // tm, prow, srow


def _in_proj_kernel(xp_ref, xs_ref, nw_ref, wa_ref, wb_ref, wgc_ref, wgr_ref,
                    z_ref, kp_ref, ks_ref, vp_ref, vs_ref, gc_ref, gr_ref, xn_ref, *, n_prompt, n_a, n_z1, nkt):
    i = pl.program_id(0)
    j = pl.program_id(1)
    is_p = i < n_prompt
    is_s = jnp.logical_not(is_p)

    def norm(x_ref):
        x = x_ref[...]
        ms = jnp.mean(x * x, axis=-1, keepdims=True)
        xn = (x * lax.rsqrt(ms + EPS) * nw_ref[...]).astype(BF16)
        xn_ref[...] = xn
        gc_ref[...] = _dot_nt(xn, wgc_ref[...])
        gr_ref[...] = _dot_nt(wgr_ref[...], xn)

    @pl.when((j == 0) & is_p)
    def _():
        norm(xp_ref)

    @pl.when((j == 0) & is_s)
    def _():
        norm(xs_ref)

    in_k = (j >= n_z1) & (j < n_z1 + nkt)
    in_v = (j >= n_z1 + nkt) & (j < n_z1 + 2 * nkt)
    in_zb = (j >= n_a) & jnp.logical_not(in_k | in_v)
    for cond, w_ref, o_ref in ((j < n_a, wa_ref, z_ref), (in_zb, wb_ref, z_ref),
                               (in_k & is_p, wb_ref, kp_ref), (in_k & is_s, wb_ref, ks_ref),
                               (in_v & is_p, wb_ref, vp_ref), (in_v & is_s, wb_ref, vs_ref)):
        @pl.when(cond)
        def _(w_ref=w_ref, o_ref=o_ref):
            o_ref[...] = _dot_nt(xn_ref[...], w_ref[...])


def _in_proj(xp, xs, norm_w, w_all, w_rest, w_gate_col, w_gate_row):
    S, D = xp.shape
    BT = xs.shape[0]
    tm, n_prompt, n_sample, prow, srow = _two_way_rows(S, BT, (1024, 512, 256, 128))
    tn = _pick(math.gcd(D, QK_D), (512, 256, 128))
    n_a = ZC_QD // tn
    n_z1 = ZC_GA // tn
    nkt = QK_D // tn
    n_tiles = n_a + w_rest.shape[0] // tn
    nzt = n_tiles - 2 * nkt
    zcol = lambda j: jnp.where(j < n_z1, j, jnp.where(j < n_z1 + 2 * nkt, n_z1 - 1, j - 2 * nkt))
    kcol = lambda j: jnp.clip(j - n_z1, 0, nkt - 1)
    vcol = lambda j: jnp.clip(j - n_z1 - nkt, 0, nkt - 1)
    p_spec = lambda col: pl.BlockSpec((tm, tn), lambda i, j: (prow(i), jnp.where(i < n_prompt, col(j), nkt - 1)))
    s_spec = lambda col: pl.BlockSpec((tm, tn), lambda i, j: (srow(i), jnp.where(i < n_prompt, 0, col(j))))
    return pl.pallas_call(
        functools.partial(_in_proj_kernel, n_prompt=n_prompt, n_a=n_a, n_z1=n_z1, nkt=nkt),
        grid=(n_prompt + n_sample, n_tiles),
        in_specs=[
            pl.BlockSpec((tm, D), lambda i, j: (prow(i), 0), pipeline_mode=pl.Buffered(1)),
            pl.BlockSpec((tm, D), lambda i, j: (srow(i), 0), pipeline_mode=pl.Buffered(1)),
            pl.BlockSpec((1, D), lambda i, j: (0, 0)),
            pl.BlockSpec((tn, D), lambda i, j: (jnp.minimum(j, n_a - 1), 0)),
            pl.BlockSpec((tn, D), lambda i, j: (jnp.maximum(j - n_a, 0), 0)),
            pl.BlockSpec((2 * LANES, D), lambda i, j: (0, 0)),
            pl.BlockSpec((2 * SUBLANES, D), lambda i, j: (0, 0)),
        ],
        out_specs=[
            pl.BlockSpec((tm, tn), lambda i, j: (i, zcol(j))),
            p_spec(kcol), s_spec(kcol), p_spec(vcol), s_spec(vcol),
            pl.BlockSpec((tm, 2 * LANES), lambda i, j: (i, 0)),
            pl.BlockSpec((2 * SUBLANES, tm), lambda i, j: (0, i)),
        ],
        out_shape=[
            jax.ShapeDtypeStruct((S + BT, nzt * tn), F32),
            jax.ShapeDtypeStruct((S, QK_D), F32),
            jax.ShapeDtypeStruct((BT, QK_D), F32),
            jax.ShapeDtypeStruct((S, W_D), F32),
            jax.ShapeDtypeStruct((BT, W_D), F32),
            jax.ShapeDtypeStruct((S + BT, 2 * LANES), F32),
            jax.ShapeDtypeStruct((2 * SUBLANES, S + BT), F32),
        ],
        scratch_shapes=[pltpu.VMEM((tm, D), BF16)],
        compiler_params=_params("arbitrary", "arbitrary"),
        name="in_proj",
    )(xp, xs, norm_w, w_all, w_rest, w_gate_col, w_gate_row)


def _conv_silu(x, hist, hist_shift, cw_ref, cb_ref, row_in_seq):
    acc = cb_ref[...] + cw_ref[CONV_W - 1:CONV_W, :] * x
    for j in range(1, CONV_W):
        xr = pltpu.roll(x, j, axis=0)
        hr = pltpu.roll(hist, (j + hist_shift) % hist.shape[0], axis=0)
        if hist.shape[0] != x.shape[0]:
            first = jnp.where(row_in_seq[0:SUBLANES] < j, hr, xr[0:SUBLANES])
            xs = jnp.concatenate([first, xr[SUBLANES:]], axis=0)
        else:
            xs = jnp.where(row_in_seq < j, hr, xr)
        acc = acc + cw_ref[CONV_W - 1 - j:CONV_W - j, :] * xs
    return acc * _sigmoid(acc)


def _mlstm_intra(qb, kb, vb, mask, bt_c, bt_r, ig_r, inter_c):
    dlog = jnp.where(mask, bt_c - bt_r + ig_r, -jnp.inf)
    m_t = jnp.maximum(inter_c, jnp.max(dlog, axis=1, keepdims=True))
    dw = jnp.exp(dlog - m_t)
    iw = jnp.exp(inter_c - m_t)
    s = _dot_nt(qb, kb) * dw
    sv = _dot(s.astype(BF16), vb)
    return sv, jnp.sum(s, axis=1, keepdims=True), m_t, iw


def _head_out(num, den, m_t, o, hw_ref):
    den = jnp.maximum(jnp.abs(den), jnp.exp(-m_t))
    h = num / den
    hn = h * lax.rsqrt(jnp.mean(h * h, axis=-1, keepdims=True) + EPS) * hw_ref[...]
    return (hn * _sigmoid(o)).astype(BF16)


def _mlstm_prompt_kernel(zq_ref, zv_ref, zo_ref, gc_ref, gr_ref, cw_ref, cb_ref, bc_ref, br_ref, hw_ref,
                         h_ref, c_out, n_out, m_out, conv_out,
                         c_s, n_s, m_s, hist_s):
    c = pl.program_id(0)
    L = zq_ref.shape[0]

    @pl.when(c == 0)
    def _():
        c_s[...] = jnp.zeros_like(c_s)
        n_s[...] = jnp.zeros_like(n_s)
        m_s[...] = jnp.zeros_like(m_s)
        hist_s[...] = jnp.zeros_like(hist_s)

    x = zq_ref[...]
    row = lax.broadcasted_iota(jnp.int32, (L, 1), 0)
    qk = _conv_silu(x, hist_s[...], 0, cw_ref, cb_ref, row)
    hist_s[...] = x[L - SUBLANES:L, :]
    conv_out[...] = x[L - SUBLANES:L, :]

    gcol = gc_ref[...] + bc_ref[...]
    grow = gr_ref[...] + br_ref[...]
    ig_c = gcol[:, 0:LANES]
    lf_c = _log_sigmoid(gcol[:, LANES:2 * LANES])
    ig_r = grow[0:SUBLANES, :]
    lf_r = _log_sigmoid(grow[SUBLANES:2 * SUBLANES, :])
    ri = lax.broadcasted_iota(jnp.int32, (L, L), 0)
    ci = lax.broadcasted_iota(jnp.int32, (L, L), 1)
    mask = ci <= ri
    bt_c = _dot_exact(mask.astype(F32), lf_c)
    bt_r = _dot_exact(lf_r, (ri <= ci).astype(F32))
    m_prev = m_s[...]
    inter = bt_c + m_prev
    b_last = bt_c[L - 1:L, :]
    wlog = b_last - bt_c + ig_c
    m_new = jnp.maximum(b_last + m_prev, jnp.max(wlog, axis=0, keepdims=True))
    ws = jnp.exp(wlog - m_new)
    decay = jnp.exp(b_last + m_prev - m_new)
    m_s[...] = m_new
    m_out[...] = m_new

    for h in range(H_M):
        q = qk[:, h * DK_M:(h + 1) * DK_M]
        k = qk[:, QK_M + h * DK_M:QK_M + (h + 1) * DK_M] * (DK_M ** -0.5)
        v = zv_ref[:, h * DV_M:(h + 1) * DV_M]
        qb, kb, vb = q.astype(BF16), k.astype(BF16), v.astype(BF16)
        sv, ssum, m_t, iw = _mlstm_intra(qb, kb, vb, mask, bt_c[:, h:h + 1], bt_r[h:h + 1, :],
                                         ig_r[h:h + 1, :], inter[:, h:h + 1])
        C = c_s[h]
        n_row = n_s[h:h + 1, :]
        num = sv + iw * _dot_nt(qb, C.astype(BF16))
        den = ssum + iw * jnp.sum(q * n_row, axis=1, keepdims=True)
        h_ref[:, h * DV_M:(h + 1) * DV_M] = _head_out(num, den, m_t, zo_ref[:, h * DV_M:(h + 1) * DV_M], hw_ref)
        ws_h = ws[:, h:h + 1]
        dc = decay[:, h:h + 1]
        c_new = dc * C + _dot_tn((v * ws_h).astype(BF16), kb)
        n_new = dc * n_row + jnp.sum(ws_h * k, axis=0, keepdims=True)
        c_s[h] = c_new
        n_s[h:h + 1, :] = n_new
        c_out[h] = c_new
        n_out[h:h + 1, :] = n_new


def _mlstm_prompt(z, gc, gr, conv_w, conv_b, bias_c, bias_r, hnorm_w, S):
    L = _pick(S, (256, 128))
    nz = lambda col, width: col // width
    return pl.pallas_call(
        _mlstm_prompt_kernel,
        grid=(S // L,),
        in_specs=[
            pl.BlockSpec((L, 2 * QK_M), lambda c: (c, nz(ZC_QK, 2 * QK_M))),
            pl.BlockSpec((L, W_M), lambda c: (c, nz(ZC_VM, W_M))),
            pl.BlockSpec((L, W_M), lambda c: (c, nz(ZC_OM, W_M))),
            pl.BlockSpec((L, 2 * LANES), lambda c: (c, 0)),
            pl.BlockSpec((2 * SUBLANES, L), lambda c: (0, c)),
            pl.BlockSpec((CONV_W, 2 * QK_M), lambda c: (0, 0)),
            pl.BlockSpec((1, 2 * QK_M), lambda c: (0, 0)),
            pl.BlockSpec((1, 2 * LANES), lambda c: (0, 0)),
            pl.BlockSpec((2 * SUBLANES, 1), lambda c: (0, 0)),
            pl.BlockSpec((1, DV_M), lambda c: (0, 0)),
        ],
        out_specs=[
            pl.BlockSpec((L, W_M), lambda c: (c, 0)),
            pl.BlockSpec((H_M, DV_M, DK_M), lambda c: (0, 0, 0)),
            pl.BlockSpec((H_M, DK_M), lambda c: (0, 0)),
            pl.BlockSpec((1, LANES), lambda c: (0, 0)),
            pl.BlockSpec((SUBLANES, 2 * QK_M), lambda c: (0, 0)),
        ],
        out_shape=[
            jax.ShapeDtypeStruct((S, W_M), BF16),
            jax.ShapeDtypeStruct((H_M, DV_M, DK_M), F32),
            jax.ShapeDtypeStruct((H_M, DK_M), F32),
            jax.ShapeDtypeStruct((1, LANES), F32),
            jax.ShapeDtypeStruct((SUBLANES, 2 * QK_M), F32),
        ],
        scratch_shapes=[
            pltpu.VMEM((H_M, DV_M, DK_M), F32),
            pltpu.VMEM((H_M, DK_M), F32),
            pltpu.VMEM((1, LANES), F32),
            pltpu.VMEM((SUBLANES, 2 * QK_M), F32),
        ],
        compiler_params=_params("arbitrary"),
        name="mlstm_prompt",
    )(z, z, z, gc, gr, conv_w, conv_b, bias_c, bias_r, hnorm_w)


def _mlstm_sample_kernel(zq_ref, zv_ref, zo_ref, gc_ref, gr_ref, hist_ref, c0_ref, n0_ref, m0_ref,
                         cw_ref, cb_ref, bc_ref, br_ref, hw_ref,
                         h_ref, c_out, n_out, m_out, *, T):
    L = zq_ref.shape[0]
    NB = L // T
    x = zq_ref[...]
    ri = lax.broadcasted_iota(jnp.int32, (L, L), 0)
    ci = lax.broadcasted_iota(jnp.int32, (L, L), 1)
    same = (ri // T) == (ci // T)
    mask = same & (ci <= ri)
    row_t = lax.broadcasted_iota(jnp.int32, (L, 1), 0) % T
    qk = _conv_silu(x, hist_ref[...], L - T, cw_ref, cb_ref, row_t)

    gcol = gc_ref[...] + bc_ref[...]
    grow = gr_ref[...] + br_ref[...]
    ig_c = gcol[:, 0:LANES]
    lf_c = _log_sigmoid(gcol[:, LANES:2 * LANES])
    ig_r = grow[0:SUBLANES, :]
    lf_r = _log_sigmoid(grow[SUBLANES:2 * SUBLANES, :])
    bt_c = _dot_exact(mask.astype(F32), lf_c)
    bt_r = _dot_exact(lf_r, (same & (ri <= ci)).astype(F32))
    m_prev = m0_ref[...]
    last = same & (ci % T == T - 1)
    b_last = _dot_exact(last.astype(F32), bt_c)
    inter = bt_c + m_prev
    wlog = b_last - bt_c + ig_c
    wmax = jnp.max(wlog.reshape(NB, T, LANES), axis=1, keepdims=True)
    wmax = jnp.broadcast_to(wmax, (NB, T, LANES)).reshape(L, LANES)
    m_new = jnp.maximum(b_last + m_prev, wmax)
    ws = jnp.exp(wlog - m_new)
    decay = jnp.exp(b_last + m_prev - m_new)
    m_out[...] = m_new

    lane_seq = lax.broadcasted_iota(jnp.int32, (L, NB * DV_M), 1) // DV_M
    row_seq = lax.broadcasted_iota(jnp.int32, (L, NB * DV_M), 0) // T
    blockdiag = lane_seq == row_seq

    for h in range(H_M):
        q = qk[:, h * DK_M:(h + 1) * DK_M]
        k = qk[:, QK_M + h * DK_M:QK_M + (h + 1) * DK_M] * (DK_M ** -0.5)
        v = zv_ref[:, h * DV_M:(h + 1) * DV_M]
        qb, kb, vb = q.astype(BF16), k.astype(BF16), v.astype(BF16)
        sv, ssum, m_t, iw = _mlstm_intra(qb, kb, vb, mask, bt_c[:, h:h + 1], bt_r[h:h + 1, :],
                                         ig_r[h:h + 1, :], inter[:, h:h + 1])
        C = c0_ref[:, h]
        c_flat = C.reshape(NB * DV_M, DK_M)
        qc_all = _dot_nt(qb, c_flat.astype(BF16))
        qc = jnp.concatenate([qc_all[b * T:(b + 1) * T, b * DV_M:(b + 1) * DV_M] for b in range(NB)], axis=0)
        n_rows = jnp.broadcast_to(n0_ref[:, h:h + 1, :], (NB, T, DK_M)).reshape(L, DK_M)
        num = sv + iw * qc
        den = ssum + iw * jnp.sum(q * n_rows, axis=1, keepdims=True)
        h_ref[:, h * DV_M:(h + 1) * DV_M] = _head_out(num, den, m_t, zo_ref[:, h * DV_M:(h + 1) * DV_M], hw_ref)
        ws_h = ws[:, h:h + 1]
        vw = v * ws_h
        vw_exp = jnp.where(blockdiag, jnp.concatenate([vw] * NB, axis=1), 0.0).astype(BF16)
        upd = _dot_tn(vw_exp, kb).reshape(NB, DV_M, DK_M)
        dc = decay[:, h:h + 1].reshape(NB, T, 1)[:, 0:1, :]
        c_out[:, h] = dc * C + upd
        kw = (ws_h * k).reshape(NB, T, DK_M)
        n_out[:, h:h + 1, :] = dc * n0_ref[:, h:h + 1, :] + jnp.sum(kw, axis=1, keepdims=True)


def _mlstm_sample(z, gc, gr, hist, c0, n0, m0p, conv_w, conv_b, bias_c, bias_r, hnorm_w, S, B, T):
    NB = SEQ_BLOCK
    L = NB * T
    assert L == LANES and B % NB == 0 and S % L == 0
    r0 = S // L
    nz = lambda col, width: col // width
    return pl.pallas_call(
        functools.partial(_mlstm_sample_kernel, T=T),
        grid=(B // NB,),
        in_specs=[
            pl.BlockSpec((L, 2 * QK_M), lambda i: (r0 + i, nz(ZC_QK, 2 * QK_M))),
            pl.BlockSpec((L, W_M), lambda i: (r0 + i, nz(ZC_VM, W_M))),
            pl.BlockSpec((L, W_M), lambda i: (r0 + i, nz(ZC_OM, W_M))),
            pl.BlockSpec((L, 2 * LANES), lambda i: (r0 + i, 0)),
            pl.BlockSpec((2 * SUBLANES, L), lambda i: (0, r0 + i)),
            pl.BlockSpec((L, 2 * QK_M), lambda i: (i, 0)),
            pl.BlockSpec((NB, H_M, DV_M, DK_M), lambda i: (i, 0, 0, 0)),
            pl.BlockSpec((NB, H_M, DK_M), lambda i: (i, 0, 0)),
            pl.BlockSpec((L, LANES), lambda i: (i, 0)),
            pl.BlockSpec((CONV_W, 2 * QK_M), lambda i: (0, 0)),
            pl.BlockSpec((1, 2 * QK_M), lambda i: (0, 0)),
            pl.BlockSpec((1, 2 * LANES), lambda i: (0, 0)),
            pl.BlockSpec((2 * SUBLANES, 1), lambda i: (0, 0)),
            pl.BlockSpec((1, DV_M), lambda i: (0, 0)),
        ],
        out_specs=[
            pl.BlockSpec((L, W_M), lambda i: (i, 0)),
            pl.BlockSpec((NB, H_M, DV_M, DK_M), lambda i: (i, 0, 0, 0)),
            pl.BlockSpec((NB, H_M, DK_M), lambda i: (i, 0, 0)),
            pl.BlockSpec((L, LANES), lambda i: (i, 0)),
        ],
        out_shape=[
            jax.ShapeDtypeStruct((B * T, W_M), BF16),
            jax.ShapeDtypeStruct((B, H_M, DV_M, DK_M), F32),
            jax.ShapeDtypeStruct((B, H_M, DK_M), F32),
            jax.ShapeDtypeStruct((B * T, LANES), F32),
        ],
        compiler_params=_params("parallel"),
        name="mlstm_sample",
    )(z, z, z, gc, gr, hist, c0, n0, m0p, conv_w, conv_b, bias_c, bias_r, hnorm_w)


def _lambda(lq1, lk1, lq2, lk2):
    a = jnp.sum(lq1[...] * lk1[...], axis=-1, keepdims=True)
    b = jnp.sum(lq2[...] * lk2[...], axis=-1, keepdims=True)
    return jnp.exp(a) - jnp.exp(b) + LAM_INIT


def _subln(att, w_ref):
    y = att * lax.rsqrt(jnp.mean(att * att, axis=-1, keepdims=True) + EPS) * w_ref[...]
    return (y * (1.0 - LAM_INIT)).astype(BF16)


def _attn_prompt_body(qi, q_ref, k_ref, v_ref, b0_ref, b1_ref, lq1, lk1, lq2, lk2, sw_ref,
                      o_ref, kb_s, vt_s, m_s, acc_s):
    T = q_ref.shape[0]
    n_tiles = kb_s.shape[0]

    @pl.when(qi == 0)
    def _():
        for t in range(n_tiles):
            kb_s[t] = k_ref[t * T:(t + 1) * T, :].astype(BF16)
            vt_s[t, 0:DV_D, :] = v_ref[t * T:(t + 1) * T, :].T.astype(BF16)
            vt_s[t, DV_D:, :] = jnp.ones((ONES_ROWS, T), BF16)

    q = q_ref[...] * (DK_D ** -0.5 * LOG2E)
    lane = lax.broadcasted_iota(jnp.int32, q.shape, 1)
    qpad = (jnp.where(lane < DK_D, q, 0.0).astype(BF16), jnp.where(lane >= DK_D, q, 0.0).astype(BF16))

    def group(tiles, state):
        scores = []
        for kj, bias in tiles:
            kt = kb_s[kj]
            for c in range(2):
                s = _dot_nt(kt, qpad[c])
                scores.append(s if bias is None else s + bias)
        parts = ([], [])
        for t, (kj, _) in enumerate(tiles):
            vt = vt_s[kj]
            for c in range(2):
                s = scores[2 * t + c]
                m = jnp.max(s, axis=0, keepdims=True)
                parts[c].append((m, _dot(vt, jnp.exp2(s - m).astype(BF16))))
        out = []
        for c in range(2):
            m_old, acc_old = state[c]
            m_new = m_old
            for m, _ in parts[c]:
                m_new = jnp.maximum(m_new, m)
            acc_new = jnp.exp2(m_old - m_new) * acc_old
            for m, pv in parts[c]:
                acc_new = acc_new + jnp.exp2(m - m_new) * pv
            out.append((m_new, acc_new))
        return tuple(out)

    def load():
        return tuple((m_s[c], acc_s[c]) for c in range(2))

    def store(state):
        for c in range(2):
            m_s[c], acc_s[c] = state[c]

    n_all = qi + 1
    n_head = functools.reduce(lambda acc, G: jnp.where(n_all >= G, jnp.maximum(acc, G), acc), HEAD_GROUPS, 0)
    n_far = n_all - n_head
    state = tuple((jnp.full((1, T), -jnp.inf, F32), jnp.zeros((DV_D + ONES_ROWS, T), F32)) for _ in range(2))
    done = 0
    for G in FAR_GROUPS:
        n_grp = (n_far - done) // G
        state = lax.fori_loop(0, n_grp, lambda g, st, G=G, done=done: group(
            [(done + g * G + t, None) for t in range(G)], st), state)
        done = done + n_grp * G
    store(state)

    for G in HEAD_GROUPS:
        @pl.when(n_head == G)
        def _(G=G):
            bias = [None] * (G - 2) + [b1_ref[0], b0_ref[0]]
            store(group([(qi - (G - 1) + t, bias[-G:][t]) for t in range(G)], load()))

    (_, a0), (_, a1) = load()
    lam = _lambda(lq1, lk1, lq2, lk2)
    att_t = a0[0:DV_D] / a0[DV_D:DV_D + 1] - lam * (a1[0:DV_D] / a1[DV_D:DV_D + 1])
    o_ref[...] = _subln(att_t.T, sw_ref)


def _attn_prompt_kernel(*refs):
    _attn_prompt_body(pl.program_id(1), *refs)


def _attn_prompt_specs(S, T, extra=()):
    hw = 2 * DK_D
    in_specs = [
        pl.BlockSpec((T, hw), lambda h, i, *_: (i, ZC_QD // hw + h)),
        pl.BlockSpec((S, hw), lambda h, i, *_: (0, h)),
        pl.BlockSpec((S, DV_D), lambda h, i, *_: (0, h)),
        pl.BlockSpec((1, T, T), lambda h, i, *_: (h, 0, 0)),
        pl.BlockSpec((1, T, T), lambda h, i, *_: (h, 0, 0)),
    ]
    out_spec = pl.BlockSpec((T, DV_D), lambda h, i, *_: (i, h))
    scratch = [
        pltpu.VMEM((S // T, T, hw), BF16),
        pltpu.VMEM((S // T, DV_D + ONES_ROWS, T), BF16),
        pltpu.VMEM((2, 1, T), F32),
        pltpu.VMEM((2, DV_D + ONES_ROWS, T), F32),
    ]
    return in_specs, out_spec, scratch


def _small_specs(shapes):
    return [pl.BlockSpec(shape, lambda *_, n=len(shape): (0,) * n) for shape in shapes]


_LAMBDA_AND_SUBLN = [(1, DK_D)] * 4 + [(1, DV_D)]


def _attn_prompt(z, kp, vp, bias0, bias1, lq1, lk1, lq2, lk2, subln_w, S):
    T = bias0.shape[-1]
    in_specs, out_spec, scratch = _attn_prompt_specs(S, T)
    return pl.pallas_call(
        _attn_prompt_kernel,
        grid=(H_D, S // T),
        in_specs=in_specs + _small_specs(_LAMBDA_AND_SUBLN),
        out_specs=out_spec,
        out_shape=jax.ShapeDtypeStruct((S, W_D), BF16),
        scratch_shapes=scratch,
        compiler_params=_params("arbitrary", "arbitrary"),
        name="attn_prompt",
    )(z, kp, vp, bias0, bias1, lq1, lk1, lq2, lk2, subln_w)


def _head_rows(ref, h):
    n_keys, n_heads, width = ref.shape
    return ref.reshape(n_keys * n_heads, width)[pl.ds(h, n_keys, stride=n_heads), :]


def _attn_sample_body(p, n_steps, q_ref, *refs, pps):
    kc = refs[0:pps]
    vc = refs[pps:2 * pps]
    (kn_ref, vn_ref, blast_ref, bnew_ref, lq1, lk1, lq2, lk2, sw_ref,
     o_ref, qbd_s, m_s, acc_s) = refs[2 * pps:]
    last_step = p == n_steps - 1
    T = q_ref.shape[2]
    PG = kc[0].shape[0]

    @pl.when(p == 0)
    def _():
        lane = lax.broadcasted_iota(jnp.int32, (T, 2 * DK_D), 1)
        for h in range(H_D):
            q = q_ref[0, h] * (DK_D ** -0.5 * LOG2E)
            qbd_s[h, 0:T, :] = jnp.where(lane < DK_D, q, 0.0).astype(BF16)
            qbd_s[h, T:2 * T, :] = jnp.where(lane >= DK_D, q, 0.0).astype(BF16)
        m_s[...] = jnp.full_like(m_s, -jnp.inf)
        acc_s[...] = jnp.zeros_like(acc_s)

    def update(keys, values, biases):
        scores = []
        for h in range(H_D):
            s = _dot_nt(qbd_s[h], keys[h])
            scores.append(s if biases[h] is None else s + biases[h])
        for h in range(H_D):
            s = scores[h]
            m_old = m_s[h]
            m_new = jnp.maximum(m_old, jnp.max(s, axis=1, keepdims=True))
            pr = jnp.exp2(s - m_new).astype(BF16)
            v = values[h]
            v_ext = jnp.concatenate([v.astype(BF16), jnp.ones(v.shape, BF16)], axis=1)
            acc_s[h] = jnp.exp2(m_old - m_new) * acc_s[h] + _dot(pr, v_ext)
            m_s[h] = m_new

    pad = jnp.zeros((2 * T, (pps - 1) * PG), F32)
    keys, values, biases = [], [], []
    for h in range(H_D):
        keys.append(jnp.concatenate([_head_rows(kc[i], h) for i in range(pps)], axis=0).astype(BF16))
        values.append(jnp.concatenate([_head_rows(vc[i], h) for i in range(pps)], axis=0))
        b_end = jnp.where(last_step, blast_ref[h], 0.0)
        biases.append(jnp.concatenate([pad, b_end], axis=1) if pps > 1 else b_end)
    update(keys, values, biases)

    @pl.when(last_step)
    def _():
        zeros = jnp.zeros((PG - T, 2 * DK_D), F32)
        update([jnp.concatenate([kn_ref[0, h], zeros], axis=0).astype(BF16) for h in range(H_D)],
               [jnp.concatenate([vn_ref[0, h], zeros], axis=0) for h in range(H_D)],
               [bnew_ref[h] for h in range(H_D)])
        lam = _lambda(lq1, lk1, lq2, lk2)
        for h in range(H_D):
            r = acc_s[h, :, 0:DV_D] / acc_s[h, :, DV_D:DV_D + 1]
            o_ref[0, h] = _subln(r[0:T] - lam * r[T:2 * T], sw_ref)


def _attn_sample_kernel(pt_ref, *refs, pps):
    _attn_sample_body(pl.program_id(1), pl.num_programs(1), *refs, pps=pps)


def _attn_sample_specs(qs, blast, bnew, PG, pps, seq_step):
    B, _, T, _ = qs.shape

    def per_seq(width):
        return pl.BlockSpec((1, H_D, T, width), lambda *g: (seq_step(*g[:-1])[0], 0, 0, 0))

    def page_spec(i, width):
        def index(*g):
            b, p = seq_step(*g[:-1])
            return (0, g[-1][b, p * pps + i], 0, 0, 0)
        return pl.BlockSpec((None, None, PG, H_D, width), index)

    in_specs = ([per_seq(2 * DK_D)]
                + [page_spec(i, 2 * DK_D) for i in range(pps)]
                + [page_spec(i, DV_D) for i in range(pps)]
                + [per_seq(2 * DK_D), per_seq(DV_D)]
                + _small_specs([blast.shape, bnew.shape]))
    scratch = [
        pltpu.VMEM((H_D, 2 * T, 2 * DK_D), BF16),
        pltpu.VMEM((H_D, 2 * T, 1), F32),
        pltpu.VMEM((H_D, 2 * T, 2 * DV_D), F32),
    ]
    return in_specs, per_seq(DV_D), scratch


def _attn_sample(page_table, qs, cache_k, cache_v, kn, vn, blast, bnew, lq1, lk1, lq2, lk2, subln_w, pps):
    B, _, T, _ = qs.shape
    n_pages = page_table.shape[1]
    in_specs, out_spec, scratch = _attn_sample_specs(qs, blast, bnew, cache_k.shape[2], pps, lambda b, p: (b, p))
    grid_spec = pltpu.PrefetchScalarGridSpec(
        num_scalar_prefetch=1,
        grid=(B, n_pages // pps),
        in_specs=in_specs + _small_specs(_LAMBDA_AND_SUBLN),
        out_specs=out_spec,
        scratch_shapes=scratch,
    )
    return pl.pallas_call(
        functools.partial(_attn_sample_kernel, pps=pps),
        grid_spec=grid_spec,
        out_shape=jax.ShapeDtypeStruct((B, H_D, T, DV_D), BF16),
        compiler_params=_params("arbitrary", "arbitrary"),
        name="attn_sample",
    )(page_table, qs, *([cache_k] * pps), *([cache_v] * pps), kn, vn, blast, bnew,
      lq1, lk1, lq2, lk2, subln_w)


def _attn_fused_kernel(pt_ref, *refs, pps, n_sp, n_sample_in):
    n_prompt_in = 5
    p_in = refs[0:n_prompt_in]
    s_in = refs[n_prompt_in:n_prompt_in + n_sample_in]
    shared = refs[n_prompt_in + n_sample_in:n_prompt_in + n_sample_in + 5]
    o_p, o_s = refs[n_prompt_in + n_sample_in + 5:n_prompt_in + n_sample_in + 7]
    scratch = refs[n_prompt_in + n_sample_in + 7:]
    qi = pl.program_id(1)
    sid = pl.program_id(0) * pl.num_programs(1) + qi
    _attn_prompt_body(qi, *p_in, *shared, o_p, *scratch[0:4])
    _attn_sample_body(sid % n_sp, n_sp, *s_in, *shared, o_s, *scratch[4:], pps=pps)


def _attn_fused(page_table, z, kp, vp, bias0, bias1, qs, cache_k, cache_v, kn, vn, blast, bnew,
                lq1, lk1, lq2, lk2, subln_w, S, pps):
    T = bias0.shape[-1]
    B, _, TS, _ = qs.shape
    nq = S // T
    n_sp = page_table.shape[1] // pps
    assert H_D * nq == B * n_sp
    p_specs, p_out, p_scratch = _attn_prompt_specs(S, T)
    seq_step = lambda h, i: ((h * nq + i) // n_sp, (h * nq + i) % n_sp)
    s_specs, s_out, s_scratch = _attn_sample_specs(qs, blast, bnew, cache_k.shape[2], pps, seq_step)
    grid_spec = pltpu.PrefetchScalarGridSpec(
        num_scalar_prefetch=1,
        grid=(H_D, nq),
        in_specs=p_specs + s_specs + _small_specs(_LAMBDA_AND_SUBLN),
        out_specs=[p_out, s_out],
        scratch_shapes=p_scratch + s_scratch,
    )
    return pl.pallas_call(
        functools.partial(_attn_fused_kernel, pps=pps, n_sp=n_sp, n_sample_in=len(s_specs)),
        grid_spec=grid_spec,
        out_shape=[jax.ShapeDtypeStruct((S, W_D), BF16), jax.ShapeDtypeStruct((B, H_D, TS, DV_D), BF16)],
        compiler_params=_params("arbitrary", "arbitrary"),
        name="attn_fused",
    )(page_table, z, kp, vp, bias0, bias1, qs, *([cache_k] * pps), *([cache_v] * pps), kn, vn, blast, bnew,
      lq1, lk1, lq2, lk2, subln_w)


def _merge_kernel(hmp_ref, hms_ref, atp_ref, ats_ref, wa_ref, wb_ref, ga_ref, gb_ref, u_ref, *, n_prompt):
    def body(hm_ref, at_ref):
        ya = _dot(hm_ref[...], wa_ref[...])
        yb = _dot(at_ref[...], wb_ref[...])
        u_ref[...] = (_sigmoid(ga_ref[...]) * ya + _sigmoid(gb_ref[...]) * yb).astype(BF16)

    @pl.when(pl.program_id(0) < n_prompt)
    def _():
        body(hmp_ref, atp_ref)

    @pl.when(pl.program_id(0) >= n_prompt)
    def _():
        body(hms_ref, ats_ref)


def _merge(hm_p, hm_s, att_p, att_s, w_a, w_b, z, D):
    S, BT = hm_p.shape[0], hm_s.shape[0]
    tm, n_prompt, n_sample, prow, srow = _two_way_rows(S, BT, (1024, 512, 256, 128))
    tn = _pick(D, (1024, 512, 256, 128))
    ga0, gb0 = ZC_GA // tn, (ZC_GA + D) // tn
    return pl.pallas_call(
        functools.partial(_merge_kernel, n_prompt=n_prompt),
        grid=(n_prompt + n_sample, D // tn),
        in_specs=[
            pl.BlockSpec((tm, W_M), lambda i, j: (prow(i), 0)),
            pl.BlockSpec((tm, W_M), lambda i, j: (srow(i), 0)),
            pl.BlockSpec((tm, W_D), lambda i, j: (prow(i), 0)),
            pl.BlockSpec((tm, W_D), lambda i, j: (srow(i), 0)),
            pl.BlockSpec((W_M, tn), lambda i, j: (0, j)),
            pl.BlockSpec((W_D, tn), lambda i, j: (0, j)),
            pl.BlockSpec((tm, tn), lambda i, j: (i, ga0 + j)),
            pl.BlockSpec((tm, tn), lambda i, j: (i, gb0 + j)),
        ],
        out_specs=pl.BlockSpec((tm, tn), lambda i, j: (i, j)),
        out_shape=jax.ShapeDtypeStruct((S + BT, D), BF16),
        compiler_params=_params("parallel", "parallel"),
        name="merge",
    )(hm_p, hm_s, att_p, att_s, w_a, w_b, z, z)


def _out_proj_kernel(u_ref, w_ref, xp_ref, xs_ref, o_ref, *, n_prompt):
    y = _dot(u_ref[...], w_ref[...])

    @pl.when(pl.program_id(0) < n_prompt)
    def _():
        o_ref[...] = xp_ref[...] + y

    @pl.when(pl.program_id(0) >= n_prompt)
    def _():
        o_ref[...] = xs_ref[...] + y


def _out_proj(u, w_out, xp, xs):
    S, D = xp.shape
    BT = xs.shape[0]
    tm, n_prompt, n_sample, prow, srow = _two_way_rows(S, BT, (1024, 512, 256, 128))
    tn = _pick(D, (1024, 512, 256, 128))
    return pl.pallas_call(
        functools.partial(_out_proj_kernel, n_prompt=n_prompt),
        grid=(n_prompt + n_sample, D // tn),
        in_specs=[
            pl.BlockSpec((tm, D), lambda i, j: (i, 0)),
            pl.BlockSpec((D, tn), lambda i, j: (0, j)),
            pl.BlockSpec((tm, tn), lambda i, j: (prow(i), jnp.where(i < n_prompt, j, D // tn - 1))),
            pl.BlockSpec((tm, tn), lambda i, j: (srow(i), jnp.where(i < n_prompt, 0, j))),
        ],
        out_specs=pl.BlockSpec((tm, tn), lambda i, j: (i, j)),
        out_shape=jax.ShapeDtypeStruct((S + BT, D), F32),
        compiler_params=_params("parallel", "parallel"),
        name="out_proj",
    )(u, w_out, xp, xs)


def _ffn_kernel(x_ref, nw_ref, w1_ref, w2_ref, fw_ref, yp_ref, ys_ref, xn_s, acc_s, *, n_prompt):
    f = pl.program_id(1)

    @pl.when(f == 0)
    def _():
        x = x_ref[...]
        ms = jnp.mean(x * x, axis=-1, keepdims=True)
        xn_s[...] = (x * lax.rsqrt(ms + EPS) * nw_ref[...]).astype(BF16)
        acc_s[...] = jnp.zeros_like(acc_s)

    hid = jnp.maximum(_dot(xn_s[...], w1_ref[...]), 0.0)
    acc_s[...] += _dot((hid * hid).astype(BF16), w2_ref[...])

    def final(y_ref):
        x2 = x_ref[...] + acc_s[...]
        ms = jnp.mean(x2 * x2, axis=-1, keepdims=True)
        y_ref[...] = x2 * lax.rsqrt(ms + EPS) * fw_ref[...]

    last = f == pl.num_programs(1) - 1

    @pl.when(last & (pl.program_id(0) < n_prompt))
    def _():
        final(yp_ref)

    @pl.when(last & (pl.program_id(0) >= n_prompt))
    def _():
        final(ys_ref)


def _ffn(x, norm_w, w1, w2, final_w, S):
    R, D = x.shape
    DF = w1.shape[1]
    tm, n_prompt, n_sample, prow, srow = _two_way_rows(S, R - S, (512, 256, 128))
    tf = _pick(DF, (1024, 512, 256, 128))
    return pl.pallas_call(
        functools.partial(_ffn_kernel, n_prompt=n_prompt),
        grid=(n_prompt + n_sample, DF // tf),
        in_specs=[
            pl.BlockSpec((tm, D), lambda i, f: (i, 0)),
            pl.BlockSpec((1, D), lambda i, f: (0, 0)),
            pl.BlockSpec((D, tf), lambda i, f: (0, f)),
            pl.BlockSpec((tf, D), lambda i, f: (f, 0)),
            pl.BlockSpec((1, D), lambda i, f: (0, 0)),
        ],
        out_specs=[
            pl.BlockSpec((tm, D), lambda i, f: (prow(i), 0)),
            pl.BlockSpec((tm, D), lambda i, f: (srow(i), 0)),
        ],
        out_shape=[jax.ShapeDtypeStruct((S, D), F32), jax.ShapeDtypeStruct((R - S, D), F32)],
        scratch_shapes=[pltpu.VMEM((tm, D), BF16), pltpu.VMEM((tm, D), F32)],
        compiler_params=_params("arbitrary", "arbitrary"),
        name="ffn",
    )(x, norm_w, w1, w2, final_w)


def _bias_by_distance(rel_bias, n):
    d = jnp.arange(n, dtype=jnp.int32)
    max_exact = N_BUCKETS // 2
    nf = jnp.maximum(d, 1).astype(F32)
    large = max_exact + jnp.floor(jnp.log(nf / max_exact) / math.log(MAX_DIST / max_exact)
                                  * (N_BUCKETS - max_exact))
    large = jnp.minimum(large, N_BUCKETS - 1.0)
    bucket = jnp.where(d < max_exact, d.astype(F32), large)
    onehot = (bucket[:, None] == jnp.arange(N_BUCKETS, dtype=F32)[None, :]).astype(F32)
    return jnp.dot(onehot, rel_bias.astype(F32), precision=HIGHEST).T


def _toeplitz(w, rows, cols):
    n = w.shape[-1]
    assert cols <= n - 1
    lead = w.shape[:-1]
    flat = jnp.tile(w, (1,) * len(lead) + (rows,))[..., :rows * (n - 1)]
    return flat.reshape(lead + (rows, n - 1))[..., :cols]


def _prompt_bias_tiles(rel_bias, T):
    assert T + 1 >= MAX_DIST
    bd = _bias_by_distance(rel_bias, 2 * T)
    val = (bd - bd[:, 2 * T - 1:]) * LOG2E
    neg = jnp.full((H_D, T), NEG, F32)
    t0 = _toeplitz(jnp.concatenate([val[:, :T], neg], axis=1), T, T)
    t1 = _toeplitz(jnp.concatenate([val[:, T:], val[:, :T]], axis=1), T, T)
    return t0, t1


def _sample_bias_tables(rel_bias, T, PG):
    assert PG + 1 >= MAX_DIST
    bd = _bias_by_distance(rel_bias, 2 * PG + T)
    bd = (bd - bd[:, 2 * PG + T - 1:]) * LOG2E
    w_last = jnp.concatenate([bd[:, PG:0:-1], bd[:, :1], bd[:, PG + T - 1:PG:-1]], axis=1)
    last = _toeplitz(w_last, T, PG)
    w_new = jnp.concatenate([bd[:, :1], jnp.full((H_D, T), NEG, F32), bd[:, T - 1:0:-1]], axis=1)
    new = jnp.concatenate([_toeplitz(w_new, T, T), jnp.full((H_D, T, PG - T), NEG, F32)], axis=2)
    both_maps = lambda t: jnp.concatenate([t, t], axis=1).astype(F32)
    return both_maps(last), both_maps(new)


def kernel(x_prompt, x_sample, cache_k, cache_v, page_table, state_C, state_n, state_m, state_conv,
           norm1_w, w_in, b_i, b_f, conv_w, conv_b, hnorm_w, lambda_q1, lambda_k1, lambda_q2, lambda_k2,
           subln_w, rel_bias, w_a, w_b, w_out, norm2_w, w_ff1, w_ff2, final_norm_w):
    assert w_in.shape[0] == 1 and x_prompt.shape[0] == 1
    _, S, D = x_prompt.shape
    B, T, _ = x_sample.shape
    PG = cache_k.shape[2]
    xp = x_prompt[0]
    xs = x_sample.reshape(B * T, D)

    o_i = 2 * QK_M + 2 * W_M
    o_qd = o_i + 2 * H_M
    w_t = jnp.transpose(w_in[0])
    w_all = w_t.astype(BF16)
    w_rest = w_all[o_qd:]
    w_gate_row = w_all[o_i:o_qd]
    gate_pad = jnp.zeros((LANES - H_M, D), BF16)
    w_gate_col = jnp.concatenate([w_gate_row[:H_M], gate_pad, w_gate_row[H_M:], gate_pad], axis=0)
    bias_c = jnp.zeros((1, 2 * LANES), F32).at[0, 0:H_M].set(b_i[0]).at[0, LANES:LANES + H_M].set(b_f[0])
    bias_r = jnp.concatenate([b_i[0], b_f[0]])[:, None]

    z, kp, ks, vp, vs, gc, gr = _in_proj(xp, xs, norm1_w, w_all, w_rest, w_gate_col, w_gate_row)

    hm_p, c_p, n_p, m_p, conv_p = _mlstm_prompt(z, gc, gr, conv_w[0], conv_b, bias_c, bias_r, hnorm_w, S)
    hist = jnp.pad(state_conv[0], ((0, 0), (T - (CONV_W - 1), 0), (0, 0))).reshape(B * T, 2 * QK_M)
    m0p = jnp.repeat(jnp.pad(state_m[0], ((0, 0), (0, LANES - H_M))), T, axis=0)
    hm_s, c_s, n_s, m_s = _mlstm_sample(z, gc, gr, hist, state_C[0], state_n[0], m0p, conv_w[0], conv_b,
                                        bias_c, bias_r, hnorm_w, S, B, T)

    TQ = _pick(S, (256, 128))
    t0, t1 = _prompt_bias_tiles(rel_bias, TQ)
    lq1, lk1, lq2, lk2 = lambda_q1, lambda_k1, lambda_q2, lambda_k2
    blast, bnew = _sample_bias_tables(rel_bias, T, PG)
    per_head = lambda a, width: jnp.transpose(a.reshape(B, T, H_D, width), (0, 2, 1, 3))
    qs = per_head(z[S:, ZC_QD:ZC_QD + QK_D], 2 * DK_D)
    kn = per_head(ks, 2 * DK_D)
    vn = per_head(vs, DV_D)
    n_pages = page_table.shape[1]
    pps = _pick(n_pages, (PAGES_PER_STEP, 4, 2, 1))
    if H_D * (S // TQ) == B * (n_pages // pps):
        att_p, att_s = _attn_fused(page_table, z, kp, vp, t0, t1, qs, cache_k, cache_v, kn, vn, blast, bnew,
                                   lq1, lk1, lq2, lk2, subln_w, S, pps)
    else:
        att_p = _attn_prompt(z, kp, vp, t0, t1, lq1, lk1, lq2, lk2, subln_w, S)
        att_s = _attn_sample(page_table, qs, cache_k, cache_v, kn, vn, blast, bnew,
                             lq1, lk1, lq2, lk2, subln_w, pps)
    att_s = jnp.transpose(att_s, (0, 2, 1, 3))

    u = _merge(hm_p, hm_s, att_p, att_s.reshape(B * T, W_D), w_a[0].astype(BF16), w_b[0].astype(BF16), z, D)
    x1 = _out_proj(u, w_out[0].astype(BF16), xp, xs)
    y_p, y_s = _ffn(x1, norm2_w, w_ff1[0].astype(BF16), w_ff2[0].astype(BF16), final_norm_w[None, :], S)

    conv_prompt = conv_p[SUBLANES - (CONV_W - 1):].reshape(1, 1, CONV_W - 1, 2 * QK_M)
    conv_sample = z[S:, :2 * QK_M].reshape(B, T, 2 * QK_M)[:, T - (CONV_W - 1):][None]
    return (y_p.reshape(1, S, D), y_s.reshape(B, T, D),
            kp.reshape(1, 1, S, H_D, 2 * DK_D), vp.reshape(1, 1, S, H_D, DV_D),
            c_p[None, None], n_p[None, None], m_p[:, :H_M][None], conv_prompt,
            ks.reshape(1, B, T, H_D, 2 * DK_D), vs.reshape(1, B, T, H_D, DV_D),
            c_s[None], n_s[None], m_s[::T, :H_M][None], conv_sample)
```
